```python
import jax, jax.numpy as jnp
from jax import lax
import numpy as np

D_MODEL = 1024
BATCH = 2
SEQ = 8192
DEPTH = 2

D_MIX = D_MODEL
N_MIXERS = 4
GROUP_WIDTH = D_MIX // N_MIXERS
N_GROUP_HEADS = 4
HEAD_DIM = GROUP_WIDTH // N_GROUP_HEADS
N_IN_SLICES = 10
D_IN = N_IN_SLICES * GROUP_WIDTH
LRU_CONV = 4
LRU_C = 8.0
HGRN_CHUNK = 64
POOL_WINDOWS = (2, 4, 8, 16)
SCONV_WIDTH = 3
D_FF = 2816
N_EXPERTS = 8
TOP_K = 2
D_FF_EXPERT = 3584
ROW_BLOCK = 128
NORM_EPS = 1e-6
HEAD_NORM_EPS = 1e-5
N_DENSE = (DEPTH + 1) // 2
N_MOE = DEPTH // 2

kernel_name = 'hybrid_lru_hgrn2_pool_sconv_moe'

F32 = jnp.float32


def rms_norm(x, g):
    xf = x.astype(F32)
    y = xf * lax.rsqrt(jnp.mean(xf * xf, axis=-1, keepdims=True) + NORM_EPS)
    return (y * g.astype(F32)).astype(x.dtype)


def causal_depthwise_conv(x, w):
    k = w.shape[0]
    return lax.conv_general_dilated(
        x, w[:, None, :].astype(x.dtype), window_strides=(1,), padding=[(k - 1, 0)],
        dimension_numbers=('NWC', 'WIO', 'NWC'), feature_group_count=x.shape[-1])


def block_diag_linear(x, w, b):
    bsz, s, c = x.shape
    xh = x.reshape(bsz, s, N_GROUP_HEADS, HEAD_DIM)
    return jnp.einsum('bshi,hij->bshj', xh, w).reshape(bsz, s, c) + b


def rg_lru(x, w_a, b_a, w_x, b_x, lam):
    r = jax.nn.sigmoid(block_diag_linear(x, w_a, b_a).astype(F32))
    i = jax.nn.sigmoid(block_diag_linear(x, w_x, b_x).astype(F32))
    log_a = -LRU_C * r * jax.nn.softplus(-lam.astype(F32))
    a = jnp.exp(log_a)
    mult = jnp.sqrt(-jnp.expm1(2.0 * log_a))
    u = mult * i * x.astype(F32)

    def combine(c1, c2):
        a1, b1 = c1
        a2, b2 = c2
        return a1 * a2, a2 * b1 + b2

    _, h = lax.associative_scan(combine, (a, u), axis=1)
    return h.astype(x.dtype)


def hgrn2(q, f_pre, v, g, lb, norm_g):
    bsz, s, c = q.shape
    n_chunks = s // HGRN_CHUNK
    z = f_pre.astype(F32)
    log_f = jnp.log(lb + (1.0 - lb) * jax.nn.sigmoid(z))
    k = (1.0 - lb) * jax.nn.sigmoid(-z)
    qf = jax.nn.silu(q.astype(F32))

    def to_chunks(t):
        return t.reshape(bsz, n_chunks, HGRN_CHUNK, N_GROUP_HEADS, HEAD_DIM).transpose(1, 0, 3, 2, 4)

    qc, kc, vc, gc = (to_chunks(t) for t in (qf, k, v.astype(F32), log_f))
    causal = jnp.tril(jnp.ones((HGRN_CHUNK, HGRN_CHUNK), dtype=bool))

    def step(state, inp):
        q_, k_, v_, g_ = inp
        G = jnp.cumsum(g_, axis=2)
        diff = G[:, :, :, None, :] - G[:, :, None, :, :]
        decay = jnp.exp(jnp.where(causal[:, :, None], diff, -jnp.inf))
        scores = jnp.einsum('bhtk,bhtsk,bhsk->bhts', q_, decay, k_)
        o = (jnp.einsum('bhts,bhsv->bhtv', scores, v_)
             + jnp.einsum('bhtk,bhkv->bhtv', q_ * jnp.exp(G), state))
        G_last = G[:, :, -1:, :]
        state = (jnp.exp(G_last[:, :, 0, :])[..., None] * state
                 + jnp.einsum('bhsk,bhsv->bhkv', k_ * jnp.exp(G_last - G), v_))
        return state, o

    s0 = jnp.zeros((bsz, N_GROUP_HEADS, HEAD_DIM, HEAD_DIM), F32)
    _, o = lax.scan(step, s0, (qc, kc, vc, gc))
    o = o.transpose(1, 0, 3, 2, 4).reshape(bsz, s, N_GROUP_HEADS, HEAD_DIM)
    o = o * lax.rsqrt(jnp.mean(o * o, axis=-1, keepdims=True) + HEAD_NORM_EPS) * norm_g.astype(F32)
    o = o * jax.nn.silu(g.astype(F32).reshape(bsz, s, N_GROUP_HEADS, HEAD_DIM))
    return o.reshape(bsz, s, c).astype(q.dtype)


def multiscale_pool(x, w_pool, scale):
    bsz, s, c = x.shape
    xf = x.astype(F32).reshape(bsz, s, len(POOL_WINDOWS), HEAD_DIM)
    cs = jnp.cumsum(xf, axis=1)
    pos = jnp.arange(1, s + 1, dtype=F32)
    outs = []
    for gi, w in enumerate(POOL_WINDOWS):
        cg = cs[:, :, gi]
        lag = jnp.pad(cg[:, :-w], ((0, 0), (w, 0), (0, 0)))
        mean = (cg - lag) / jnp.minimum(pos, float(w))[None, :, None]
        outs.append(mean - xf[:, :, gi])
    d = jnp.stack(outs, axis=2).astype(x.dtype)
    y = jnp.einsum('bsgi,gij->bsgj', d, w_pool).reshape(bsz, s, c)
    return y * scale


def hybrid_mixer(h, w_in, w_out, lru_conv_w, lru_conv_b, lru_w_a, lru_b_a, lru_w_x, lru_b_x,
                 lru_lambda, lb, hgrn_norm_g, pool_w, pool_scale, sconv_w):
    proj = h @ w_in
    a_x, a_g, q, f, v, o_g, p, sb, sc, sx = jnp.split(proj, N_IN_SLICES, axis=-1)
    a_in = causal_depthwise_conv(a_x, lru_conv_w) + lru_conv_b
    y_a = rg_lru(a_in, lru_w_a, lru_b_a, lru_w_x, lru_b_x, lru_lambda) * jax.nn.gelu(a_g)
    y_b = hgrn2(q, f, v, o_g, lb, hgrn_norm_g)
    y_c = multiscale_pool(p, pool_w, pool_scale)
    y_d = sb * causal_depthwise_conv(sc * sx, sconv_w)
    y = jnp.concatenate([y_a, y_b, y_c, y_d], axis=-1)
    return y @ w_out


def swiglu(h, w_gate, w_up, w_down):
    return (jax.nn.silu(h @ w_gate) * (h @ w_up)) @ w_down


def moe_swiglu(h, w_router, w1, w3, w2):
    bsz, s, d = h.shape
    n = bsz * s
    xf = h.reshape(n, d)
    logits = jnp.dot(xf.astype(F32), w_router.astype(F32))
    top_logit, top_e = lax.top_k(logits, TOP_K)
    gates = jax.nn.softmax(top_logit, axis=-1)
    e_flat = top_e.reshape(-1).astype(jnp.int32)
    tok_flat = jnp.arange(n * TOP_K, dtype=jnp.int32) // TOP_K
    g_flat = gates.reshape(-1)
    order = jnp.argsort(e_flat)
    e_sorted, tok_sorted, g_sorted = e_flat[order], tok_flat[order], g_flat[order]
    counts = jnp.bincount(e_flat, length=N_EXPERTS)
    start = jnp.cumsum(counts) - counts
    padded = (counts + ROW_BLOCK - 1) // ROW_BLOCK * ROW_BLOCK
    pend = jnp.cumsum(padded)
    pstart = pend - padded
    dest = pstart[e_sorted] + (jnp.arange(n * TOP_K, dtype=jnp.int32) - start[e_sorted])
    n_blocks = (n * TOP_K + ROW_BLOCK - 1) // ROW_BLOCK + N_EXPERTS
    n_rows = n_blocks * ROW_BLOCK
    row_tok = jnp.full((n_rows,), n, jnp.int32).at[dest].set(tok_sorted)
    row_gate = jnp.zeros((n_rows,), F32).at[dest].set(g_sorted)
    block_e = jnp.minimum(
        jnp.searchsorted(pend, jnp.arange(n_blocks, dtype=pend.dtype) * ROW_BLOCK, side='right'),
        N_EXPERTS - 1)
    x_rows = jnp.concatenate([xf, jnp.zeros((1, d), xf.dtype)], axis=0)[row_tok]
    x_rows = x_rows.reshape(n_blocks, ROW_BLOCK, d)

    def expert_block(args):
        xb, e = args
        return (jax.nn.silu(xb @ w1[e]) * (xb @ w3[e])) @ w2[e]

    y_rows = lax.map(expert_block, (x_rows, block_e)).reshape(n_rows, d)
    y = jax.ops.segment_sum(y_rows.astype(F32) * row_gate[:, None], row_tok, num_segments=n + 1)[:n]
    return y.reshape(bsz, s, d).astype(h.dtype)


def setup_inputs(seed: int = 0) -> dict:
    key = jax.random.key(seed)
    ks = jax.random.split(key, 32)

    def nrm(k, shape, fan_in):
        return jax.random.normal(k, shape, F32) * (fan_in ** -0.5)

    u = jax.random.uniform(ks[9], (DEPTH, GROUP_WIDTH), F32, minval=0.9, maxval=0.999)
    a0 = u ** (1.0 / LRU_C)
    lru_lambda = jnp.log(a0) - jnp.log1p(-a0)
    return {
        'x': jax.random.normal(ks[0], (BATCH, SEQ, D_MODEL), F32),
        'w_in': nrm(ks[1], (DEPTH, D_MODEL, D_IN), D_MODEL),
        'w_out': nrm(ks[2], (DEPTH, D_MIX, D_MODEL), D_MIX),
        'lru_conv_w': nrm(ks[3], (DEPTH, LRU_CONV, GROUP_WIDTH), LRU_CONV),
        'lru_conv_b': 0.01 * jax.random.normal(ks[4], (DEPTH, GROUP_WIDTH), F32),
        'lru_w_a': nrm(ks[5], (DEPTH, N_GROUP_HEADS, HEAD_DIM, HEAD_DIM), HEAD_DIM),
        'lru_b_a': 0.01 * jax.random.normal(ks[6], (DEPTH, GROUP_WIDTH), F32),
        'lru_w_x': nrm(ks[7], (DEPTH, N_GROUP_HEADS, HEAD_DIM, HEAD_DIM), HEAD_DIM),
        'lru_b_x': 0.01 * jax.random.normal(ks[8], (DEPTH, GROUP_WIDTH), F32),
        'lru_lambda': lru_lambda,
        'hgrn_lower_bounds': 0.1 * jax.random.normal(ks[10], (DEPTH, GROUP_WIDTH), F32),
        'hgrn_norm_g': 1.0 + 0.02 * jax.random.normal(ks[11], (DEPTH, HEAD_DIM), F32),
        'pool_w': nrm(ks[12], (DEPTH, len(POOL_WINDOWS), HEAD_DIM, HEAD_DIM), HEAD_DIM),
        'pool_scale': 1.0 + 0.1 * jax.random.normal(ks[13], (DEPTH, GROUP_WIDTH), F32),
        'sconv_w': nrm(ks[14], (DEPTH, SCONV_WIDTH, GROUP_WIDTH), SCONV_WIDTH),
        'norm_mix_g': 1.0 + 0.02 * jax.random.normal(ks[15], (DEPTH, D_MODEL), F32),
        'norm_ffn_g': 1.0 + 0.02 * jax.random.normal(ks[16], (DEPTH, D_MODEL), F32),
        'ffn_w_gate': nrm(ks[17], (N_DENSE, D_MODEL, D_FF), D_MODEL),
        'ffn_w_up': nrm(ks[18], (N_DENSE, D_MODEL, D_FF), D_MODEL),
        'ffn_w_down': nrm(ks[19], (N_DENSE, D_FF, D_MODEL), D_FF),
        'moe_w_router': nrm(ks[20], (N_MOE, D_MODEL, N_EXPERTS), D_MODEL),
        'moe_w1': nrm(ks[21], (N_MOE, N_EXPERTS, D_MODEL, D_FF_EXPERT), D_MODEL),
        'moe_w3': nrm(ks[22], (N_MOE, N_EXPERTS, D_MODEL, D_FF_EXPERT), D_MODEL),
        'moe_w2': nrm(ks[23], (N_MOE, N_EXPERTS, D_FF_EXPERT, D_MODEL), D_FF_EXPERT),
        'final_norm_g': 1.0 + 0.02 * jax.random.normal(ks[24], (D_MODEL,), F32),
    }


def reference(x, w_in, w_out, lru_conv_w, lru_conv_b, lru_w_a, lru_b_a, lru_w_x, lru_b_x,
              lru_lambda, hgrn_lower_bounds, hgrn_norm_g, pool_w, pool_scale, sconv_w,
              norm_mix_g, norm_ffn_g, ffn_w_gate, ffn_w_up, ffn_w_down,
              moe_w_router, moe_w1, moe_w3, moe_w2, final_norm_g):
    lb_all = jnp.cumsum(jax.nn.softmax(hgrn_lower_bounds.astype(F32), axis=0), axis=0)
    lb_all = lb_all - lb_all[0]
    h = x
    for layer in range(DEPTH):
        hn = rms_norm(h, norm_mix_g[layer])
        h = h + hybrid_mixer(hn, w_in[layer], w_out[layer], lru_conv_w[layer], lru_conv_b[layer],
                             lru_w_a[layer], lru_b_a[layer], lru_w_x[layer], lru_b_x[layer],
                             lru_lambda[layer], lb_all[layer], hgrn_norm_g[layer],
                             pool_w[layer], pool_scale[layer], sconv_w[layer])
        hn = rms_norm(h, norm_ffn_g[layer])
        if layer % 2 == 0:
            j = layer // 2
            h = h + swiglu(hn, ffn_w_gate[j], ffn_w_up[j], ffn_w_down[j])
        else:
            j = layer // 2
            h = h + moe_swiglu(hn, moe_w_router[j], moe_w1[j], moe_w3[j], moe_w2[j])
    return rms_norm(h, final_norm_g)
```

```python
import functools

import jax
import jax.numpy as jnp
from jax import lax
from jax.experimental import pallas as pl
from jax.experimental.pallas import tpu as pltpu

F32 = jnp.float32
BF16 = jnp.bfloat16
I32 = jnp.int32

GROUP_WIDTH = 256
N_GROUP_HEADS = 4
HEAD_DIM = 64
N_IN_SLICES = 10
LRU_CONV = 4
LRU_C = 8.0
HGRN_CHUNK = 64
POOL_WINDOWS = (2, 4, 8, 16)
SCONV_WIDTH = 3
TOP_K = 2
NORM_EPS = 1e-6
HEAD_NORM_EPS = 1e-5

V7X_LANES = 128
V7X_SUBLANES = 8
V7X_VMEM_BYTES = 64 * 1024 * 1024

MIX_TILE = 256
HALO = 16
DIAG = 16
FFN_TILE = 512
FFN_CHUNK = 256
ROUTER_TILE = 512
ROW_TILE = 256
EXPERT_ROWS = 1024
EXPERT_SUB = 256
EXPERT_FF = 512


def _dot(a, b):
    return jnp.dot(a, b, preferred_element_type=F32)


def _dot_nt(a, b):
    return lax.dot_general(a, b, (((1,), (1,)), ((), ())), preferred_element_type=F32)


def _dot_tn(a, b):
    return lax.dot_general(a, b, (((0,), (0,)), ((), ())), preferred_element_type=F32)


def _rms_norm(x, g):
    ms = jnp.mean(x * x, axis=-1, keepdims=True)
    return x * lax.rsqrt(ms + NORM_EPS) * g


def _split3(x):
    hi = x.astype(BF16)
    r1 = x - hi.astype(F32)
    mid = r1.astype(BF16)
    lo = (r1 - mid.astype(F32)).astype(BF16)
    return hi, mid, lo


def _mix_kernel(layer, h_ref, gmix_ref, w_in_ref, w_out_ref, cw_ref, cb_ref, wg_ref, bg_ref,
                lam_ref, lbraw_ref, hg_ref, wp_ref, ps_ref, sw_ref, ones_ref, tri_ref,
                out_ref,
                proj_s, ax_s, p_s, sc_s, sa_s, su_s, lruh_s, st_s, q_s, k_s, lf_s, o_s, y_s):
    T = h_ref.shape[0]
    GW = GROUP_WIDTH
    ti = pl.program_id(1)

    @pl.when(ti == 0)
    def _():
        ax_s[0:HALO, :] = jnp.zeros((HALO, GW), F32)
        p_s[0:HALO, :] = jnp.zeros((HALO, GW), F32)
        sc_s[0:HALO, :] = jnp.zeros((HALO, GW), F32)
        sa_s[0:T, :] = jnp.ones((T, GW), F32)
        su_s[0:T, :] = jnp.zeros((T, GW), F32)
        lruh_s[...] = jnp.zeros(lruh_s.shape, F32)
        st_s[...] = jnp.zeros(st_s.shape, F32)

    x = h_ref[...]
    hn = _rms_norm(x, gmix_ref[...]).astype(BF16)
    proj_s[...] = _dot(hn, w_in_ref[...])

    def sl(i):
        return proj_s[:, i * GW:(i + 1) * GW]

    ax_s[HALO:HALO + T, :] = sl(0)
    a_in = cb_ref[...]
    for kk in range(LRU_CONV):
        off = HALO - (LRU_CONV - 1) + kk
        a_in = a_in + cw_ref[kk:kk + 1, :] * ax_s[off:off + T, :]
    gates = _dot(a_in.astype(BF16), wg_ref[...]) + bg_ref[...]
    r_gate = jax.nn.sigmoid(gates[:, :GW])
    i_gate = jax.nn.sigmoid(gates[:, GW:])
    log_a = (-LRU_C) * r_gate * jax.nn.softplus(-lam_ref[...])
    a = jnp.exp(log_a)
    mult = jnp.sqrt(1.0 - a * a)
    u = mult * i_gate * a_in
    d = 1
    while d < T:
        sa_s[T:2 * T, :] = a
        su_s[T:2 * T, :] = u
        a_sh = sa_s[T - d:2 * T - d, :]
        u_sh = su_s[T - d:2 * T - d, :]
        u = a * u_sh + u
        a = a * a_sh
        d *= 2
    h_lru = a * lruh_s[0:1, :] + u
    lruh_s[0:1, :] = h_lru[T - 1:T, :]
    y_a = h_lru * jax.nn.gelu(sl(1))
    y_s[:, 0:GW] = y_a.astype(BF16)
    ax_s[0:HALO, :] = ax_s[T:T + HALO, :]

    p = sl(6)
    p_s[HALO:HALO + T, :] = p
    lane = lax.broadcasted_iota(I32, (1, GW), 1)
    win = jnp.where(lane < HEAD_DIM, float(POOL_WINDOWS[0]),
                    jnp.where(lane < 2 * HEAD_DIM, float(POOL_WINDOWS[1]),
                              jnp.where(lane < 3 * HEAD_DIM, float(POOL_WINDOWS[2]),
                                        float(POOL_WINDOWS[3]))))
    wsum = jnp.zeros((T, GW), F32)
    for j in range(max(POOL_WINDOWS)):
        wsum = wsum + jnp.where(win > float(j), p_s[HALO - j:HALO - j + T, :], 0.0)
    pos = (ti * T + 1 + lax.broadcasted_iota(I32, (T, 1), 0)).astype(F32)
    dpool = wsum / jnp.minimum(pos, win) - p
    y_c = _dot(dpool.astype(BF16), wp_ref[...]) * ps_ref[...]
    y_s[:, 2 * GW:3 * GW] = y_c.astype(BF16)
    p_s[0:HALO, :] = p_s[T:T + HALO, :]

    sc_s[HALO:HALO + T, :] = sl(8) * sl(9)
    conv = jnp.zeros((T, GW), F32)
    for kk in range(SCONV_WIDTH):
        off = HALO - (SCONV_WIDTH - 1) + kk
        conv = conv + sw_ref[kk:kk + 1, :] * sc_s[off:off + T, :]
    y_s[:, 3 * GW:4 * GW] = (sl(7) * conv).astype(BF16)
    sc_s[0:HALO, :] = sc_s[T:T + HALO, :]

    lbraw = lbraw_ref[...]
    e_lb = jnp.exp(lbraw - jnp.max(lbraw, axis=0, keepdims=True))
    p_lb = e_lb / jnp.sum(e_lb, axis=0, keepdims=True)
    lb = jnp.zeros((1, GW), F32)
    for li in range(1, layer + 1):
        lb = lb + p_lb[li:li + 1, :]
    z = sl(3)
    lf_s[...] = jnp.log(lb + (1.0 - lb) * jax.nn.sigmoid(z))
    k_s[...] = (1.0 - lb) * jax.nn.sigmoid(-z)
    q_s[...] = jax.nn.silu(sl(2))

    L = HGRN_CHUNK
    ones_bd = ones_ref[...]
    ones_f = ones_bd.astype(F32)
    tri = tri_ref[...]
    row = lax.broadcasted_iota(I32, (L, L), 0)
    col = lax.broadcasted_iota(I32, (L, L), 1)
    half = L // 2
    quarter = L // 4
    mask_b = (row >= half) & (col < half)
    mask_a = ((row // quarter) % 2 == 1) & (col // quarter == row // quarter - 1)
    mask_a4 = jnp.concatenate([mask_a] * N_GROUP_HEADS, axis=0)
    mask_b4 = jnp.concatenate([mask_b] * N_GROUP_HEADS, axis=0)
    row_l = lax.broadcasted_iota(I32, (L, 1), 0)
    row_d = lax.broadcasted_iota(I32, (DIAG, 1), 0)

    def chunk_body(c, carry):
        r0 = pl.multiple_of(c * L, L)
        lf = lf_s[pl.ds(r0, L), :]
        hi, mid, lo = _split3(lf)
        G = _dot(tri, hi) + _dot(tri, mid) + _dot(tri, lo)
        q = q_s[pl.ds(r0, L), :]
        k = k_s[pl.ds(r0, L), :]
        v = proj_s[pl.ds(r0, L), 4 * GW:5 * GW]
        vb = v.astype(BF16)
        g_last = G[L - 1:L, :]
        st = st_s[...]

        o = _dot_nt((q * jnp.exp(G)).astype(BF16), st.astype(BF16))

        g_half = G[half - 1:half, :]
        g_q1 = G[quarter - 1:quarter, :]
        g_q3 = G[half + quarter - 1:half + quarter, :]
        ref_a = jnp.where(row_l < half, g_q1, g_q3)
        scores = jnp.zeros((N_GROUP_HEADS * L, L), F32)
        for gref, msk in ((ref_a, mask_a4), (g_half, mask_b4)):
            qt = q * jnp.exp(jnp.minimum(G - gref, 0.0))
            kt = k * jnp.exp(jnp.minimum(gref - G, 0.0))
            q4 = jnp.concatenate([qt] * N_GROUP_HEADS, axis=0) * ones_f
            s4 = _dot_nt(q4.astype(BF16), kt.astype(BF16))
            scores = scores + jnp.where(msk, s4, 0.0)
        o4 = _dot(scores.astype(BF16), vb) * ones_f
        for hh in range(N_GROUP_HEADS):
            o = o + o4[hh * L:(hh + 1) * L, :]

        o_diag = []
        for b in range(L // DIAG):
            gs = G[b * DIAG:(b + 1) * DIAG, :]
            qs = q[b * DIAG:(b + 1) * DIAG, :]
            ks = k[b * DIAG:(b + 1) * DIAG, :]
            vs = v[b * DIAG:(b + 1) * DIAG, :]
            terms = []
            for j in range(DIAG):
                dg = jnp.where(row_d >= j, gs - gs[j:j + 1, :], -jnp.inf)
                terms.append((qs * (ks[j:j + 1, :] * jnp.exp(dg))).astype(BF16))
            e = jnp.concatenate(terms, axis=0)
            pd = _dot(e, ones_bd)
            od = jnp.zeros((DIAG, GW), F32)
            for j in range(DIAG):
                od = od + pd[j * DIAG:(j + 1) * DIAG, :] * vs[j:j + 1, :]
            o_diag.append(od)
        o = o + jnp.concatenate(o_diag, axis=0)
        o_s[pl.ds(r0, L), :] = o

        kd = (k * jnp.exp(g_last - G)).astype(BF16)
        st_s[...] = st * jnp.exp(g_last) + _dot_tn(vb, kd) * ones_f
        return carry

    lax.fori_loop(0, T // L, chunk_body, 0)

    o = o_s[...]
    ms = _dot((o * o).astype(BF16), ones_bd) * (1.0 / HEAD_DIM)
    o = o * lax.rsqrt(ms + HEAD_NORM_EPS) * hg_ref[...]
    y_s[:, GW:2 * GW] = (o * jax.nn.silu(sl(5))).astype(BF16)

    out_ref[...] = x + _dot(y_s[...], w_out_ref[...])


def _block_diag(w):
    return jax.scipy.linalg.block_diag(*[w[i] for i in range(w.shape[0])])


def _const_spec(shape):
    nd = len(shape)
    return pl.BlockSpec(shape, lambda *_: (0,) * nd)


def _mix_layer(layer, h, gmix, w_in, w_out, conv_w, conv_b, w_a, b_a, w_x, b_x, lam, lb_raw,
               norm_g, pool_w, pool_scale, sconv_w):
    B, S, D = h.shape
    GW = GROUP_WIDTH
    T = min(MIX_TILE, S)
    assert S % T == 0 and T % HGRN_CHUNK == 0 and w_in.shape == (D, N_IN_SLICES * GW)
    depth = lb_raw.shape[0]
    wg = jnp.concatenate([_block_diag(w_a), _block_diag(w_x)], axis=1).astype(BF16)
    bg = jnp.concatenate([b_a, b_x])[None, :]
    head_of = jnp.arange(GW) // HEAD_DIM
    ones_bd = (head_of[:, None] == head_of[None, :]).astype(BF16)
    tri = jnp.tril(jnp.ones((HGRN_CHUNK, HGRN_CHUNK), BF16))
    small = [gmix[None, :], w_in.astype(BF16), w_out.astype(BF16), conv_w, conv_b[None, :], wg, bg,
             lam[None, :], lb_raw, jnp.tile(norm_g, N_GROUP_HEADS)[None, :],
             _block_diag(pool_w).astype(BF16), pool_scale[None, :], sconv_w, ones_bd, tri]
    in_specs = [pl.BlockSpec((None, T, D), lambda b, t: (b, t, 0))]
    in_specs += [_const_spec(a.shape) for a in small]
    scratch = [
        pltpu.VMEM((T, N_IN_SLICES * GW), F32),
        pltpu.VMEM((HALO + T, GW), F32),
        pltpu.VMEM((HALO + T, GW), F32),
        pltpu.VMEM((HALO + T, GW), F32),
        pltpu.VMEM((2 * T, GW), F32),
        pltpu.VMEM((2 * T, GW), F32),
        pltpu.VMEM((V7X_SUBLANES, GW), F32),
        pltpu.VMEM((GW, GW), F32),
        pltpu.VMEM((T, GW), F32),
        pltpu.VMEM((T, GW), F32),
        pltpu.VMEM((T, GW), F32),
        pltpu.VMEM((T, GW), F32),
        pltpu.VMEM((T, N_GROUP_HEADS * GW), BF16),
    ]
    return pl.pallas_call(
        functools.partial(_mix_kernel, layer),
        out_shape=jax.ShapeDtypeStruct((B, S, D), F32),
        grid=(B, S // T),
        in_specs=in_specs,
        out_specs=pl.BlockSpec((None, T, D), lambda b, t: (b, t, 0)),
        scratch_shapes=scratch,
        compiler_params=pltpu.CompilerParams(
            dimension_semantics=("arbitrary", "arbitrary"),
            vmem_limit_bytes=48 * 1024 * 1024),
        name=f"mix{layer}",
    )(h, *small)


def _ffn_kernel(final, h_ref, g_ref, gf_ref, wg_ref, wu_ref, wd_ref, out_ref, act_s):
    x = h_ref[...]
    hn = _rms_norm(x, g_ref[...]).astype(BF16)
    ff = wg_ref.shape[1]
    for c in range(0, ff, FFN_CHUNK):
        gate = _dot(hn, wg_ref[:, c:c + FFN_CHUNK])
        up = _dot(hn, wu_ref[:, c:c + FFN_CHUNK])
        act_s[:, c:c + FFN_CHUNK] = (jax.nn.silu(gate) * up).astype(BF16)
    y = x + _dot(act_s[...], wd_ref[...])
    out_ref[...] = _rms_norm(y, gf_ref[...]) if final else y


def _ffn_layer(h2, g, w_gate, w_up, w_down, g_final, final):
    n, d = h2.shape
    ff = w_gate.shape[1]
    tm = min(FFN_TILE, n)
    assert n % tm == 0 and ff % FFN_CHUNK == 0
    once = pl.Buffered(1)
    return pl.pallas_call(
        functools.partial(_ffn_kernel, final),
        out_shape=jax.ShapeDtypeStruct((n, d), F32),
        grid=(n // tm,),
        in_specs=[pl.BlockSpec((tm, d), lambda i: (i, 0)),
                  _const_spec((1, d)),
                  _const_spec((1, d)),
                  pl.BlockSpec((d, ff), lambda i: (0, 0), pipeline_mode=once),
                  pl.BlockSpec((d, ff), lambda i: (0, 0), pipeline_mode=once),
                  pl.BlockSpec((ff, d), lambda i: (0, 0), pipeline_mode=once)],
        out_specs=pl.BlockSpec((tm, d), lambda i: (i, 0)),
        scratch_shapes=[pltpu.VMEM((tm, ff), BF16)],
        compiler_params=pltpu.CompilerParams(
            dimension_semantics=("arbitrary",),
            vmem_limit_bytes=48 * 1024 * 1024),
        name="ffn",
    )(h2, g[None, :], g_final[None, :], w_gate.astype(BF16), w_up.astype(BF16), w_down.astype(BF16))


def _router_kernel(h_ref, g_ref, wr_ref, triu_ref, hn_ref, mi_ref, mf_ref, cnt_ref, base_s):
    i = pl.program_id(0)
    n_e = wr_ref.shape[0]
    tr = h_ref.shape[0]

    @pl.when(i == 0)
    def _():
        base_s[...] = jnp.zeros(base_s.shape, F32)

    hn = _rms_norm(h_ref[...], g_ref[...])
    hn_ref[...] = hn
    xh, xm, xl = _split3(hn)
    wh, wm, wl = _split3(wr_ref[...])
    logits = (_dot_nt(wh, xh) + _dot_nt(wh, xm) + _dot_nt(wm, xh)
              + _dot_nt(wh, xl) + _dot_nt(wl, xh) + _dot_nt(wm, xm))
    eid = lax.broadcasted_iota(I32, (n_e, tr), 0)
    m1 = jnp.max(logits, axis=0, keepdims=True)
    e1 = jnp.min(jnp.where(logits == m1, eid, n_e), axis=0, keepdims=True)
    rest = jnp.where(eid == e1, -jnp.inf, logits)
    m2 = jnp.max(rest, axis=0, keepdims=True)
    e2 = jnp.min(jnp.where(rest == m2, eid, n_e), axis=0, keepdims=True)
    ex = jnp.exp(m2 - m1)
    g1 = 1.0 / (1.0 + ex)
    g2 = ex / (1.0 + ex)
    member = jnp.where((eid == e1) | (eid == e2), 1.0, 0.0)
    incl = _dot(member.astype(BF16), triu_ref[...])
    rank = incl - member + base_s[:, 0:1]
    r1 = jnp.sum(jnp.where(eid == e1, rank, 0.0), axis=0, keepdims=True)
    r2 = jnp.sum(jnp.where(eid == e2, rank, 0.0), axis=0, keepdims=True)
    base_s[...] = base_s[...] + incl[:, tr - 1:tr]
    cnt_ref[...] = base_s[...]
    zi = jnp.zeros((1, tr), I32)
    mi_ref[...] = jnp.concatenate(
        [e1, e2, r1.astype(I32), r2.astype(I32), zi, zi, zi, zi], axis=0)
    zf = jnp.zeros((1, tr), F32)
    mf_ref[...] = jnp.concatenate([g1, g2, zf, zf, zf, zf, zf, zf], axis=0)


def _router(h2, g, w_router):
    n, d = h2.shape
    n_e = w_router.shape[1]
    tr = min(ROUTER_TILE, n)
    assert n % tr == 0 and n_e == V7X_SUBLANES
    tok = jnp.arange(tr)
    triu = (tok[:, None] <= tok[None, :]).astype(BF16)
    return pl.pallas_call(
        _router_kernel,
        out_shape=(jax.ShapeDtypeStruct((n, d), F32),
                   jax.ShapeDtypeStruct((V7X_SUBLANES, n), I32),
                   jax.ShapeDtypeStruct((V7X_SUBLANES, n), F32),
                   jax.ShapeDtypeStruct((n_e, V7X_LANES), F32)),
        grid=(n // tr,),
        in_specs=[pl.BlockSpec((tr, d), lambda i: (i, 0)),
                  _const_spec((1, d)),
                  _const_spec((n_e, d)),
                  _const_spec((tr, tr))],
        out_specs=(pl.BlockSpec((tr, d), lambda i: (i, 0)),
                   pl.BlockSpec((V7X_SUBLANES, tr), lambda i: (0, i)),
                   pl.BlockSpec((V7X_SUBLANES, tr), lambda i: (0, i)),
                   _const_spec((n_e, V7X_LANES))),
        scratch_shapes=[pltpu.VMEM((n_e, V7X_LANES), F32)],
        compiler_params=pltpu.CompilerParams(dimension_semantics=("arbitrary",)),
        name="router",
    )(h2, g[None, :], w_router.T, triu)


def _row_copy(src_ref, src_row, dst_ref, dst_row, sem):
    return pltpu.make_async_copy(src_ref.at[pl.ds(src_row, 1)], dst_ref.at[pl.ds(dst_row, 1)], sem)


def _dispatch_kernel(pos1_ref, pos2_ref, hn_ref, xs_in_ref, xs_ref, sem):
    del xs_in_ref
    td = hn_ref.shape[0]
    base = pl.program_id(0) * td

    def start(r, c):
        _row_copy(hn_ref, r, xs_ref, pos1_ref[base + r], sem).start()
        _row_copy(hn_ref, r, xs_ref, pos2_ref[base + r], sem).start()
        return c

    def wait(r, c):
        _row_copy(hn_ref, r, xs_ref, pos1_ref[base + r], sem).wait()
        _row_copy(hn_ref, r, xs_ref, pos2_ref[base + r], sem).wait()
        return c

    lax.fori_loop(0, td, start, 0)
    lax.fori_loop(0, td, wait, 0)


def _dispatch(hn, pos1, pos2, n_rows):
    n, d = hn.shape
    td = min(ROW_TILE, n)
    assert n % td == 0
    zeros = jnp.zeros((n_rows, d), F32)
    return pl.pallas_call(
        _dispatch_kernel,
        out_shape=jax.ShapeDtypeStruct((n_rows, d), F32),
        grid_spec=pltpu.PrefetchScalarGridSpec(
            num_scalar_prefetch=2,
            grid=(n // td,),
            in_specs=[pl.BlockSpec((td, d), lambda i, p1, p2: (i, 0)),
                      pl.BlockSpec(memory_space=pl.ANY)],
            out_specs=pl.BlockSpec(memory_space=pl.ANY),
            scratch_shapes=[pltpu.SemaphoreType.DMA(())]),
        input_output_aliases={3: 0},
        compiler_params=pltpu.CompilerParams(dimension_semantics=("arbitrary",),
                                             has_side_effects=True),
        name="dispatch",
    )(pos1, pos2, hn, zeros)


def _expert_kernel(be_ref, rows_ref, src_ref, x_ref, w1_ref, w3_ref, w2_ref, out_ref,
                   w1_s, w3_s, w2_s):
    del be_ref, src_ref
    b = pl.program_id(0)
    f = pl.program_id(1)
    rows = rows_ref[b]

    @pl.when(f == 0)
    def _():
        out_ref[...] = jnp.zeros(out_ref.shape, F32)

    @pl.when(rows > 0)
    def _():
        w1_s[...] = w1_ref[...].astype(BF16)
        w3_s[...] = w3_ref[...].astype(BF16)
        w2_s[...] = w2_ref[...].astype(BF16)

        def sub(s, c):
            r0 = pl.multiple_of(s * EXPERT_SUB, EXPERT_SUB)
            xb = x_ref[pl.ds(r0, EXPERT_SUB), :].astype(BF16)
            gate = _dot(xb, w1_s[...])
            up = _dot(xb, w3_s[...])
            act = (jax.nn.silu(gate) * up).astype(BF16)
            out_ref[pl.ds(r0, EXPERT_SUB), :] += _dot(act, w2_s[...])
            return c

        lax.fori_loop(0, (rows + EXPERT_SUB - 1) // EXPERT_SUB, sub, 0)


def _experts(xs, blk_e, blk_rows, blk_src, w1, w3, w2):
    n_rows, d = xs.shape
    n_e, _, ff = w1.shape
    rb = EXPERT_ROWS
    fft = min(EXPERT_FF, ff)
    assert n_rows % rb == 0 and ff % fft == 0
    nf = ff // fft
    nb = n_rows // rb

    def f_eff(b, f, rows):
        return jnp.where(rows[b] > 0, f, nf - 1)

    return pl.pallas_call(
        _expert_kernel,
        out_shape=jax.ShapeDtypeStruct((n_rows, d), F32),
        grid_spec=pltpu.PrefetchScalarGridSpec(
            num_scalar_prefetch=3,
            grid=(nb, nf),
            in_specs=[
                pl.BlockSpec((rb, d), lambda b, f, be, rows, src: (src[b], 0)),
                pl.BlockSpec((None, d, fft), lambda b, f, be, rows, src: (be[b], 0, f_eff(b, f, rows))),
                pl.BlockSpec((None, d, fft), lambda b, f, be, rows, src: (be[b], 0, f_eff(b, f, rows))),
                pl.BlockSpec((None, fft, d), lambda b, f, be, rows, src: (be[b], f_eff(b, f, rows), 0)),
            ],
            out_specs=pl.BlockSpec((rb, d), lambda b, f, be, rows, src: (b, 0)),
            scratch_shapes=[pltpu.VMEM((d, fft), BF16), pltpu.VMEM((d, fft), BF16),
                            pltpu.VMEM((fft, d), BF16)]),
        compiler_params=pltpu.CompilerParams(
            dimension_semantics=("arbitrary", "arbitrary"),
            vmem_limit_bytes=48 * 1024 * 1024),
        name="experts",
    )(blk_e, blk_rows, blk_src, xs, w1, w3, w2)


def _combine_kernel(final, pos1_ref, pos2_ref, h_ref, mf_ref, g_ref, ys_ref, out_ref, y1_s, y2_s,
                    sem):
    tc = h_ref.shape[0]
    base = pl.program_id(0) * tc

    def start(r, c):
        _row_copy(ys_ref, pos1_ref[base + r], y1_s, r, sem).start()
        _row_copy(ys_ref, pos2_ref[base + r], y2_s, r, sem).start()
        return c

    def wait(r, c):
        _row_copy(ys_ref, pos1_ref[base + r], y1_s, r, sem).wait()
        _row_copy(ys_ref, pos2_ref[base + r], y2_s, r, sem).wait()
        return c

    lax.fori_loop(0, tc, start, 0)
    lax.fori_loop(0, tc, wait, 0)
    gates = mf_ref[...]
    y = h_ref[...] + y1_s[...] * gates[:, 0:1] + y2_s[...] * gates[:, 1:2]
    out_ref[...] = _rms_norm(y, g_ref[...]) if final else y


def _combine(h2, gates, pos1, pos2, ys, g_final, final):
    n, d = h2.shape
    tc = min(ROW_TILE, n)
    assert n % tc == 0
    return pl.pallas_call(
        functools.partial(_combine_kernel, final),
        out_shape=jax.ShapeDtypeStruct((n, d), F32),
        grid_spec=pltpu.PrefetchScalarGridSpec(
            num_scalar_prefetch=2,
            grid=(n // tc,),
            in_specs=[pl.BlockSpec((tc, d), lambda i, p1, p2: (i, 0)),
                      pl.BlockSpec((tc, TOP_K), lambda i, p1, p2: (i, 0)),
                      pl.BlockSpec((1, d), lambda i, p1, p2: (0, 0)),
                      pl.BlockSpec(memory_space=pl.ANY)],
            out_specs=pl.BlockSpec((tc, d), lambda i, p1, p2: (i, 0)),
            scratch_shapes=[pltpu.VMEM((tc, d), F32), pltpu.VMEM((tc, d), F32),
                            pltpu.SemaphoreType.DMA(())]),
        compiler_params=pltpu.CompilerParams(dimension_semantics=("arbitrary",)),
        name="combine",
    )(pos1, pos2, h2, gates, g_final[None, :], ys)


def _moe_layer(h2, g, w_router, w1, w3, w2, g_final, final=True):
    n, d = h2.shape
    n_e = w_router.shape[1]
    rb = EXPERT_ROWS
    hn, mi, mf, cnt = _router(h2, g, w_router)
    counts = cnt[:, 0].astype(I32)
    padded = (counts + rb - 1) // rb * rb
    pend = jnp.cumsum(padded)
    pstart = pend - padded
    nb = (n * TOP_K) // rb + n_e
    blk = jnp.arange(nb, dtype=I32)
    n_used = pend[-1] // rb
    last = jnp.maximum(n_used - 1, 0)
    blk_src = jnp.minimum(blk, last)
    blk_e = jnp.minimum(jnp.searchsorted(pend, blk_src * rb, side='right'), n_e - 1).astype(I32)
    blk_rows = jnp.where(blk < n_used,
                         jnp.clip(counts[blk_e] - (blk * rb - pstart[blk_e]), 0, rb), 0).astype(I32)
    pos1 = pstart[mi[0]] + mi[2]
    pos2 = pstart[mi[1]] + mi[3]
    xs = _dispatch(hn, pos1, pos2, nb * rb)
    ys = _experts(xs, blk_e, blk_rows, blk_src, w1, w3, w2)
    return _combine(h2, mf[:TOP_K].T, pos1, pos2, ys, g_final, final)


def kernel(x, w_in, w_out, lru_conv_w, lru_conv_b, lru_w_a, lru_b_a, lru_w_x, lru_b_x, lru_lambda,
           hgrn_lower_bounds, hgrn_norm_g, pool_w, pool_scale, sconv_w, norm_mix_g, norm_ffn_g,
           ffn_w_gate, ffn_w_up, ffn_w_down, moe_w_router, moe_w1, moe_w3, moe_w2, final_norm_g):
    B, S, D = x.shape
    depth = w_in.shape[0]
    h = x
    for layer in range(depth):
        h = _mix_layer(layer, h, norm_mix_g[layer], w_in[layer], w_out[layer], lru_conv_w[layer],
                       lru_conv_b[layer], lru_w_a[layer], lru_b_a[layer], lru_w_x[layer],
                       lru_b_x[layer], lru_lambda[layer], hgrn_lower_bounds, hgrn_norm_g[layer],
                       pool_w[layer], pool_scale[layer], sconv_w[layer])
        j = layer // 2
        final = layer == depth - 1
        h2 = h.reshape(B * S, D)
        if layer % 2 == 0:
            h2 = _ffn_layer(h2, norm_ffn_g[layer], ffn_w_gate[j], ffn_w_up[j], ffn_w_down[j],
                            final_norm_g, final)
        else:
            h2 = _moe_layer(h2, norm_ffn_g[layer], moe_w_router[j], moe_w1[j], moe_w3[j], moe_w2[j],
                            final_norm_g, final)
        h = h2.reshape(B, S, D)
    return h
```

```python
import functools

import jax
import jax.numpy as jnp
from jax import lax
from jax.experimental import pallas as pl
from jax.experimental.pallas import tpu as pltpu

F32 = jnp.float32
BF16 = jnp.bfloat16
I32 = jnp.int32

GROUP_WIDTH = 256
N_GROUP_HEADS = 4
HEAD_DIM = 64
N_IN_SLICES = 10
LRU_CONV = 4
LRU_C = 8.0
HGRN_CHUNK = 64
POOL_WINDOWS = (2, 4, 8, 16)
SCONV_WIDTH = 3
TOP_K = 2
NORM_EPS = 1e-6
HEAD_NORM_EPS = 1e-5

V7X_LANES = 128
V7X_SUBLANES = 8
V7X_VMEM_BYTES = 64 * 1024 * 1024

MIX_TILE = 256
HALO = 16
DIAG = 16
FFN_TILE = 512
FFN_CHUNK = 256
MOE_TILE = 256
ROW_PAD = V7X_SUBLANES
EXPERT_ROWS = 1024
EXPERT_SUB = 256
EXPERT_FF = 512


def _dot(a, b):
    return jnp.dot(a, b, preferred_element_type=F32)


def _dot_nt(a, b):
    return lax.dot_general(a, b, (((1,), (1,)), ((), ())), preferred_element_type=F32)


def _dot_tn(a, b):
    return lax.dot_general(a, b, (((0,), (0,)), ((), ())), preferred_element_type=F32)


def _rms_norm(x, g):
    ms = jnp.mean(x * x, axis=-1, keepdims=True)
    return x * lax.rsqrt(ms + NORM_EPS) * g


def _split3(x):
    hi = x.astype(BF16)
    r1 = x - hi.astype(F32)
    mid = r1.astype(BF16)
    lo = (r1 - mid.astype(F32)).astype(BF16)
    return hi, mid, lo


def _mix_kernel(layer, h_ref, gmix_ref, w_in_ref, w_out_ref, cw_ref, cb_ref, wg_ref, bg_ref,
                lam_ref, lbraw_ref, hg_ref, wp_ref, ps_ref, sw_ref, ones_ref, tri_ref,
                out_ref,
                proj_s, ax_s, p_s, sc_s, sa_s, su_s, lruh_s, st_s, q_s, k_s, lf_s, o_s, y_s):
    T = h_ref.shape[0]
    GW = GROUP_WIDTH
    ti = pl.program_id(1)

    @pl.when(ti == 0)
    def _():
        ax_s[0:HALO, :] = jnp.zeros((HALO, GW), F32)
        p_s[0:HALO, :] = jnp.zeros((HALO, GW), F32)
        sc_s[0:HALO, :] = jnp.zeros((HALO, GW), F32)
        sa_s[0:T, :] = jnp.ones((T, GW), F32)
        su_s[0:T, :] = jnp.zeros((T, GW), F32)
        lruh_s[...] = jnp.zeros(lruh_s.shape, F32)
        st_s[...] = jnp.zeros(st_s.shape, F32)

    x = h_ref[...]
    hn = _rms_norm(x, gmix_ref[...]).astype(BF16)
    proj_s[...] = _dot(hn, w_in_ref[...])

    def sl(i):
        return proj_s[:, i * GW:(i + 1) * GW]

    ax_s[HALO:HALO + T, :] = sl(0)
    a_in = cb_ref[...]
    for kk in range(LRU_CONV):
        off = HALO - (LRU_CONV - 1) + kk
        a_in = a_in + cw_ref[kk:kk + 1, :] * ax_s[off:off + T, :]
    gates = _dot(a_in.astype(BF16), wg_ref[...]) + bg_ref[...]
    r_gate = jax.nn.sigmoid(gates[:, :GW])
    i_gate = jax.nn.sigmoid(gates[:, GW:])
    log_a = (-LRU_C) * r_gate * jax.nn.softplus(-lam_ref[...])
    a = jnp.exp(log_a)
    mult = jnp.sqrt(1.0 - a * a)
    u = mult * i_gate * a_in
    d = 1
    while d < T:
        sa_s[T:2 * T, :] = a
        su_s[T:2 * T, :] = u
        a_sh = sa_s[T - d:2 * T - d, :]
        u_sh = su_s[T - d:2 * T - d, :]
        u = a * u_sh + u
        a = a * a_sh
        d *= 2
    h_lru = a * lruh_s[0:1, :] + u
    lruh_s[0:1, :] = h_lru[T - 1:T, :]
    y_a = h_lru * jax.nn.gelu(sl(1))
    y_s[:, 0:GW] = y_a.astype(BF16)
    ax_s[0:HALO, :] = ax_s[T:T + HALO, :]

    p = sl(6)
    p_s[HALO:HALO + T, :] = p
    lane = lax.broadcasted_iota(I32, (1, GW), 1)
    win = jnp.where(lane < HEAD_DIM, float(POOL_WINDOWS[0]),
                    jnp.where(lane < 2 * HEAD_DIM, float(POOL_WINDOWS[1]),
                              jnp.where(lane < 3 * HEAD_DIM, float(POOL_WINDOWS[2]),
                                        float(POOL_WINDOWS[3]))))
    wsum = jnp.zeros((T, GW), F32)
    for j in range(max(POOL_WINDOWS)):
        wsum = wsum + jnp.where(win > float(j), p_s[HALO - j:HALO - j + T, :], 0.0)
    pos = (ti * T + 1 + lax.broadcasted_iota(I32, (T, 1), 0)).astype(F32)
    dpool = wsum / jnp.minimum(pos, win) - p
    y_c = _dot(dpool.astype(BF16), wp_ref[...]) * ps_ref[...]
    y_s[:, 2 * GW:3 * GW] = y_c.astype(BF16)
    p_s[0:HALO, :] = p_s[T:T + HALO, :]

    sc_s[HALO:HALO + T, :] = sl(8) * sl(9)
    conv = jnp.zeros((T, GW), F32)
    for kk in range(SCONV_WIDTH):
        off = HALO - (SCONV_WIDTH - 1) + kk
        conv = conv + sw_ref[kk:kk + 1, :] * sc_s[off:off + T, :]
    y_s[:, 3 * GW:4 * GW] = (sl(7) * conv).astype(BF16)
    sc_s[0:HALO, :] = sc_s[T:T + HALO, :]

    lbraw = lbraw_ref[...]
    e_lb = jnp.exp(lbraw - jnp.max(lbraw, axis=0, keepdims=True))
    p_lb = e_lb / jnp.sum(e_lb, axis=0, keepdims=True)
    lb = jnp.zeros((1, GW), F32)
    for li in range(1, layer + 1):
        lb = lb + p_lb[li:li + 1, :]
    z = sl(3)
    lf_s[...] = jnp.log(lb + (1.0 - lb) * jax.nn.sigmoid(z))
    k_s[...] = (1.0 - lb) * jax.nn.sigmoid(-z)
    q_s[...] = jax.nn.silu(sl(2))

    L = HGRN_CHUNK
    ones_bd = ones_ref[...]
    ones_f = ones_bd.astype(F32)
    tri = tri_ref[...]
    row = lax.broadcasted_iota(I32, (L, L), 0)
    col = lax.broadcasted_iota(I32, (L, L), 1)
    half = L // 2
    quarter = L // 4
    mask_b = (row >= half) & (col < half)
    mask_a = ((row // quarter) % 2 == 1) & (col // quarter == row // quarter - 1)
    mask_a4 = jnp.concatenate([mask_a] * N_GROUP_HEADS, axis=0)
    mask_b4 = jnp.concatenate([mask_b] * N_GROUP_HEADS, axis=0)
    row_l = lax.broadcasted_iota(I32, (L, 1), 0)
    row_d = lax.broadcasted_iota(I32, (DIAG, 1), 0)

    def chunk_body(c, carry):
        r0 = pl.multiple_of(c * L, L)
        lf = lf_s[pl.ds(r0, L), :]
        hi, mid, lo = _split3(lf)
        G = _dot(tri, hi) + _dot(tri, mid) + _dot(tri, lo)
        q = q_s[pl.ds(r0, L), :]
        k = k_s[pl.ds(r0, L), :]
        v = proj_s[pl.ds(r0, L), 4 * GW:5 * GW]
        vb = v.astype(BF16)
        g_last = G[L - 1:L, :]
        st = st_s[...]

        o = _dot_nt((q * jnp.exp(G)).astype(BF16), st.astype(BF16))

        g_half = G[half - 1:half, :]
        g_q1 = G[quarter - 1:quarter, :]
        g_q3 = G[half + quarter - 1:half + quarter, :]
        ref_a = jnp.where(row_l < half, g_q1, g_q3)
        scores = jnp.zeros((N_GROUP_HEADS * L, L), F32)
        for gref, msk in ((ref_a, mask_a4), (g_half, mask_b4)):
            qt = q * jnp.exp(jnp.minimum(G - gref, 0.0))
            kt = k * jnp.exp(jnp.minimum(gref - G, 0.0))
            q4 = jnp.concatenate([qt] * N_GROUP_HEADS, axis=0) * ones_f
            s4 = _dot_nt(q4.astype(BF16), kt.astype(BF16))
            scores = scores + jnp.where(msk, s4, 0.0)
        o4 = _dot(scores.astype(BF16), vb) * ones_f
        for hh in range(N_GROUP_HEADS):
            o = o + o4[hh * L:(hh + 1) * L, :]

        o_diag = []
        for b in range(L // DIAG):
            gs = G[b * DIAG:(b + 1) * DIAG, :]
            qs = q[b * DIAG:(b + 1) * DIAG, :]
            ks = k[b * DIAG:(b + 1) * DIAG, :]
            vs = v[b * DIAG:(b + 1) * DIAG, :]
            terms = []
            for j in range(DIAG):
                dg = jnp.where(row_d >= j, gs - gs[j:j + 1, :], -jnp.inf)
                terms.append((qs * (ks[j:j + 1, :] * jnp.exp(dg))).astype(BF16))
            e = jnp.concatenate(terms, axis=0)
            pd = _dot(e, ones_bd)
            od = jnp.zeros((DIAG, GW), F32)
            for j in range(DIAG):
                od = od + pd[j * DIAG:(j + 1) * DIAG, :] * vs[j:j + 1, :]
            o_diag.append(od)
        o = o + jnp.concatenate(o_diag, axis=0)
        o_s[pl.ds(r0, L), :] = o

        kd = (k * jnp.exp(g_last - G)).astype(BF16)
        st_s[...] = st * jnp.exp(g_last) + _dot_tn(vb, kd) * ones_f
        return carry

    lax.fori_loop(0, T // L, chunk_body, 0)

    o = o_s[...]
    ms = _dot((o * o).astype(BF16), ones_bd) * (1.0 / HEAD_DIM)
    o = o * lax.rsqrt(ms + HEAD_NORM_EPS) * hg_ref[...]
    y_s[:, GW:2 * GW] = (o * jax.nn.silu(sl(5))).astype(BF16)

    out_ref[...] = x + _dot(y_s[...], w_out_ref[...])


def _block_diag(w):
    return jax.scipy.linalg.block_diag(*[w[i] for i in range(w.shape[0])])


def _const_spec(shape):
    nd = len(shape)
    return pl.BlockSpec(shape, lambda *_: (0,) * nd)


def _mix_layer(layer, h, gmix, w_in, w_out, conv_w, conv_b, w_a, b_a, w_x, b_x, lam, lb_raw,
               norm_g, pool_w, pool_scale, sconv_w):
    B, S, D = h.shape
    GW = GROUP_WIDTH
    T = min(MIX_TILE, S)
    assert S % T == 0 and T % HGRN_CHUNK == 0 and w_in.shape == (D, N_IN_SLICES * GW)
    depth = lb_raw.shape[0]
    wg = jnp.concatenate([_block_diag(w_a), _block_diag(w_x)], axis=1).astype(BF16)
    bg = jnp.concatenate([b_a, b_x])[None, :]
    head_of = jnp.arange(GW) // HEAD_DIM
    ones_bd = (head_of[:, None] == head_of[None, :]).astype(BF16)
    tri = jnp.tril(jnp.ones((HGRN_CHUNK, HGRN_CHUNK), BF16))
    small = [gmix[None, :], w_in.astype(BF16), w_out.astype(BF16), conv_w, conv_b[None, :], wg, bg,
             lam[None, :], lb_raw, jnp.tile(norm_g, N_GROUP_HEADS)[None, :],
             _block_diag(pool_w).astype(BF16), pool_scale[None, :], sconv_w, ones_bd, tri]
    in_specs = [pl.BlockSpec((None, T, D), lambda b, t: (b, t, 0))]
    in_specs += [_const_spec(a.shape) for a in small]
    scratch = [
        pltpu.VMEM((T, N_IN_SLICES * GW), F32),
        pltpu.VMEM((HALO + T, GW), F32),
        pltpu.VMEM((HALO + T, GW), F32),
        pltpu.VMEM((HALO + T, GW), F32),
        pltpu.VMEM((2 * T, GW), F32),
        pltpu.VMEM((2 * T, GW), F32),
        pltpu.VMEM((V7X_SUBLANES, GW), F32),
        pltpu.VMEM((GW, GW), F32),
        pltpu.VMEM((T, GW), F32),
        pltpu.VMEM((T, GW), F32),
        pltpu.VMEM((T, GW), F32),
        pltpu.VMEM((T, GW), F32),
        pltpu.VMEM((T, N_GROUP_HEADS * GW), BF16),
    ]
    return pl.pallas_call(
        functools.partial(_mix_kernel, layer),
        out_shape=jax.ShapeDtypeStruct((B, S, D), F32),
        grid=(B, S // T),
        in_specs=in_specs,
        out_specs=pl.BlockSpec((None, T, D), lambda b, t: (b, t, 0)),
        scratch_shapes=scratch,
        compiler_params=pltpu.CompilerParams(
            dimension_semantics=("arbitrary", "arbitrary"),
            vmem_limit_bytes=48 * 1024 * 1024),
        name=f"mix{layer}",
    )(h, *small)


def _ffn_kernel(final, h_ref, g_ref, gf_ref, wg_ref, wu_ref, wd_ref, out_ref, act_s):
    x = h_ref[...]
    hn = _rms_norm(x, g_ref[...]).astype(BF16)
    ff = wg_ref.shape[1]
    for c in range(0, ff, FFN_CHUNK):
        gate = _dot(hn, wg_ref[:, c:c + FFN_CHUNK])
        up = _dot(hn, wu_ref[:, c:c + FFN_CHUNK])
        act_s[:, c:c + FFN_CHUNK] = (jax.nn.silu(gate) * up).astype(BF16)
    y = x + _dot(act_s[...], wd_ref[...])
    out_ref[...] = _rms_norm(y, gf_ref[...]) if final else y


def _ffn_layer(h2, g, w_gate, w_up, w_down, g_final, final):
    n, d = h2.shape
    ff = w_gate.shape[1]
    tm = min(FFN_TILE, n)
    assert n % tm == 0 and ff % FFN_CHUNK == 0
    once = pl.Buffered(1)
    return pl.pallas_call(
        functools.partial(_ffn_kernel, final),
        out_shape=jax.ShapeDtypeStruct((n, d), F32),
        grid=(n // tm,),
        in_specs=[pl.BlockSpec((tm, d), lambda i: (i, 0)),
                  _const_spec((1, d)),
                  _const_spec((1, d)),
                  pl.BlockSpec((d, ff), lambda i: (0, 0), pipeline_mode=once),
                  pl.BlockSpec((d, ff), lambda i: (0, 0), pipeline_mode=once),
                  pl.BlockSpec((ff, d), lambda i: (0, 0), pipeline_mode=once)],
        out_specs=pl.BlockSpec((tm, d), lambda i: (i, 0)),
        scratch_shapes=[pltpu.VMEM((tm, ff), BF16)],
        compiler_params=pltpu.CompilerParams(
            dimension_semantics=("arbitrary",),
            vmem_limit_bytes=48 * 1024 * 1024),
        name="ffn",
    )(h2, g[None, :], g_final[None, :], w_gate.astype(BF16), w_up.astype(BF16), w_down.astype(BF16))


def _router_kernel(h_ref, g_ref, wr_ref, triu_ref, hn_ref, mi_ref, mf_ref, cnt_ref):
    n_e = wr_ref.shape[0]
    tr = h_ref.shape[0]
    hn = _rms_norm(h_ref[...], g_ref[...])
    hn_ref[...] = hn.astype(BF16)
    xh, xm, xl = _split3(hn)
    wh, wm, wl = _split3(wr_ref[...])
    logits = (_dot_nt(wh, xh) + _dot_nt(wh, xm) + _dot_nt(wm, xh)
              + _dot_nt(wh, xl) + _dot_nt(wl, xh) + _dot_nt(wm, xm))
    eid = lax.broadcasted_iota(I32, (n_e, tr), 0)
    m1 = jnp.max(logits, axis=0, keepdims=True)
    e1 = jnp.min(jnp.where(logits == m1, eid, n_e), axis=0, keepdims=True)
    rest = jnp.where(eid == e1, -jnp.inf, logits)
    m2 = jnp.max(rest, axis=0, keepdims=True)
    e2 = jnp.min(jnp.where(rest == m2, eid, n_e), axis=0, keepdims=True)
    ex = jnp.exp(m2 - m1)
    g1 = 1.0 / (1.0 + ex)
    g2 = ex / (1.0 + ex)
    member = jnp.where((eid == e1) | (eid == e2), 1.0, 0.0)
    incl = _dot(member.astype(BF16), triu_ref[...])
    rank = incl - member
    r1 = jnp.sum(jnp.where(eid == e1, rank, 0.0), axis=0, keepdims=True)
    r2 = jnp.sum(jnp.where(eid == e2, rank, 0.0), axis=0, keepdims=True)
    cnt_ref[...] = jnp.broadcast_to(incl[:, tr - 1:tr], cnt_ref.shape).astype(I32)
    zi = jnp.zeros((1, tr), I32)
    mi_ref[...] = jnp.concatenate(
        [e1, e2, r1.astype(I32), r2.astype(I32), zi, zi, zi, zi], axis=0)
    zf = jnp.zeros((1, tr), F32)
    mf_ref[...] = jnp.concatenate([g1, g2, zf, zf, zf, zf, zf, zf], axis=0)


def _router(h2, g, w_router):
    n, d = h2.shape
    n_e = w_router.shape[1]
    tr = min(MOE_TILE, n)
    assert n % tr == 0 and n_e == V7X_SUBLANES
    tok = jnp.arange(tr)
    triu = (tok[:, None] <= tok[None, :]).astype(BF16)
    return pl.pallas_call(
        _router_kernel,
        out_shape=(jax.ShapeDtypeStruct((n, d), BF16),
                   jax.ShapeDtypeStruct((V7X_SUBLANES, n), I32),
                   jax.ShapeDtypeStruct((V7X_SUBLANES, n), F32),
                   jax.ShapeDtypeStruct((n // tr, n_e, V7X_LANES), I32)),
        grid=(n // tr,),
        in_specs=[pl.BlockSpec((tr, d), lambda i: (i, 0)),
                  _const_spec((1, d)),
                  _const_spec((n_e, d)),
                  _const_spec((tr, tr))],
        out_specs=(pl.BlockSpec((tr, d), lambda i: (i, 0)),
                   pl.BlockSpec((V7X_SUBLANES, tr), lambda i: (0, i)),
                   pl.BlockSpec((V7X_SUBLANES, tr), lambda i: (0, i)),
                   pl.BlockSpec((None, n_e, V7X_LANES), lambda i: (i, 0, 0))),
        compiler_params=pltpu.CompilerParams(dimension_semantics=("arbitrary",)),
        name="router",
    )(h2, g[None, :], w_router.T, triu)


def _segment_copies(step, slot, tables, n_e, tile, local_ref, rows_ref, sem, to_rows, start):
    off_ref, row_ref, pad_ref = tables
    for e in range(n_e):
        off = off_ref[step * n_e + e]
        row = row_ref[step * n_e + e]
        pad = pad_ref[step * n_e + e]
        size = tile
        while size >= ROW_PAD:
            done = pad & (-2 * size)
            local = local_ref.at[slot, pl.ds(pl.multiple_of(off + done, ROW_PAD), size)]
            remote = rows_ref.at[pl.ds(pl.multiple_of(row + done, ROW_PAD), size)]
            cp = (pltpu.make_async_copy(local, remote, sem.at[slot]) if to_rows
                  else pltpu.make_async_copy(remote, local, sem.at[slot]))

            @pl.when((pad & size) != 0)
            def _():
                if start:
                    cp.start()
                else:
                    cp.wait()

            size //= 2


def _stack_rows(mi, off_ref, step, n_e):
    e1, e2, row1, row2 = mi[0:1, :], mi[1:2, :], mi[2:3, :], mi[3:4, :]
    for e in range(n_e):
        off = off_ref[step * n_e + e]
        row1 = row1 + jnp.where(e1 == e, off, 0)
        row2 = row2 + jnp.where(e2 == e, off, 0)
    return row1, row2


def _fill_copies(fill_ref, n_e, zero_s, xs_ref, sem, start):
    half = zero_s.shape[0]
    rb = 2 * half

    def go(cp):
        if start:
            cp.start()
        else:
            cp.wait()

    for e in range(n_e):
        row = fill_ref[e]
        gap = fill_ref[n_e + e]
        size = half
        while size >= ROW_PAD:
            done = gap & (-2 * size)
            cp = pltpu.make_async_copy(
                zero_s.at[pl.ds(0, size)],
                xs_ref.at[pl.ds(pl.multiple_of(row + done, ROW_PAD), size)], sem)
            pl.when((gap & size) != 0)(functools.partial(go, cp))
            size //= 2

    def block(b, c):
        for part in range(2):
            go(pltpu.make_async_copy(
                zero_s, xs_ref.at[pl.ds(pl.multiple_of(b * rb + part * half, half), half)], sem))
        return c

    lax.fori_loop(fill_ref[2 * n_e], xs_ref.shape[0] // rb, block, 0)


def _dispatch_kernel(nt, off_ref, row_ref, pad_ref, fill_ref, hn_ref, mi_ref, xs_ref, slab_s, zero_s,
                     sem, fill_sem):
    i = pl.program_id(0)
    tile = hn_ref.shape[0]
    n_e = V7X_SUBLANES
    r_stack = slab_s.shape[1]
    slot = lax.rem(i, 2)
    copies = functools.partial(_segment_copies, tables=(off_ref, row_ref, pad_ref), n_e=n_e,
                               tile=tile, local_ref=slab_s, rows_ref=xs_ref, sem=sem, to_rows=True)

    @pl.when(i == 0)
    def _():
        zero_s[...] = jnp.zeros(zero_s.shape, F32)
        _fill_copies(fill_ref, n_e, zero_s, xs_ref, fill_sem, start=True)

    @pl.when(i >= 2)
    def _():
        copies(i - 2, slot, start=False)

    row1, row2 = _stack_rows(mi_ref[...], off_ref, i, n_e)
    rid = lax.broadcasted_iota(I32, (r_stack, tile), 0)
    sel = jnp.where(rid == row1, 1.0, jnp.where(rid == row2, 1.0, 0.0)).astype(BF16)
    slab_s[slot] = _dot(sel, hn_ref[...])
    copies(i, slot, start=True)

    @pl.when(i == nt - 1)
    def _():
        if nt > 1:
            copies(i - 1, 1 - slot, start=False)
        copies(i, slot, start=False)
        _fill_copies(fill_ref, n_e, zero_s, xs_ref, fill_sem, start=False)


def _dispatch(hn, mi, tables, fill, n_rows):
    n, d = hn.shape
    tile = min(MOE_TILE, n)
    nt = n // tile
    r_stack = TOP_K * tile + V7X_SUBLANES * ROW_PAD
    return pl.pallas_call(
        functools.partial(_dispatch_kernel, nt),
        out_shape=jax.ShapeDtypeStruct((n_rows, d), F32),
        grid_spec=pltpu.PrefetchScalarGridSpec(
            num_scalar_prefetch=4,
            grid=(nt,),
            in_specs=[pl.BlockSpec((tile, d), lambda i, *_: (i, 0)),
                      pl.BlockSpec((V7X_SUBLANES, tile), lambda i, *_: (0, i))],
            out_specs=pl.BlockSpec(memory_space=pl.ANY),
            scratch_shapes=[pltpu.VMEM((2, r_stack, d), F32),
                            pltpu.VMEM((EXPERT_ROWS // 2, d), F32),
                            pltpu.SemaphoreType.DMA((2,)), pltpu.SemaphoreType.DMA(())]),
        compiler_params=pltpu.CompilerParams(dimension_semantics=("arbitrary",),
                                             has_side_effects=True),
        name="dispatch",
    )(*tables, fill, hn, mi)


def _expert_kernel(be_ref, rows_ref, src_ref, x_ref, w1_ref, w3_ref, w2_ref, out_ref,
                   w1_s, w3_s, w2_s):
    del be_ref, src_ref
    b = pl.program_id(0)
    f = pl.program_id(1)
    rows = rows_ref[b]

    @pl.when(f == 0)
    def _():
        out_ref[...] = jnp.zeros(out_ref.shape, F32)

    @pl.when(rows > 0)
    def _():
        w1_s[...] = w1_ref[...].astype(BF16)
        w3_s[...] = w3_ref[...].astype(BF16)
        w2_s[...] = w2_ref[...].astype(BF16)

        def sub(s, c):
            r0 = pl.multiple_of(s * EXPERT_SUB, EXPERT_SUB)
            xb = x_ref[pl.ds(r0, EXPERT_SUB), :].astype(BF16)
            gate = _dot(xb, w1_s[...])
            up = _dot(xb, w3_s[...])
            act = (jax.nn.silu(gate) * up).astype(BF16)
            out_ref[pl.ds(r0, EXPERT_SUB), :] += _dot(act, w2_s[...])
            return c

        lax.fori_loop(0, (rows + EXPERT_SUB - 1) // EXPERT_SUB, sub, 0)


def _experts(xs, blk_e, blk_rows, blk_src, w1, w3, w2):
    n_rows, d = xs.shape
    n_e, _, ff = w1.shape
    rb = EXPERT_ROWS
    fft = min(EXPERT_FF, ff)
    assert n_rows % rb == 0 and ff % fft == 0
    nf = ff // fft
    nb = n_rows // rb

    def f_eff(b, f, rows):
        return jnp.where(rows[b] > 0, f, nf - 1)

    return pl.pallas_call(
        _expert_kernel,
        out_shape=jax.ShapeDtypeStruct((n_rows, d), F32),
        grid_spec=pltpu.PrefetchScalarGridSpec(
            num_scalar_prefetch=3,
            grid=(nb, nf),
            in_specs=[
                pl.BlockSpec((rb, d), lambda b, f, be, rows, src: (src[b], 0)),
                pl.BlockSpec((None, d, fft), lambda b, f, be, rows, src: (be[b], 0, f_eff(b, f, rows))),
                pl.BlockSpec((None, d, fft), lambda b, f, be, rows, src: (be[b], 0, f_eff(b, f, rows))),
                pl.BlockSpec((None, fft, d), lambda b, f, be, rows, src: (be[b], f_eff(b, f, rows), 0)),
            ],
            out_specs=pl.BlockSpec((rb, d), lambda b, f, be, rows, src: (b, 0)),
            scratch_shapes=[pltpu.VMEM((d, fft), BF16), pltpu.VMEM((d, fft), BF16),
                            pltpu.VMEM((fft, d), BF16)]),
        compiler_params=pltpu.CompilerParams(
            dimension_semantics=("arbitrary", "arbitrary"),
            vmem_limit_bytes=48 * 1024 * 1024),
        name="experts",
    )(blk_e, blk_rows, blk_src, xs, w1, w3, w2)


def _combine_kernel(final, nt, off_ref, row_ref, pad_ref, h_ref, mi_ref, gt_ref, g_ref, ys_ref,
                    out_ref, stack_s, sem):
    i = pl.program_id(0)
    tile = h_ref.shape[0]
    n_e = V7X_SUBLANES
    r_stack = stack_s.shape[1]
    slot = lax.rem(i, 2)
    copies = functools.partial(_segment_copies, tables=(off_ref, row_ref, pad_ref), n_e=n_e,
                               tile=tile, local_ref=stack_s, rows_ref=ys_ref, sem=sem, to_rows=False)

    @pl.when(i == 0)
    def _():
        stack_s[...] = jnp.zeros(stack_s.shape, F32)
        copies(i, slot, start=True)

    @pl.when(i + 1 < nt)
    def _():
        copies(i + 1, 1 - slot, start=True)

    copies(i, slot, start=False)
    y_rows = stack_s[slot].astype(BF16)
    row1, row2 = _stack_rows(mi_ref[...], off_ref, i, n_e)
    rid = lax.broadcasted_iota(I32, (r_stack, tile), 0)
    y1 = _dot_tn(jnp.where(rid == row1, 1.0, 0.0).astype(BF16), y_rows)
    y2 = _dot_tn(jnp.where(rid == row2, 1.0, 0.0).astype(BF16), y_rows)
    gates = gt_ref[...]
    y = h_ref[...] + y1 * gates[:, 0:1] + y2 * gates[:, 1:2]
    out_ref[...] = _rms_norm(y, g_ref[...]) if final else y


def _combine(h2, mi, gates, tables, ys, g_final, final):
    n, d = h2.shape
    tile = min(MOE_TILE, n)
    nt = n // tile
    r_stack = TOP_K * tile + V7X_SUBLANES * ROW_PAD
    return pl.pallas_call(
        functools.partial(_combine_kernel, final, nt),
        out_shape=jax.ShapeDtypeStruct((n, d), F32),
        grid_spec=pltpu.PrefetchScalarGridSpec(
            num_scalar_prefetch=3,
            grid=(nt,),
            in_specs=[pl.BlockSpec((tile, d), lambda i, *_: (i, 0)),
                      pl.BlockSpec((V7X_SUBLANES, tile), lambda i, *_: (0, i)),
                      pl.BlockSpec((tile, TOP_K), lambda i, *_: (i, 0)),
                      pl.BlockSpec((1, d), lambda i, *_: (0, 0)),
                      pl.BlockSpec(memory_space=pl.ANY)],
            out_specs=pl.BlockSpec((tile, d), lambda i, *_: (i, 0)),
            scratch_shapes=[pltpu.VMEM((2, r_stack, d), F32), pltpu.SemaphoreType.DMA((2,))]),
        compiler_params=pltpu.CompilerParams(dimension_semantics=("arbitrary",)),
        name="combine",
    )(*tables, h2, mi, gates, g_final[None, :], ys)


def _moe_layer(h2, g, w_router, w1, w3, w2, g_final, final=True):
    n, d = h2.shape
    n_e = w_router.shape[1]
    rb = EXPERT_ROWS
    tile = min(MOE_TILE, n)
    nt = n // tile
    hn, mi, mf, cnt = _router(h2, g, w_router)
    pad = (cnt[:, :, 0] + ROW_PAD - 1) // ROW_PAD * ROW_PAD
    off = jnp.cumsum(pad, axis=1) - pad
    total = jnp.sum(pad, axis=0)
    padded = (total + rb - 1) // rb * rb
    pend = jnp.cumsum(padded)
    pstart = pend - padded
    row = pstart[None, :] + jnp.cumsum(pad, axis=0) - pad
    tables = tuple(t.reshape(-1).astype(I32) for t in (off, row, pad))
    nb = (n * TOP_K + nt * n_e * (ROW_PAD - 1)) // rb + n_e
    blk = jnp.arange(nb, dtype=I32)
    n_used = pend[-1] // rb
    last = jnp.maximum(n_used - 1, 0)
    blk_src = jnp.minimum(blk, last)
    blk_e = jnp.minimum(jnp.searchsorted(pend, blk_src * rb, side='right'), n_e - 1).astype(I32)
    blk_rows = jnp.where(blk < n_used,
                         jnp.clip(total[blk_e] - (blk * rb - pstart[blk_e]), 0, rb), 0).astype(I32)
    fill = jnp.concatenate([pstart + total, padded - total, n_used[None]]).astype(I32)
    xs = _dispatch(hn, mi, tables, fill, nb * rb)
    ys = _experts(xs, blk_e, blk_rows, blk_src, w1, w3, w2)
    return _combine(h2, mi, mf[:TOP_K].T, tables, ys, g_final, final)


def kernel(x, w_in, w_out, lru_conv_w, lru_conv_b, lru_w_a, lru_b_a, lru_w_x, lru_b_x, lru_lambda,
           hgrn_lower_bounds, hgrn_norm_g, pool_w, pool_scale, sconv_w, norm_mix_g, norm_ffn_g,
           ffn_w_gate, ffn_w_up, ffn_w_down, moe_w_router, moe_w1, moe_w3, moe_w2, final_norm_g):
    B, S, D = x.shape
    depth = w_in.shape[0]
    h = x
    for layer in range(depth):
        h = _mix_layer(layer, h, norm_mix_g[layer], w_in[layer], w_out[layer], lru_conv_w[layer],
                       lru_conv_b[layer], lru_w_a[layer], lru_b_a[layer], lru_w_x[layer],
                       lru_b_x[layer], lru_lambda[layer], hgrn_lower_bounds, hgrn_norm_g[layer],
                       pool_w[layer], pool_scale[layer], sconv_w[layer])
        j = layer // 2
        final = layer == depth - 1
        h2 = h.reshape(B * S, D)
        if layer % 2 == 0:
            h2 = _ffn_layer(h2, norm_ffn_g[layer], ffn_w_gate[j], ffn_w_up[j], ffn_w_down[j],
                            final_norm_g, final)
        else:
            h2 = _moe_layer(h2, norm_ffn_g[layer], moe_w_router[j], moe_w1[j], moe_w3[j], moe_w2[j],
                            final_norm_g, final)
        h = h2.reshape(B, S, D)
    return h
```

```python
import functools

import jax
import jax.numpy as jnp
from jax import lax
from jax.experimental import pallas as pl
from jax.experimental.pallas import tpu as pltpu

F32 = jnp.float32
BF16 = jnp.bfloat16
I32 = jnp.int32

GROUP_WIDTH = 256
N_GROUP_HEADS = 4
HEAD_DIM = 64
N_IN_SLICES = 10
LRU_CONV = 4
LRU_C = 8.0
HGRN_CHUNK = 64
POOL_WINDOWS = (2, 4, 8, 16)
SCONV_WIDTH = 3
TOP_K = 2
NORM_EPS = 1e-6
HEAD_NORM_EPS = 1e-5

V7X_LANES = 128
V7X_SUBLANES = 8
V7X_VMEM_BYTES = 64 * 1024 * 1024

MIX_TILE = 256
HALO = 16
DIAG = 8
FFN_TILE = 512
FFN_CHUNK = 256
MOE_TILE = 256
ROW_PAD = V7X_SUBLANES
EXPERT_ROWS = 1024
EXPERT_SUB = 256
EXPERT_FULL = 512
EXPERT_FF = 512


def _dot(a, b):
    return jnp.dot(a, b, preferred_element_type=F32)


def _dot_nt(a, b):
    return lax.dot_general(a, b, (((1,), (1,)), ((), ())), preferred_element_type=F32)


def _dot_tn(a, b):
    return lax.dot_general(a, b, (((0,), (0,)), ((), ())), preferred_element_type=F32)


def _rms_norm(x, g):
    ms = jnp.mean(x * x, axis=-1, keepdims=True)
    return x * lax.rsqrt(ms + NORM_EPS) * g


def _split3(x):
    hi = x.astype(BF16)
    r1 = x - hi.astype(F32)
    mid = r1.astype(BF16)
    lo = (r1 - mid.astype(F32)).astype(BF16)
    return hi, mid, lo


def _mix_kernel(layer, h_ref, gmix_ref, w_in_ref, w_out_ref, cw_ref, cb_ref, wg_ref, bg_ref,
                lam_ref, lbraw_ref, hg_ref, wp_ref, ps_ref, sw_ref, ones_ref, tri_ref, lvl_ref,
                out_ref,
                proj_s, ax_s, p_s, sc_s, sa_s, su_s, ta_s, tu_s, tp_s, lruh_s, st_s, q_s, k_s, lf_s,
                o_s, y_s):
    T = h_ref.shape[0]
    GW = GROUP_WIDTH
    ti = pl.program_id(1)

    @pl.when(ti == 0)
    def _():
        ax_s[0:HALO, :] = jnp.zeros((HALO, GW), F32)
        p_s[0:HALO, :] = jnp.zeros((HALO, GW), F32)
        sc_s[0:HALO, :] = jnp.zeros((HALO, GW), F32)
        sa_s[0:V7X_SUBLANES, :] = jnp.ones((V7X_SUBLANES, GW), F32)
        su_s[0:V7X_SUBLANES, :] = jnp.zeros((V7X_SUBLANES, GW), F32)
        ta_s[0:V7X_SUBLANES, :] = jnp.zeros((V7X_SUBLANES, GW), F32)
        tu_s[0:V7X_SUBLANES, :] = jnp.zeros((V7X_SUBLANES, GW), F32)
        tp_s[0:HALO, :] = jnp.zeros((HALO, GW), F32)
        lruh_s[...] = jnp.zeros(lruh_s.shape, F32)
        st_s[...] = jnp.zeros(st_s.shape, F32)

    x = h_ref[...]
    hn = _rms_norm(x, gmix_ref[...]).astype(BF16)
    proj_s[...] = _dot(hn, w_in_ref[...])

    def sl(i):
        return proj_s[:, i * GW:(i + 1) * GW]

    ax_s[HALO:HALO + T, :] = sl(0)
    a_in = cb_ref[...]
    for kk in range(LRU_CONV):
        off = HALO - (LRU_CONV - 1) + kk
        a_in = a_in + cw_ref[kk:kk + 1, :] * ax_s[off:off + T, :]
    gates = _dot(a_in.astype(BF16), wg_ref[...]) + bg_ref[...]
    r_gate = jax.nn.sigmoid(gates[:, :GW])
    i_gate = jax.nn.sigmoid(gates[:, GW:])
    log_a = (-LRU_C) * r_gate * jax.nn.softplus(-lam_ref[...])
    a = jnp.exp(log_a)
    mult = jnp.sqrt(1.0 - a * a)
    u = mult * i_gate * a_in
    SUB = V7X_SUBLANES
    sa_s[SUB:SUB + T, :] = a
    su_s[SUB:SUB + T, :] = u
    a = sa_s[...]
    u = su_s[...]
    d = 1
    while d < SUB:
        ta_s[SUB:, :] = a
        tu_s[SUB:, :] = u
        a_sh = ta_s[SUB - d:SUB - d + SUB + T, :]
        u_sh = tu_s[SUB - d:SUB - d + SUB + T, :]
        u = a * u_sh + u
        a = a * a_sh
        d *= 2
    sa_s[0:SUB, :] = sa_s[T:T + SUB, :]
    su_s[0:SUB, :] = su_s[T:T + SUB, :]
    h_grp = lruh_s[...]
    groups = []
    for gi in range(1, T // SUB + 1):
        h_grp = a[gi * SUB:(gi + 1) * SUB, :] * h_grp + u[gi * SUB:(gi + 1) * SUB, :]
        groups.append(h_grp)
    lruh_s[...] = h_grp
    h_lru = jnp.concatenate(groups, axis=0)
    y_a = h_lru * jax.nn.gelu(sl(1))
    y_s[:, 0:GW] = y_a.astype(BF16)
    ax_s[0:HALO, :] = ax_s[T:T + HALO, :]

    p = sl(6)
    p_s[HALO:HALO + T, :] = p
    lane = lax.broadcasted_iota(I32, (1, GW), 1)
    win = jnp.where(lane < HEAD_DIM, float(POOL_WINDOWS[0]),
                    jnp.where(lane < 2 * HEAD_DIM, float(POOL_WINDOWS[1]),
                              jnp.where(lane < 3 * HEAD_DIM, float(POOL_WINDOWS[2]),
                                        float(POOL_WINDOWS[3]))))
    sums = []
    acc = p_s[...]
    d = 1
    while d < max(POOL_WINDOWS):
        tp_s[HALO:, :] = acc
        acc = acc + tp_s[HALO - d:HALO - d + HALO + T, :]
        sums.append(acc[HALO:, :])
        d *= 2
    wsum = jnp.where(lane < HEAD_DIM, sums[0],
                     jnp.where(lane < 2 * HEAD_DIM, sums[1],
                               jnp.where(lane < 3 * HEAD_DIM, sums[2], sums[3])))
    pos =(ti * T + 1 + lax.broadcasted_iota(I32, (T, 1), 0)).astype(F32)
    dpool = wsum / jnp.minimum(pos, win) - p
    y_c = _dot(dpool.astype(BF16), wp_ref[...]) * ps_ref[...]
    y_s[:, 2 * GW:3 * GW] = y_c.astype(BF16)
    p_s[0:HALO, :] = p_s[T:T + HALO, :]

    sc_s[HALO:HALO + T, :] = sl(8) * sl(9)
    conv = jnp.zeros((T, GW), F32)
    for kk in range(SCONV_WIDTH):
        off = HALO - (SCONV_WIDTH - 1) + kk
        conv = conv + sw_ref[kk:kk + 1, :] * sc_s[off:off + T, :]
    y_s[:, 3 * GW:4 * GW] = (sl(7) * conv).astype(BF16)
    sc_s[0:HALO, :] = sc_s[T:T + HALO, :]

    lbraw = lbraw_ref[...]
    e_lb = jnp.exp(lbraw - jnp.max(lbraw, axis=0, keepdims=True))
    p_lb = e_lb / jnp.sum(e_lb, axis=0, keepdims=True)
    lb = jnp.zeros((1, GW), F32)
    for li in range(1, layer + 1):
        lb = lb + p_lb[li:li + 1, :]
    z = sl(3)
    lf_s[...] = jnp.log(lb + (1.0 - lb) * jax.nn.sigmoid(z))
    k_s[...] = (1.0 - lb) * jax.nn.sigmoid(-z)
    q_s[...] = jax.nn.silu(sl(2))

    L = HGRN_CHUNK
    ones_bd = ones_ref[...]
    ones_f = ones_bd.astype(F32)
    tri = tri_ref[...]
    widths = _level_widths()
    row_l = lax.broadcasted_iota(I32, (L, 1), 0)
    row_d = lax.broadcasted_iota(I32, (DIAG, 1), 0)

    def chunk_body(c, carry):
        r0 = c * L
        lf = lf_s[pl.ds(r0, L), :]
        hi, mid, lo = _split3(lf)
        G = _dot(tri, hi) + _dot(tri, mid) + _dot(tri, lo)
        q = q_s[pl.ds(r0, L), :]
        k = k_s[pl.ds(r0, L), :]
        v = proj_s[pl.ds(r0, L), 4 * GW:5 * GW]
        vb = v.astype(BF16)
        g_last = G[L - 1:L, :]
        st = st_s[...]

        o = _dot_nt((q * jnp.exp(G)).astype(BF16), st.astype(BF16))

        scores = jnp.zeros((N_GROUP_HEADS * L, L), F32)
        for li, w in enumerate(widths):
            gref = G[w - 1:w, :]
            for pair in range(1, L // (2 * w)):
                lastrow = pair * 2 * w + w - 1
                gref = jnp.where(row_l < pair * 2 * w, gref, G[lastrow:lastrow + 1, :])
            qt = q * jnp.exp(jnp.minimum(G - gref, 0.0))
            kt = k * jnp.exp(jnp.minimum(gref - G, 0.0))
            q4 = jnp.concatenate([qt] * N_GROUP_HEADS, axis=0) * ones_f
            s4 = _dot_nt(q4.astype(BF16), kt.astype(BF16))
            scores = scores + s4 * lvl_ref[li]
        o4 = _dot(scores.astype(BF16), vb) * ones_f
        for hh in range(N_GROUP_HEADS):
            o = o + o4[hh * L:(hh + 1) * L, :]

        o_diag = []
        for b in range(L // DIAG):
            gs = G[b * DIAG:(b + 1) * DIAG, :]
            qs = q[b * DIAG:(b + 1) * DIAG, :]
            ks = k[b * DIAG:(b + 1) * DIAG, :]
            vs = v[b * DIAG:(b + 1) * DIAG, :]
            terms = []
            for j in range(DIAG):
                dg = jnp.where(row_d >= j, gs - gs[j:j + 1, :], -jnp.inf)
                terms.append((qs * (ks[j:j + 1, :] * jnp.exp(dg))).astype(BF16))
            e = jnp.concatenate(terms, axis=0)
            pd = _dot(e, ones_bd)
            od = jnp.zeros((DIAG, GW), F32)
            for j in range(DIAG):
                od = od + pd[j * DIAG:(j + 1) * DIAG, :] * vs[j:j + 1, :]
            o_diag.append(od)
        o = o + jnp.concatenate(o_diag, axis=0)
        o_s[pl.ds(r0, L), :] = o

        kd = (k * jnp.exp(g_last - G)).astype(BF16)
        st_s[...] = st * jnp.exp(g_last) + _dot_tn(vb, kd) * ones_f
        return carry

    for c in range(T // L):
        chunk_body(c, 0)

    o = o_s[...]
    ms = _dot((o * o).astype(BF16), ones_bd) * (1.0 / HEAD_DIM)
    o = o * lax.rsqrt(ms + HEAD_NORM_EPS) * hg_ref[...]
    y_s[:, GW:2 * GW] = (o * jax.nn.silu(sl(5))).astype(BF16)

    out_ref[...] = x + _dot(y_s[...], w_out_ref[...])


def _level_widths():
    widths = []
    w = DIAG
    while w < HGRN_CHUNK:
        widths.append(w)
        w *= 2
    return widths


def _level_masks():
    L = HGRN_CHUNK
    t = jnp.arange(N_GROUP_HEADS * L)[:, None] % L
    s = jnp.arange(L)[None, :]
    return jnp.stack([(((t // w) % 2 == 1) & (s // w == t // w - 1)).astype(F32)
                      for w in _level_widths()])


def _block_diag(w):
    return jax.scipy.linalg.block_diag(*[w[i] for i in range(w.shape[0])])


def _const_spec(shape):
    nd = len(shape)
    return pl.BlockSpec(shape, lambda *_: (0,) * nd)


def _mix_layer(layer, h, gmix, w_in, w_out, conv_w, conv_b, w_a, b_a, w_x, b_x, lam, lb_raw,
               norm_g, pool_w, pool_scale, sconv_w):
    B, S, D = h.shape
    GW = GROUP_WIDTH
    T = min(MIX_TILE, S)
    assert S % T == 0 and T % HGRN_CHUNK == 0 and w_in.shape == (D, N_IN_SLICES * GW)
    depth = lb_raw.shape[0]
    wg = jnp.concatenate([_block_diag(w_a), _block_diag(w_x)], axis=1).astype(BF16)
    bg = jnp.concatenate([b_a, b_x])[None, :]
    head_of = jnp.arange(GW) // HEAD_DIM
    ones_bd = (head_of[:, None] == head_of[None, :]).astype(BF16)
    tri = jnp.tril(jnp.ones((HGRN_CHUNK, HGRN_CHUNK), BF16))
    small = [gmix[None, :], w_in.astype(BF16), w_out.astype(BF16), conv_w, conv_b[None, :], wg, bg,
             lam[None, :], lb_raw, jnp.tile(norm_g, N_GROUP_HEADS)[None, :],
             _block_diag(pool_w).astype(BF16), pool_scale[None, :], sconv_w, ones_bd, tri,
             _level_masks()]
    in_specs = [pl.BlockSpec((None, T, D), lambda b, t: (b, t, 0))]
    in_specs += [_const_spec(a.shape) for a in small]
    scratch = [
        pltpu.VMEM((T, N_IN_SLICES * GW), F32),
        pltpu.VMEM((HALO + T, GW), F32),
        pltpu.VMEM((HALO + T, GW), F32),
        pltpu.VMEM((HALO + T, GW), F32),
        pltpu.VMEM((V7X_SUBLANES + T, GW), F32),
        pltpu.VMEM((V7X_SUBLANES + T, GW), F32),
        pltpu.VMEM((2 * V7X_SUBLANES + T, GW), F32),
        pltpu.VMEM((2 * V7X_SUBLANES + T, GW), F32),
        pltpu.VMEM((2 * HALO + T, GW), F32),
        pltpu.VMEM((V7X_SUBLANES, GW), F32),
        pltpu.VMEM((GW, GW), F32),
        pltpu.VMEM((T, GW), F32),
        pltpu.VMEM((T, GW), F32),
        pltpu.VMEM((T, GW), F32),
        pltpu.VMEM((T, GW), F32),
        pltpu.VMEM((T, N_GROUP_HEADS * GW), BF16),
    ]
    return pl.pallas_call(
        functools.partial(_mix_kernel, layer),
        out_shape=jax.ShapeDtypeStruct((B, S, D), F32),
        grid=(B, S // T),
        in_specs=in_specs,
        out_specs=pl.BlockSpec((None, T, D), lambda b, t: (b, t, 0)),
        scratch_shapes=scratch,
        compiler_params=pltpu.CompilerParams(
            dimension_semantics=("arbitrary", "arbitrary"),
            vmem_limit_bytes=48 * 1024 * 1024),
        name=f"mix{layer}",
    )(h, *small)


def _ffn_kernel(final, h_ref, g_ref, gf_ref, wg_ref, wu_ref, wd_ref, out_ref, act_s):
    x = h_ref[...]
    hn = _rms_norm(x, g_ref[...]).astype(BF16)
    ff = wg_ref.shape[1]
    for c in range(0, ff, FFN_CHUNK):
        gate = _dot(hn, wg_ref[:, c:c + FFN_CHUNK])
        up = _dot(hn, wu_ref[:, c:c + FFN_CHUNK])
        act_s[:, c:c + FFN_CHUNK] = (jax.nn.silu(gate) * up).astype(BF16)
    y = x + _dot(act_s[...], wd_ref[...])
    out_ref[...] = _rms_norm(y, gf_ref[...]) if final else y


def _ffn_layer(h2, g, w_gate, w_up, w_down, g_final, final):
    n, d = h2.shape
    ff = w_gate.shape[1]
    tm = min(FFN_TILE, n)
    assert n % tm == 0 and ff % FFN_CHUNK == 0
    once = pl.Buffered(1)
    return pl.pallas_call(
        functools.partial(_ffn_kernel, final),
        out_shape=jax.ShapeDtypeStruct((n, d), F32),
        grid=(n // tm,),
        in_specs=[pl.BlockSpec((tm, d), lambda i: (i, 0)),
                  _const_spec((1, d)),
                  _const_spec((1, d)),
                  pl.BlockSpec((d, ff), lambda i: (0, 0), pipeline_mode=once),
                  pl.BlockSpec((d, ff), lambda i: (0, 0), pipeline_mode=once),
                  pl.BlockSpec((ff, d), lambda i: (0, 0), pipeline_mode=once)],
        out_specs=pl.BlockSpec((tm, d), lambda i: (i, 0)),
        scratch_shapes=[pltpu.VMEM((tm, ff), BF16)],
        compiler_params=pltpu.CompilerParams(
            dimension_semantics=("arbitrary",),
            vmem_limit_bytes=48 * 1024 * 1024),
        name="ffn",
    )(h2, g[None, :], g_final[None, :], w_gate.astype(BF16), w_up.astype(BF16), w_down.astype(BF16))


def _router_kernel(h_ref, g_ref, wr_ref, triu_ref, hn_ref, mi_ref, mf_ref, cnt_ref):
    n_e = wr_ref.shape[0]
    tr = h_ref.shape[0]
    hn = _rms_norm(h_ref[...], g_ref[...])
    hn_ref[...] = hn.astype(BF16)
    xh, xm, xl = _split3(hn)
    wh, wm, wl = _split3(wr_ref[...])
    logits = (_dot_nt(wh, xh) + _dot_nt(wh, xm) + _dot_nt(wm, xh)
              + _dot_nt(wh, xl) + _dot_nt(wl, xh) + _dot_nt(wm, xm))
    eid = lax.broadcasted_iota(I32, (n_e, tr), 0)
    m1 = jnp.max(logits, axis=0, keepdims=True)
    e1 = jnp.min(jnp.where(logits == m1, eid, n_e), axis=0, keepdims=True)
    rest = jnp.where(eid == e1, -jnp.inf, logits)
    m2 = jnp.max(rest, axis=0, keepdims=True)
    e2 = jnp.min(jnp.where(rest == m2, eid, n_e), axis=0, keepdims=True)
    ex = jnp.exp(m2 - m1)
    g1 = 1.0 / (1.0 + ex)
    g2 = ex / (1.0 + ex)
    member = jnp.where((eid == e1) | (eid == e2), 1.0, 0.0)
    incl = _dot(member.astype(BF16), triu_ref[...])
    rank = incl - member
    r1 = jnp.sum(jnp.where(eid == e1, rank, 0.0), axis=0, keepdims=True)
    r2 = jnp.sum(jnp.where(eid == e2, rank, 0.0), axis=0, keepdims=True)
    cnt_ref[...] = jnp.broadcast_to(incl[:, tr - 1:tr], cnt_ref.shape).astype(I32)
    zi = jnp.zeros((1, tr), I32)
    mi_ref[...] = jnp.concatenate(
        [e1, e2, r1.astype(I32), r2.astype(I32), zi, zi, zi, zi], axis=0)
    zf = jnp.zeros((1, tr), F32)
    mf_ref[...] = jnp.concatenate([g1, g2, zf, zf, zf, zf, zf, zf], axis=0)


def _router(h2, g, w_router):
    n, d = h2.shape
    n_e = w_router.shape[1]
    tr = min(MOE_TILE, n)
    assert n % tr == 0 and n_e == V7X_SUBLANES
    tok = jnp.arange(tr)
    triu = (tok[:, None] <= tok[None, :]).astype(BF16)
    return pl.pallas_call(
        _router_kernel,
        out_shape=(jax.ShapeDtypeStruct((n, d), BF16),
                   jax.ShapeDtypeStruct((V7X_SUBLANES, n), I32),
                   jax.ShapeDtypeStruct((V7X_SUBLANES, n), F32),
                   jax.ShapeDtypeStruct((n // tr, n_e, V7X_LANES), I32)),
        grid=(n // tr,),
        in_specs=[pl.BlockSpec((tr, d), lambda i: (i, 0)),
                  _const_spec((1, d)),
                  _const_spec((n_e, d)),
                  _const_spec((tr, tr))],
        out_specs=(pl.BlockSpec((tr, d), lambda i: (i, 0)),
                   pl.BlockSpec((V7X_SUBLANES, tr), lambda i: (0, i)),
                   pl.BlockSpec((V7X_SUBLANES, tr), lambda i: (0, i)),
                   pl.BlockSpec((None, n_e, V7X_LANES), lambda i: (i, 0, 0))),
        compiler_params=pltpu.CompilerParams(dimension_semantics=("arbitrary",)),
        name="router",
    )(h2, g[None, :], w_router.T, triu)


def _segment_copies(step, slot, tables, n_e, tile, local_ref, rows_ref, sem, to_rows, start):
    off_ref, row_ref, pad_ref = tables
    for e in range(n_e):
        off = off_ref[step * n_e + e]
        row = row_ref[step * n_e + e]
        pad = pad_ref[step * n_e + e]
        size = tile
        while size >= ROW_PAD:
            done = pad & (-2 * size)
            local = local_ref.at[slot, pl.ds(pl.multiple_of(off + done, ROW_PAD), size)]
            remote = rows_ref.at[pl.ds(pl.multiple_of(row + done, ROW_PAD), size)]
            cp = (pltpu.make_async_copy(local, remote, sem.at[slot]) if to_rows
                  else pltpu.make_async_copy(remote, local, sem.at[slot]))

            @pl.when((pad & size) != 0)
            def _():
                if start:
                    cp.start()
                else:
                    cp.wait()

            size //= 2


def _stack_rows(mi, off_ref, step, n_e):
    e1, e2, row1, row2 = mi[0:1, :], mi[1:2, :], mi[2:3, :], mi[3:4, :]
    for e in range(n_e):
        off = off_ref[step * n_e + e]
        row1 = row1 + jnp.where(e1 == e, off, 0)
        row2 = row2 + jnp.where(e2 == e, off, 0)
    return row1, row2


def _fill_copies(fill_ref, n_e, zero_s, xs_ref, sem, start):
    half = zero_s.shape[0]
    rb = 2 * half

    def go(cp):
        if start:
            cp.start()
        else:
            cp.wait()

    for e in range(n_e):
        row = fill_ref[e]
        gap = fill_ref[n_e + e]
        size = half
        while size >= ROW_PAD:
            done = gap & (-2 * size)
            cp = pltpu.make_async_copy(
                zero_s.at[pl.ds(0, size)],
                xs_ref.at[pl.ds(pl.multiple_of(row + done, ROW_PAD), size)], sem)
            pl.when((gap & size) != 0)(functools.partial(go, cp))
            size //= 2

    def block(b, c):
        for part in range(2):
            go(pltpu.make_async_copy(
                zero_s, xs_ref.at[pl.ds(pl.multiple_of(b * rb + part * half, half), half)], sem))
        return c

    lax.fori_loop(fill_ref[2 * n_e], xs_ref.shape[0] // rb, block, 0)


def _dispatch_kernel(nt, off_ref, row_ref, pad_ref, fill_ref, hn_ref, mi_ref, xs_ref, slab_s, zero_s,
                     sem, fill_sem):
    i = pl.program_id(0)
    tile = hn_ref.shape[0]
    n_e = V7X_SUBLANES
    r_stack = slab_s.shape[1]
    slot = lax.rem(i, 2)
    copies = functools.partial(_segment_copies, tables=(off_ref, row_ref, pad_ref), n_e=n_e,
                               tile=tile, local_ref=slab_s, rows_ref=xs_ref, sem=sem, to_rows=True)

    @pl.when(i == 0)
    def _():
        zero_s[...] = jnp.zeros(zero_s.shape, F32)
        _fill_copies(fill_ref, n_e, zero_s, xs_ref, fill_sem, start=True)

    @pl.when(i >= 2)
    def _():
        copies(i - 2, slot, start=False)

    row1, row2 = _stack_rows(mi_ref[...], off_ref, i, n_e)
    rid = lax.broadcasted_iota(I32, (r_stack, tile), 0)
    sel = jnp.where(rid == row1, 1.0, jnp.where(rid == row2, 1.0, 0.0)).astype(BF16)
    slab_s[slot] = _dot(sel, hn_ref[...])
    copies(i, slot, start=True)

    @pl.when(i == nt - 1)
    def _():
        if nt > 1:
            copies(i - 1, 1 - slot, start=False)
        copies(i, slot, start=False)
        _fill_copies(fill_ref, n_e, zero_s, xs_ref, fill_sem, start=False)


def _dispatch(hn, mi, tables, fill, n_rows):
    n, d = hn.shape
    tile = min(MOE_TILE, n)
    nt = n // tile
    r_stack = TOP_K * tile + V7X_SUBLANES * ROW_PAD
    return pl.pallas_call(
        functools.partial(_dispatch_kernel, nt),
        out_shape=jax.ShapeDtypeStruct((n_rows, d), F32),
        grid_spec=pltpu.PrefetchScalarGridSpec(
            num_scalar_prefetch=4,
            grid=(nt,),
            in_specs=[pl.BlockSpec((tile, d), lambda i, *_: (i, 0)),
                      pl.BlockSpec((V7X_SUBLANES, tile), lambda i, *_: (0, i))],
            out_specs=pl.BlockSpec(memory_space=pl.ANY),
            scratch_shapes=[pltpu.VMEM((2, r_stack, d), F32),
                            pltpu.VMEM((EXPERT_ROWS // 2, d), F32),
                            pltpu.SemaphoreType.DMA((2,)), pltpu.SemaphoreType.DMA(())]),
        compiler_params=pltpu.CompilerParams(dimension_semantics=("arbitrary",),
                                             has_side_effects=True),
        name="dispatch",
    )(*tables, fill, hn, mi)


def _expert_kernel(be_ref, rows_ref, src_ref, x_ref, w1_ref, w3_ref, w2_ref, out_ref,
                   w1_s, w3_s, w2_s):
    del be_ref, src_ref
    b = pl.program_id(0)
    f = pl.program_id(1)
    rows = rows_ref[b]

    @pl.when(f == 0)
    def _():
        out_ref[...] = jnp.zeros(out_ref.shape, F32)

    @pl.when(rows > 0)
    def _():
        w1_s[...] = w1_ref[...].astype(BF16)
        w3_s[...] = w3_ref[...].astype(BF16)
        w2_s[...] = w2_ref[...].astype(BF16)

        def swiglu_rows(r0, n_rows):
            xb = x_ref[pl.ds(r0, n_rows), :].astype(BF16)
            gate = _dot(xb, w1_s[...])
            up = _dot(xb, w3_s[...])
            act = (jax.nn.silu(gate) * up).astype(BF16)
            out_ref[pl.ds(r0, n_rows), :] += _dot(act, w2_s[...])

        @pl.when(rows == EXPERT_ROWS)
        def _():
            for s in range(EXPERT_ROWS // EXPERT_FULL):
                swiglu_rows(s * EXPERT_FULL, EXPERT_FULL)

        @pl.when(rows < EXPERT_ROWS)
        def _():
            def sub(s, c):
                swiglu_rows(pl.multiple_of(s * EXPERT_SUB, EXPERT_SUB), EXPERT_SUB)
                return c

            lax.fori_loop(0, (rows + EXPERT_SUB - 1) // EXPERT_SUB, sub, 0)


def _experts(xs, blk_e, blk_rows, blk_src, w1, w3, w2):
    n_rows, d = xs.shape
    n_e, _, ff = w1.shape
    rb = EXPERT_ROWS
    fft = min(EXPERT_FF, ff)
    assert n_rows % rb == 0 and ff % fft == 0
    nf = ff // fft
    nb = n_rows // rb

    def f_eff(b, f, rows):
        return jnp.where(rows[b] > 0, f, nf - 1)

    return pl.pallas_call(
        _expert_kernel,
        out_shape=jax.ShapeDtypeStruct((n_rows, d), F32),
        grid_spec=pltpu.PrefetchScalarGridSpec(
            num_scalar_prefetch=3,
            grid=(nb, nf),
            in_specs=[
                pl.BlockSpec((rb, d), lambda b, f, be, rows, src: (src[b], 0)),
                pl.BlockSpec((None, d, fft), lambda b, f, be, rows, src: (be[b], 0, f_eff(b, f, rows))),
                pl.BlockSpec((None, d, fft), lambda b, f, be, rows, src: (be[b], 0, f_eff(b, f, rows))),
                pl.BlockSpec((None, fft, d), lambda b, f, be, rows, src: (be[b], f_eff(b, f, rows), 0)),
            ],
            out_specs=pl.BlockSpec((rb, d), lambda b, f, be, rows, src: (b, 0)),
            scratch_shapes=[pltpu.VMEM((d, fft), BF16), pltpu.VMEM((d, fft), BF16),
                            pltpu.VMEM((fft, d), BF16)]),
        compiler_params=pltpu.CompilerParams(
            dimension_semantics=("arbitrary", "arbitrary"),
            vmem_limit_bytes=48 * 1024 * 1024),
        name="experts",
    )(blk_e, blk_rows, blk_src, xs, w1, w3, w2)


def _combine_kernel(final, nt, off_ref, row_ref, pad_ref, h_ref, mi_ref, gt_ref, g_ref, ys_ref,
                    out_ref, stack_s, sem):
    i = pl.program_id(0)
    tile = h_ref.shape[0]
    n_e = V7X_SUBLANES
    r_stack = stack_s.shape[1]
    slot = lax.rem(i, 2)
    copies = functools.partial(_segment_copies, tables=(off_ref, row_ref, pad_ref), n_e=n_e,
                               tile=tile, local_ref=stack_s, rows_ref=ys_ref, sem=sem, to_rows=False)

    @pl.when(i == 0)
    def _():
        stack_s[...] = jnp.zeros(stack_s.shape, F32)
        copies(i, slot, start=True)

    @pl.when(i + 1 < nt)
    def _():
        copies(i + 1, 1 - slot, start=True)

    copies(i, slot, start=False)
    y_rows = stack_s[slot].astype(BF16)
    row1, row2 = _stack_rows(mi_ref[...], off_ref, i, n_e)
    rid = lax.broadcasted_iota(I32, (r_stack, tile), 0)
    y1 = _dot_tn(jnp.where(rid == row1, 1.0, 0.0).astype(BF16), y_rows)
    y2 = _dot_tn(jnp.where(rid == row2, 1.0, 0.0).astype(BF16), y_rows)
    gates = gt_ref[...]
    y = h_ref[...] + y1 * gates[:, 0:1] + y2 * gates[:, 1:2]
    out_ref[...] = _rms_norm(y, g_ref[...]) if final else y


def _combine(h2, mi, gates, tables, ys, g_final, final):
    n, d = h2.shape
    tile = min(MOE_TILE, n)
    nt = n // tile
    r_stack = TOP_K * tile + V7X_SUBLANES * ROW_PAD
    return pl.pallas_call(
        functools.partial(_combine_kernel, final, nt),
        out_shape=jax.ShapeDtypeStruct((n, d), F32),
        grid_spec=pltpu.PrefetchScalarGridSpec(
            num_scalar_prefetch=3,
            grid=(nt,),
            in_specs=[pl.BlockSpec((tile, d), lambda i, *_: (i, 0)),
                      pl.BlockSpec((V7X_SUBLANES, tile), lambda i, *_: (0, i)),
                      pl.BlockSpec((tile, TOP_K), lambda i, *_: (i, 0)),
                      pl.BlockSpec((1, d), lambda i, *_: (0, 0)),
                      pl.BlockSpec(memory_space=pl.ANY)],
            out_specs=pl.BlockSpec((tile, d), lambda i, *_: (i, 0)),
            scratch_shapes=[pltpu.VMEM((2, r_stack, d), F32), pltpu.SemaphoreType.DMA((2,))]),
        compiler_params=pltpu.CompilerParams(dimension_semantics=("arbitrary",)),
        name="combine",
    )(*tables, h2, mi, gates, g_final[None, :], ys)


def _moe_layer(h2, g, w_router, w1, w3, w2, g_final, final=True):
    n, d = h2.shape
    n_e = w_router.shape[1]
    rb = EXPERT_ROWS
    tile = min(MOE_TILE, n)
    nt = n // tile
    hn, mi, mf, cnt = _router(h2, g, w_router)
    pad = (cnt[:, :, 0] + ROW_PAD - 1) // ROW_PAD * ROW_PAD
    off = jnp.cumsum(pad, axis=1) - pad
    total = jnp.sum(pad, axis=0)
    padded = (total + rb - 1) // rb * rb
    pend = jnp.cumsum(padded)
    pstart = pend - padded
    row = pstart[None, :] + jnp.cumsum(pad, axis=0) - pad
    tables = tuple(t.reshape(-1).astype(I32) for t in (off, row, pad))
    nb = (n * TOP_K + nt * n_e * (ROW_PAD - 1)) // rb + n_e
    blk = jnp.arange(nb, dtype=I32)
    n_used = pend[-1] // rb
    last = jnp.maximum(n_used - 1, 0)
    blk_src = jnp.minimum(blk, last)
    blk_e = jnp.minimum(jnp.sum(blk_src[:, None] * rb >= pend[None, :], axis=1), n_e - 1).astype(I32)
    blk_rows = jnp.where(blk < n_used,
                         jnp.clip(total[blk_e] - (blk * rb - pstart[blk_e]), 0, rb), 0).astype(I32)
    fill = jnp.concatenate([pstart + total, padded - total, n_used[None]]).astype(I32)
    xs = _dispatch(hn, mi, tables, fill, nb * rb)
    ys = _experts(xs, blk_e, blk_rows, blk_src, w1, w3, w2)
    return _combine(h2, mi, mf[:TOP_K].T, tables, ys, g_final, final)


def kernel(x, w_in, w_out, lru_conv_w, lru_conv_b, lru_w_a, lru_b_a, lru_w_x, lru_b_x, lru_lambda,
           hgrn_lower_bounds, hgrn_norm_g, pool_w, pool_scale, sconv_w, norm_mix_g, norm_ffn_g,
           ffn_w_gate, ffn_w_up, ffn_w_down, moe_w_router, moe_w1, moe_w3, moe_w2, final_norm_g):
    B, S, D = x.shape
    depth = w_in.shape[0]
    h = x
    for layer in range(depth):
        h = _mix_layer(layer, h, norm_mix_g[layer], w_in[layer], w_out[layer], lru_conv_w[layer],
                       lru_conv_b[layer], lru_w_a[layer], lru_b_a[layer], lru_w_x[layer],
                       lru_b_x[layer], lru_lambda[layer], hgrn_lower_bounds, hgrn_norm_g[layer],
                       pool_w[layer], pool_scale[layer], sconv_w[layer])
        j = layer // 2
        final = layer == depth - 1
        h2 = h.reshape(B * S, D)
        if layer % 2 == 0:
            h2 = _ffn_layer(h2, norm_ffn_g[layer], ffn_w_gate[j], ffn_w_up[j], ffn_w_down[j],
                            final_norm_g, final)
        else:
            h2 = _moe_layer(h2, norm_ffn_g[layer], moe_w_router[j], moe_w1[j], moe_w3[j], moe_w2[j],
                            final_norm_g, final)
        h = h2.reshape(B, S, D)
    return h
```

```python
import functools

import jax
import jax.numpy as jnp
from jax import lax
from jax.experimental import pallas as pl
from jax.experimental.pallas import tpu as pltpu

F32 = jnp.float32
BF16 = jnp.bfloat16
I32 = jnp.int32

GROUP_WIDTH = 256
N_GROUP_HEADS = 4
HEAD_DIM = 64
N_IN_SLICES = 10
LRU_CONV = 4
LRU_C = 8.0
HGRN_CHUNK = 64
POOL_WINDOWS = (2, 4, 8, 16)
SCONV_WIDTH = 3
TOP_K = 2
NORM_EPS = 1e-6
HEAD_NORM_EPS = 1e-5

V7X_LANES = 128
V7X_SUBLANES = 8
V7X_VMEM_BYTES = 64 * 1024 * 1024

MIX_TILE = 512
HALO = 16
DIAG = 8
FFN_TILE = 512
FFN_CHUNK = 256
MOE_TILE = 256
ROW_PAD = V7X_SUBLANES
EXPERT_ROWS = 1024
EXPERT_SUB = 256
EXPERT_FULL = 512
EXPERT_FF = 512


def _dot(a, b):
    return jnp.dot(a, b, preferred_element_type=F32)


def _dot_nt(a, b):
    return lax.dot_general(a, b, (((1,), (1,)), ((), ())), preferred_element_type=F32)


def _dot_tn(a, b):
    return lax.dot_general(a, b, (((0,), (0,)), ((), ())), preferred_element_type=F32)


def _rms_norm(x, g):
    ms = jnp.mean(x * x, axis=-1, keepdims=True)
    return x * lax.rsqrt(ms + NORM_EPS) * g


def _split3(x):
    hi = x.astype(BF16)
    r1 = x - hi.astype(F32)
    mid = r1.astype(BF16)
    lo = (r1 - mid.astype(F32)).astype(BF16)
    return hi, mid, lo


N_MIX_PARAMS = 16
N_MIX_SCRATCH = 16


def _mix_reset(ti, scratch):
    (proj_s, ax_s, p_s, sc_s, sa_s, su_s, ta_s, tu_s, tp_s, lruh_s, st_s, q_s, k_s, lf_s, o_s,
     y_s) = scratch
    GW = GROUP_WIDTH

    @pl.when(ti == 0)
    def _():
        ax_s[0:HALO, :] = jnp.zeros((HALO, GW), F32)
        p_s[0:HALO, :] = jnp.zeros((HALO, GW), F32)
        sc_s[0:HALO, :] = jnp.zeros((HALO, GW), F32)
        sa_s[0:V7X_SUBLANES, :] = jnp.ones((V7X_SUBLANES, GW), F32)
        su_s[0:V7X_SUBLANES, :] = jnp.zeros((V7X_SUBLANES, GW), F32)
        ta_s[0:V7X_SUBLANES, :] = jnp.zeros((V7X_SUBLANES, GW), F32)
        tu_s[0:V7X_SUBLANES, :] = jnp.zeros((V7X_SUBLANES, GW), F32)
        tp_s[0:HALO, :] = jnp.zeros((HALO, GW), F32)
        lruh_s[...] = jnp.zeros(lruh_s.shape, F32)
        st_s[...] = jnp.zeros(st_s.shape, F32)


def _mix_tile(layer, x, ti, params, scratch):
    (gmix_ref, w_in_ref, w_out_ref, cw_ref, cb_ref, wg_ref, bg_ref, lam_ref, lbraw_ref, hg_ref,
     wp_ref, ps_ref, sw_ref, ones_ref, tri_ref, lvl_ref) = params
    (proj_s, ax_s, p_s, sc_s, sa_s, su_s, ta_s, tu_s, tp_s, lruh_s, st_s, q_s, k_s, lf_s, o_s,
     y_s) = scratch
    T = x.shape[0]
    GW = GROUP_WIDTH
    hn = _rms_norm(x, gmix_ref[...]).astype(BF16)
    proj_s[...] = _dot(hn, w_in_ref[...])

    def sl(i):
        return proj_s[:, i * GW:(i + 1) * GW]

    ax_s[HALO:HALO + T, :] = sl(0)
    a_in = cb_ref[...]
    for kk in range(LRU_CONV):
        off = HALO - (LRU_CONV - 1) + kk
        a_in = a_in + cw_ref[kk:kk + 1, :] * ax_s[off:off + T, :]
    gates = _dot(a_in.astype(BF16), wg_ref[...]) + bg_ref[...]
    r_gate = jax.nn.sigmoid(gates[:, :GW])
    i_gate = jax.nn.sigmoid(gates[:, GW:])
    log_a = (-LRU_C) * r_gate * jax.nn.softplus(-lam_ref[...])
    a = jnp.exp(log_a)
    mult = jnp.sqrt(1.0 - a * a)
    u = mult * i_gate * a_in
    SUB = V7X_SUBLANES
    sa_s[SUB:SUB + T, :] = a
    su_s[SUB:SUB + T, :] = u
    a = sa_s[...]
    u = su_s[...]
    d = 1
    while d < SUB:
        ta_s[SUB:, :] = a
        tu_s[SUB:, :] = u
        a_sh = ta_s[SUB - d:SUB - d + SUB + T, :]
        u_sh = tu_s[SUB - d:SUB - d + SUB + T, :]
        u = a * u_sh + u
        a = a * a_sh
        d *= 2
    sa_s[0:SUB, :] = sa_s[T:T + SUB, :]
    su_s[0:SUB, :] = su_s[T:T + SUB, :]
    h_grp = lruh_s[...]
    groups = []
    for gi in range(1, T // SUB + 1):
        h_grp = a[gi * SUB:(gi + 1) * SUB, :] * h_grp + u[gi * SUB:(gi + 1) * SUB, :]
        groups.append(h_grp)
    lruh_s[...] = h_grp
    h_lru = jnp.concatenate(groups, axis=0)
    y_a = h_lru * jax.nn.gelu(sl(1))
    y_s[:, 0:GW] = y_a.astype(BF16)
    ax_s[0:HALO, :] = ax_s[T:T + HALO, :]

    p = sl(6)
    p_s[HALO:HALO + T, :] = p
    lane = lax.broadcasted_iota(I32, (1, GW), 1)
    win = jnp.where(lane < HEAD_DIM, float(POOL_WINDOWS[0]),
                    jnp.where(lane < 2 * HEAD_DIM, float(POOL_WINDOWS[1]),
                              jnp.where(lane < 3 * HEAD_DIM, float(POOL_WINDOWS[2]),
                                        float(POOL_WINDOWS[3]))))
    sums = []
    acc = p_s[...]
    d = 1
    while d < max(POOL_WINDOWS):
        tp_s[HALO:, :] = acc
        acc = acc + tp_s[HALO - d:HALO - d + HALO + T, :]
        sums.append(acc[HALO:, :])
        d *= 2
    wsum = jnp.where(lane < HEAD_DIM, sums[0],
                     jnp.where(lane < 2 * HEAD_DIM, sums[1],
                               jnp.where(lane < 3 * HEAD_DIM, sums[2], sums[3])))
    pos =(ti * T + 1 + lax.broadcasted_iota(I32, (T, 1), 0)).astype(F32)
    dpool = wsum / jnp.minimum(pos, win) - p
    y_c = _dot(dpool.astype(BF16), wp_ref[...]) * ps_ref[...]
    y_s[:, 2 * GW:3 * GW] = y_c.astype(BF16)
    p_s[0:HALO, :] = p_s[T:T + HALO, :]

    sc_s[HALO:HALO + T, :] = sl(8) * sl(9)
    conv = jnp.zeros((T, GW), F32)
    for kk in range(SCONV_WIDTH):
        off = HALO - (SCONV_WIDTH - 1) + kk
        conv = conv + sw_ref[kk:kk + 1, :] * sc_s[off:off + T, :]
    y_s[:, 3 * GW:4 * GW] = (sl(7) * conv).astype(BF16)
    sc_s[0:HALO, :] = sc_s[T:T + HALO, :]

    lbraw = lbraw_ref[...]
    e_lb = jnp.exp(lbraw - jnp.max(lbraw, axis=0, keepdims=True))
    p_lb = e_lb / jnp.sum(e_lb, axis=0, keepdims=True)
    lb = jnp.zeros((1, GW), F32)
    for li in range(1, layer + 1):
        lb = lb + p_lb[li:li + 1, :]
    z = sl(3)
    lf_s[...] = jnp.log(lb + (1.0 - lb) * jax.nn.sigmoid(z))
    k_s[...] = (1.0 - lb) * jax.nn.sigmoid(-z)
    q_s[...] = jax.nn.silu(sl(2))

    L = HGRN_CHUNK
    ones_bd = ones_ref[...]
    ones_f = ones_bd.astype(F32)
    tri = tri_ref[...]
    widths = _level_widths()
    row_l = lax.broadcasted_iota(I32, (L, 1), 0)
    row_d = lax.broadcasted_iota(I32, (DIAG, 1), 0)

    def chunk_body(c, carry):
        r0 = c * L
        lf = lf_s[pl.ds(r0, L), :]
        hi, mid, lo = _split3(lf)
        G = _dot(tri, hi) + _dot(tri, mid) + _dot(tri, lo)
        q = q_s[pl.ds(r0, L), :]
        k = k_s[pl.ds(r0, L), :]
        v = proj_s[pl.ds(r0, L), 4 * GW:5 * GW]
        vb = v.astype(BF16)
        g_last = G[L - 1:L, :]
        st = st_s[...]

        o = _dot_nt((q * jnp.exp(G)).astype(BF16), st.astype(BF16))

        scores = jnp.zeros((N_GROUP_HEADS * L, L), F32)
        for li, w in enumerate(widths):
            gref = G[w - 1:w, :]
            for pair in range(1, L // (2 * w)):
                lastrow = pair * 2 * w + w - 1
                gref = jnp.where(row_l < pair * 2 * w, gref, G[lastrow:lastrow + 1, :])
            qt = q * jnp.exp(jnp.minimum(G - gref, 0.0))
            kt = k * jnp.exp(jnp.minimum(gref - G, 0.0))
            q4 = jnp.concatenate([qt] * N_GROUP_HEADS, axis=0) * ones_f
            s4 = _dot_nt(q4.astype(BF16), kt.astype(BF16))
            scores = scores + s4 * lvl_ref[li]
        o4 = _dot(scores.astype(BF16), vb) * ones_f
        for hh in range(N_GROUP_HEADS):
            o = o + o4[hh * L:(hh + 1) * L, :]

        o_diag = []
        for b in range(L // DIAG):
            gs = G[b * DIAG:(b + 1) * DIAG, :]
            qs = q[b * DIAG:(b + 1) * DIAG, :]
            ks = k[b * DIAG:(b + 1) * DIAG, :]
            vs = v[b * DIAG:(b + 1) * DIAG, :]
            terms = []
            for j in range(DIAG):
                dg = jnp.where(row_d >= j, gs - gs[j:j + 1, :], -jnp.inf)
                terms.append((qs * (ks[j:j + 1, :] * jnp.exp(dg))).astype(BF16))
            e = jnp.concatenate(terms, axis=0)
            pd = _dot(e, ones_bd)
            od = jnp.zeros((DIAG, GW), F32)
            for j in range(DIAG):
                od = od + pd[j * DIAG:(j + 1) * DIAG, :] * vs[j:j + 1, :]
            o_diag.append(od)
        o = o + jnp.concatenate(o_diag, axis=0)
        o_s[pl.ds(r0, L), :] = o

        kd = (k * jnp.exp(g_last - G)).astype(BF16)
        st_s[...] = st * jnp.exp(g_last) + _dot_tn(vb, kd) * ones_f
        return carry

    for c in range(T // L):
        chunk_body(c, 0)

    o = o_s[...]
    ms = _dot((o * o).astype(BF16), ones_bd) * (1.0 / HEAD_DIM)
    o = o * lax.rsqrt(ms + HEAD_NORM_EPS) * hg_ref[...]
    y_s[:, GW:2 * GW] = (o * jax.nn.silu(sl(5))).astype(BF16)

    return x + _dot(y_s[...], w_out_ref[...])


def _mix_kernel(layer, h_ref, *refs):
    params = refs[:N_MIX_PARAMS]
    out_ref = refs[N_MIX_PARAMS]
    scratch = refs[N_MIX_PARAMS + 1:]
    _mix_reset(pl.program_id(1), scratch)
    out_ref[...] = _mix_tile(layer, h_ref[...], pl.program_id(1), params, scratch)


def _level_widths():
    widths = []
    w = DIAG
    while w < HGRN_CHUNK:
        widths.append(w)
        w *= 2
    return widths


def _level_masks():
    L = HGRN_CHUNK
    t = jnp.arange(N_GROUP_HEADS * L)[:, None] % L
    s = jnp.arange(L)[None, :]
    return jnp.stack([(((t // w) % 2 == 1) & (s // w == t // w - 1)).astype(F32)
                      for w in _level_widths()])


def _block_diag(w):
    return jax.scipy.linalg.block_diag(*[w[i] for i in range(w.shape[0])])


def _const_spec(shape):
    nd = len(shape)
    return pl.BlockSpec(shape, lambda *_: (0,) * nd)


def _mix_params(gmix, w_in, w_out, conv_w, conv_b, w_a, b_a, w_x, b_x, lam, lb_raw, norm_g,
                pool_w, pool_scale, sconv_w):
    GW = GROUP_WIDTH
    assert w_in.shape[1] == N_IN_SLICES * GW and POOL_WINDOWS == (2, 4, 8, 16)
    wg = jnp.concatenate([_block_diag(w_a), _block_diag(w_x)], axis=1).astype(BF16)
    bg = jnp.concatenate([b_a, b_x])[None, :]
    head_of = jnp.arange(GW) // HEAD_DIM
    ones_bd = (head_of[:, None] == head_of[None, :]).astype(BF16)
    tri = jnp.tril(jnp.ones((HGRN_CHUNK, HGRN_CHUNK), BF16))
    params = [gmix[None, :], w_in.astype(BF16), w_out.astype(BF16), conv_w, conv_b[None, :], wg, bg,
              lam[None, :], lb_raw, jnp.tile(norm_g, N_GROUP_HEADS)[None, :],
              _block_diag(pool_w).astype(BF16), pool_scale[None, :], sconv_w, ones_bd, tri,
              _level_masks()]
    assert len(params) == N_MIX_PARAMS
    return params


def _mix_scratch(T):
    GW = GROUP_WIDTH
    scratch = [
        pltpu.VMEM((T, N_IN_SLICES * GW), F32),
        pltpu.VMEM((HALO + T, GW), F32),
        pltpu.VMEM((HALO + T, GW), F32),
        pltpu.VMEM((HALO + T, GW), F32),
        pltpu.VMEM((V7X_SUBLANES + T, GW), F32),
        pltpu.VMEM((V7X_SUBLANES + T, GW), F32),
        pltpu.VMEM((2 * V7X_SUBLANES + T, GW), F32),
        pltpu.VMEM((2 * V7X_SUBLANES + T, GW), F32),
        pltpu.VMEM((2 * HALO + T, GW), F32),
        pltpu.VMEM((V7X_SUBLANES, GW), F32),
        pltpu.VMEM((GW, GW), F32),
        pltpu.VMEM((T, GW), F32),
        pltpu.VMEM((T, GW), F32),
        pltpu.VMEM((T, GW), F32),
        pltpu.VMEM((T, GW), F32),
        pltpu.VMEM((T, N_GROUP_HEADS * GW), BF16),
    ]
    assert len(scratch) == N_MIX_SCRATCH
    return scratch


def _mix_layer(layer, h, params):
    B, S, D = h.shape
    T = min(MIX_TILE, S)
    assert S % T == 0 and T % HGRN_CHUNK == 0
    in_specs = [pl.BlockSpec((None, T, D), lambda b, t: (b, t, 0))]
    in_specs += [_const_spec(a.shape) for a in params]
    return pl.pallas_call(
        functools.partial(_mix_kernel, layer),
        out_shape=jax.ShapeDtypeStruct((B, S, D), F32),
        grid=(B, S // T),
        in_specs=in_specs,
        out_specs=pl.BlockSpec((None, T, D), lambda b, t: (b, t, 0)),
        scratch_shapes=_mix_scratch(T),
        compiler_params=pltpu.CompilerParams(
            dimension_semantics=("arbitrary", "arbitrary"),
            vmem_limit_bytes=48 * 1024 * 1024),
        name=f"mix{layer}",
    )(h, *params)


def _ffn_tile(x, g_ref, wg_ref, wu_ref, wd_ref, act_s):
    hn = _rms_norm(x, g_ref[...]).astype(BF16)
    ff = wg_ref.shape[1]
    for c in range(0, ff, FFN_CHUNK):
        gate = _dot(hn, wg_ref[:, c:c + FFN_CHUNK])
        up = _dot(hn, wu_ref[:, c:c + FFN_CHUNK])
        act_s[:, c:c + FFN_CHUNK] = (jax.nn.silu(gate) * up).astype(BF16)
    return x + _dot(act_s[...], wd_ref[...])


def _ffn_kernel(final, h_ref, g_ref, gf_ref, wg_ref, wu_ref, wd_ref, out_ref, act_s):
    y = _ffn_tile(h_ref[...], g_ref, wg_ref, wu_ref, wd_ref, act_s)
    out_ref[...] = _rms_norm(y, gf_ref[...]) if final else y


def _ffn_layer(h2, g, w_gate, w_up, w_down, g_final, final):
    n, d = h2.shape
    ff = w_gate.shape[1]
    tm = min(FFN_TILE, n)
    assert n % tm == 0 and ff % FFN_CHUNK == 0
    once = pl.Buffered(1)
    return pl.pallas_call(
        functools.partial(_ffn_kernel, final),
        out_shape=jax.ShapeDtypeStruct((n, d), F32),
        grid=(n // tm,),
        in_specs=[pl.BlockSpec((tm, d), lambda i: (i, 0)),
                  _const_spec((1, d)),
                  _const_spec((1, d)),
                  pl.BlockSpec((d, ff), lambda i: (0, 0), pipeline_mode=once),
                  pl.BlockSpec((d, ff), lambda i: (0, 0), pipeline_mode=once),
                  pl.BlockSpec((ff, d), lambda i: (0, 0), pipeline_mode=once)],
        out_specs=pl.BlockSpec((tm, d), lambda i: (i, 0)),
        scratch_shapes=[pltpu.VMEM((tm, ff), BF16)],
        compiler_params=pltpu.CompilerParams(
            dimension_semantics=("arbitrary",),
            vmem_limit_bytes=48 * 1024 * 1024),
        name="ffn",
    )(h2, g[None, :], g_final[None, :], w_gate.astype(BF16), w_up.astype(BF16), w_down.astype(BF16))


def _router_kernel(h_ref, g_ref, wr_ref, triu_ref, hn_ref, mi_ref, mf_ref, cnt_ref):
    n_e = wr_ref.shape[0]
    tr = h_ref.shape[0]
    hn = _rms_norm(h_ref[...], g_ref[...])
    hn_ref[...] = hn.astype(BF16)
    xh, xm, xl = _split3(hn)
    wh, wm, wl = _split3(wr_ref[...])
    logits = (_dot_nt(wh, xh) + _dot_nt(wh, xm) + _dot_nt(wm, xh)
              + _dot_nt(wh, xl) + _dot_nt(wl, xh) + _dot_nt(wm, xm))
    eid = lax.broadcasted_iota(I32, (n_e, tr), 0)
    m1 = jnp.max(logits, axis=0, keepdims=True)
    e1 = jnp.min(jnp.where(logits == m1, eid, n_e), axis=0, keepdims=True)
    rest = jnp.where(eid == e1, -jnp.inf, logits)
    m2 = jnp.max(rest, axis=0, keepdims=True)
    e2 = jnp.min(jnp.where(rest == m2, eid, n_e), axis=0, keepdims=True)
    ex = jnp.exp(m2 - m1)
    g1 = 1.0 / (1.0 + ex)
    g2 = ex / (1.0 + ex)
    member = jnp.where((eid == e1) | (eid == e2), 1.0, 0.0)
    incl = _dot(member.astype(BF16), triu_ref[...])
    rank = incl - member
    r1 = jnp.sum(jnp.where(eid == e1, rank, 0.0), axis=0, keepdims=True)
    r2 = jnp.sum(jnp.where(eid == e2, rank, 0.0), axis=0, keepdims=True)
    cnt_ref[...] = jnp.broadcast_to(incl[:, tr - 1:tr], cnt_ref.shape).astype(I32)
    zi = jnp.zeros((1, tr), I32)
    mi_ref[...] = jnp.concatenate(
        [e1, e2, r1.astype(I32), r2.astype(I32), zi, zi, zi, zi], axis=0)
    zf = jnp.zeros((1, tr), F32)
    mf_ref[...] = jnp.concatenate([g1, g2, zf, zf, zf, zf, zf, zf], axis=0)


def _router(h2, g, w_router):
    n, d = h2.shape
    n_e = w_router.shape[1]
    tr = min(MOE_TILE, n)
    assert n % tr == 0 and n_e == V7X_SUBLANES
    tok = jnp.arange(tr)
    triu = (tok[:, None] <= tok[None, :]).astype(BF16)
    return pl.pallas_call(
        _router_kernel,
        out_shape=(jax.ShapeDtypeStruct((n, d), BF16),
                   jax.ShapeDtypeStruct((V7X_SUBLANES, n), I32),
                   jax.ShapeDtypeStruct((V7X_SUBLANES, n), F32),
                   jax.ShapeDtypeStruct((n // tr, n_e, V7X_LANES), I32)),
        grid=(n // tr,),
        in_specs=[pl.BlockSpec((tr, d), lambda i: (i, 0)),
                  _const_spec((1, d)),
                  _const_spec((n_e, d)),
                  _const_spec((tr, tr))],
        out_specs=(pl.BlockSpec((tr, d), lambda i: (i, 0)),
                   pl.BlockSpec((V7X_SUBLANES, tr), lambda i: (0, i)),
                   pl.BlockSpec((V7X_SUBLANES, tr), lambda i: (0, i)),
                   pl.BlockSpec((None, n_e, V7X_LANES), lambda i: (i, 0, 0))),
        compiler_params=pltpu.CompilerParams(dimension_semantics=("arbitrary",)),
        name="router",
    )(h2, g[None, :], w_router.T, triu)


def _segment_copies(step, slot, tables, n_e, tile, local_ref, rows_ref, sem, to_rows, start):
    off_ref, row_ref, pad_ref = tables
    for e in range(n_e):
        off = off_ref[step * n_e + e]
        row = row_ref[step * n_e + e]
        pad = pad_ref[step * n_e + e]
        size = tile
        while size >= ROW_PAD:
            done = pad & (-2 * size)
            local = local_ref.at[slot, pl.ds(pl.multiple_of(off + done, ROW_PAD), size)]
            remote = rows_ref.at[pl.ds(pl.multiple_of(row + done, ROW_PAD), size)]
            cp = (pltpu.make_async_copy(local, remote, sem.at[slot]) if to_rows
                  else pltpu.make_async_copy(remote, local, sem.at[slot]))

            @pl.when((pad & size) != 0)
            def _():
                if start:
                    cp.start()
                else:
                    cp.wait()

            size //= 2


def _stack_rows(mi, off_ref, step, n_e):
    e1, e2, row1, row2 = mi[0:1, :], mi[1:2, :], mi[2:3, :], mi[3:4, :]
    for e in range(n_e):
        off = off_ref[step * n_e + e]
        row1 = row1 + jnp.where(e1 == e, off, 0)
        row2 = row2 + jnp.where(e2 == e, off, 0)
    return row1, row2


def _fill_copies(fill_ref, n_e, zero_s, xs_ref, sem, start):
    half = zero_s.shape[0]
    rb = 2 * half

    def go(cp):
        if start:
            cp.start()
        else:
            cp.wait()

    for e in range(n_e):
        row = fill_ref[e]
        gap = fill_ref[n_e + e]
        size = half
        while size >= ROW_PAD:
            done = gap & (-2 * size)
            cp = pltpu.make_async_copy(
                zero_s.at[pl.ds(0, size)],
                xs_ref.at[pl.ds(pl.multiple_of(row + done, ROW_PAD), size)], sem)
            pl.when((gap & size) != 0)(functools.partial(go, cp))
            size //= 2

    def block(b, c):
        for part in range(2):
            go(pltpu.make_async_copy(
                zero_s, xs_ref.at[pl.ds(pl.multiple_of(b * rb + part * half, half), half)], sem))
        return c

    lax.fori_loop(fill_ref[2 * n_e], xs_ref.shape[0] // rb, block, 0)


def _dispatch_kernel(nt, off_ref, row_ref, pad_ref, fill_ref, hn_ref, mi_ref, xs_ref, slab_s, zero_s,
                     sem, fill_sem):
    i = pl.program_id(0)
    tile = hn_ref.shape[0]
    n_e = V7X_SUBLANES
    r_stack = slab_s.shape[1]
    slot = lax.rem(i, 2)
    copies = functools.partial(_segment_copies, tables=(off_ref, row_ref, pad_ref), n_e=n_e,
                               tile=tile, local_ref=slab_s, rows_ref=xs_ref, sem=sem, to_rows=True)

    @pl.when(i == 0)
    def _():
        zero_s[...] = jnp.zeros(zero_s.shape, F32)
        _fill_copies(fill_ref, n_e, zero_s, xs_ref, fill_sem, start=True)

    @pl.when(i >= 2)
    def _():
        copies(i - 2, slot, start=False)

    row1, row2 = _stack_rows(mi_ref[...], off_ref, i, n_e)
    rid = lax.broadcasted_iota(I32, (r_stack, tile), 0)
    sel = jnp.where(rid == row1, 1.0, jnp.where(rid == row2, 1.0, 0.0)).astype(BF16)
    slab_s[slot] = _dot(sel, hn_ref[...])
    copies(i, slot, start=True)

    @pl.when(i == nt - 1)
    def _():
        if nt > 1:
            copies(i - 1, 1 - slot, start=False)
        copies(i, slot, start=False)
        _fill_copies(fill_ref, n_e, zero_s, xs_ref, fill_sem, start=False)


def _dispatch(hn, mi, tables, fill, n_rows):
    n, d = hn.shape
    tile = min(MOE_TILE, n)
    nt = n // tile
    r_stack = TOP_K * tile + V7X_SUBLANES * ROW_PAD
    return pl.pallas_call(
        functools.partial(_dispatch_kernel, nt),
        out_shape=jax.ShapeDtypeStruct((n_rows, d), F32),
        grid_spec=pltpu.PrefetchScalarGridSpec(
            num_scalar_prefetch=4,
            grid=(nt,),
            in_specs=[pl.BlockSpec((tile, d), lambda i, *_: (i, 0)),
                      pl.BlockSpec((V7X_SUBLANES, tile), lambda i, *_: (0, i))],
            out_specs=pl.BlockSpec(memory_space=pl.ANY),
            scratch_shapes=[pltpu.VMEM((2, r_stack, d), F32),
                            pltpu.VMEM((EXPERT_ROWS // 2, d), F32),
                            pltpu.SemaphoreType.DMA((2,)), pltpu.SemaphoreType.DMA(())]),
        compiler_params=pltpu.CompilerParams(dimension_semantics=("arbitrary",),
                                             has_side_effects=True),
        name="dispatch",
    )(*tables, fill, hn, mi)


def _expert_kernel(be_ref, rows_ref, src_ref, x_ref, w1_ref, w3_ref, w2_ref, out_ref,
                   w1_s, w3_s, w2_s):
    del be_ref, src_ref
    b = pl.program_id(0)
    f = pl.program_id(1)
    rows = rows_ref[b]

    @pl.when(f == 0)
    def _():
        out_ref[...] = jnp.zeros(out_ref.shape, F32)

    def cast_weights():
        w1_s[...] = w1_ref[...].astype(BF16)
        w3_s[...] = w3_ref[...].astype(BF16)
        w2_s[...] = w2_ref[...].astype(BF16)

    def swiglu_rows(r0, n_rows):
        xb = x_ref[pl.ds(r0, n_rows), :].astype(BF16)
        gate = _dot(xb, w1_s[...])
        up = _dot(xb, w3_s[...])
        act = (jax.nn.silu(gate) * up).astype(BF16)
        out_ref[pl.ds(r0, n_rows), :] += _dot(act, w2_s[...])

    @pl.when(rows == EXPERT_ROWS)
    def _():
        cast_weights()
        for s in range(EXPERT_ROWS // EXPERT_FULL):
            swiglu_rows(s * EXPERT_FULL, EXPERT_FULL)

    @pl.when((rows > 0) & (rows < EXPERT_ROWS))
    def _():
        cast_weights()

        def sub(s, c):
            swiglu_rows(pl.multiple_of(s * EXPERT_SUB, EXPERT_SUB), EXPERT_SUB)
            return c

        lax.fori_loop(0, (rows + EXPERT_SUB - 1) // EXPERT_SUB, sub, 0)


def _experts(xs, blk_e, blk_rows, blk_src, w1, w3, w2):
    n_rows, d = xs.shape
    n_e, _, ff = w1.shape
    rb = EXPERT_ROWS
    fft = min(EXPERT_FF, ff)
    assert n_rows % rb == 0 and ff % fft == 0
    nf = ff // fft
    nb = n_rows // rb

    def f_eff(b, f, rows):
        return jnp.where(rows[b] > 0, f, nf - 1)

    return pl.pallas_call(
        _expert_kernel,
        out_shape=jax.ShapeDtypeStruct((n_rows, d), F32),
        grid_spec=pltpu.PrefetchScalarGridSpec(
            num_scalar_prefetch=3,
            grid=(nb, nf),
            in_specs=[
                pl.BlockSpec((rb, d), lambda b, f, be, rows, src: (src[b], 0)),
                pl.BlockSpec((None, d, fft), lambda b, f, be, rows, src: (be[b], 0, f_eff(b, f, rows))),
                pl.BlockSpec((None, d, fft), lambda b, f, be, rows, src: (be[b], 0, f_eff(b, f, rows))),
                pl.BlockSpec((None, fft, d), lambda b, f, be, rows, src: (be[b], f_eff(b, f, rows), 0)),
            ],
            out_specs=pl.BlockSpec((rb, d), lambda b, f, be, rows, src: (b, 0)),
            scratch_shapes=[pltpu.VMEM((d, fft), BF16), pltpu.VMEM((d, fft), BF16),
                            pltpu.VMEM((fft, d), BF16)]),
        compiler_params=pltpu.CompilerParams(
            dimension_semantics=("arbitrary", "arbitrary"),
            vmem_limit_bytes=48 * 1024 * 1024),
        name="experts",
    )(blk_e, blk_rows, blk_src, xs, w1, w3, w2)


def _combine_kernel(final, nt, off_ref, row_ref, pad_ref, h_ref, mi_ref, gt_ref, g_ref, ys_ref,
                    out_ref, stack_s, sem):
    i = pl.program_id(0)
    tile = h_ref.shape[0]
    n_e = V7X_SUBLANES
    r_stack = stack_s.shape[1]
    slot = lax.rem(i, 2)
    copies = functools.partial(_segment_copies, tables=(off_ref, row_ref, pad_ref), n_e=n_e,
                               tile=tile, local_ref=stack_s, rows_ref=ys_ref, sem=sem, to_rows=False)

    @pl.when(i == 0)
    def _():
        stack_s[...] = jnp.zeros(stack_s.shape, F32)
        copies(i, slot, start=True)

    @pl.when(i + 1 < nt)
    def _():
        copies(i + 1, 1 - slot, start=True)

    copies(i, slot, start=False)
    y_rows = stack_s[slot].astype(BF16)
    row1, row2 = _stack_rows(mi_ref[...], off_ref, i, n_e)
    rid = lax.broadcasted_iota(I32, (r_stack, tile), 0)
    y1 = _dot_tn(jnp.where(rid == row1, 1.0, 0.0).astype(BF16), y_rows)
    y2 = _dot_tn(jnp.where(rid == row2, 1.0, 0.0).astype(BF16), y_rows)
    gates = gt_ref[...]
    y = h_ref[...] + y1 * gates[:, 0:1] + y2 * gates[:, 1:2]
    out_ref[...] = _rms_norm(y, g_ref[...]) if final else y


def _combine(h2, mi, gates, tables, ys, g_final, final):
    n, d = h2.shape
    tile = min(MOE_TILE, n)
    nt = n // tile
    r_stack = TOP_K * tile + V7X_SUBLANES * ROW_PAD
    return pl.pallas_call(
        functools.partial(_combine_kernel, final, nt),
        out_shape=jax.ShapeDtypeStruct((n, d), F32),
        grid_spec=pltpu.PrefetchScalarGridSpec(
            num_scalar_prefetch=3,
            grid=(nt,),
            in_specs=[pl.BlockSpec((tile, d), lambda i, *_: (i, 0)),
                      pl.BlockSpec((V7X_SUBLANES, tile), lambda i, *_: (0, i)),
                      pl.BlockSpec((tile, TOP_K), lambda i, *_: (i, 0)),
                      pl.BlockSpec((1, d), lambda i, *_: (0, 0)),
                      pl.BlockSpec(memory_space=pl.ANY)],
            out_specs=pl.BlockSpec((tile, d), lambda i, *_: (i, 0)),
            scratch_shapes=[pltpu.VMEM((2, r_stack, d), F32), pltpu.SemaphoreType.DMA((2,))]),
        compiler_params=pltpu.CompilerParams(dimension_semantics=("arbitrary",)),
        name="combine",
    )(*tables, h2, mi, gates, g_final[None, :], ys)


def _moe_layer(h2, g, w_router, w1, w3, w2, g_final, final=True):
    n, d = h2.shape
    n_e = w_router.shape[1]
    rb = EXPERT_ROWS
    tile = min(MOE_TILE, n)
    nt = n // tile
    hn, mi, mf, cnt = _router(h2, g, w_router)
    pad = (cnt[:, :, 0] + ROW_PAD - 1) // ROW_PAD * ROW_PAD
    off = jnp.cumsum(pad, axis=1) - pad
    total = jnp.sum(pad, axis=0)
    padded = (total + rb - 1) // rb * rb
    pend = jnp.cumsum(padded)
    pstart = pend - padded
    row = pstart[None, :] + jnp.cumsum(pad, axis=0) - pad
    tables = tuple(t.reshape(-1).astype(I32) for t in (off, row, pad))
    nb = (n * TOP_K + nt * n_e * (ROW_PAD - 1)) // rb + n_e
    blk = jnp.arange(nb, dtype=I32)
    n_used = pend[-1] // rb
    last = jnp.maximum(n_used - 1, 0)
    blk_src = jnp.minimum(blk, last)
    blk_e = jnp.minimum(jnp.sum(blk_src[:, None] * rb >= pend[None, :], axis=1), n_e - 1).astype(I32)
    blk_rows = jnp.where(blk < n_used,
                         jnp.clip(total[blk_e] - (blk * rb - pstart[blk_e]), 0, rb), 0).astype(I32)
    fill = jnp.concatenate([pstart + total, padded - total, n_used[None]]).astype(I32)
    xs = _dispatch(hn, mi, tables, fill, nb * rb)
    ys = _experts(xs, blk_e, blk_rows, blk_src, w1, w3, w2)
    return _combine(h2, mi, mf[:TOP_K].T, tables, ys, g_final, final)


def kernel(x, w_in, w_out, lru_conv_w, lru_conv_b, lru_w_a, lru_b_a, lru_w_x, lru_b_x, lru_lambda,
           hgrn_lower_bounds, hgrn_norm_g, pool_w, pool_scale, sconv_w, norm_mix_g, norm_ffn_g,
           ffn_w_gate, ffn_w_up, ffn_w_down, moe_w_router, moe_w1, moe_w3, moe_w2, final_norm_g):
    B, S, D = x.shape
    depth = w_in.shape[0]

    def mix_params(layer):
        return _mix_params(norm_mix_g[layer], w_in[layer], w_out[layer], lru_conv_w[layer],
                           lru_conv_b[layer], lru_w_a[layer], lru_b_a[layer], lru_w_x[layer],
                           lru_b_x[layer], lru_lambda[layer], hgrn_lower_bounds, hgrn_norm_g[layer],
                           pool_w[layer], pool_scale[layer], sconv_w[layer])

    def ffn_layer(h2, layer):
        j = layer // 2
        final = layer == depth - 1
        if layer % 2 == 0:
            return _ffn_layer(h2, norm_ffn_g[layer], ffn_w_gate[j], ffn_w_up[j], ffn_w_down[j],
                              final_norm_g, final)
        return _moe_layer(h2, norm_ffn_g[layer], moe_w_router[j], moe_w1[j], moe_w3[j], moe_w2[j],
                          final_norm_g, final)

    h = x
    for layer in range(depth):
        h = _mix_layer(layer, h, mix_params(layer))
        h = ffn_layer(h.reshape(B * S, D), layer).reshape(B, S, D)
    return h
```

```python
import functools

import jax
import jax.numpy as jnp
from jax import lax
from jax.experimental import pallas as pl
from jax.experimental.pallas import tpu as pltpu

F32 = jnp.float32
BF16 = jnp.bfloat16
I32 = jnp.int32

GROUP_WIDTH = 256
N_GROUP_HEADS = 4
HEAD_DIM = 64
N_IN_SLICES = 10
LRU_CONV = 4
LRU_C = 8.0
HGRN_CHUNK = 64
POOL_WINDOWS = (2, 4, 8, 16)
SCONV_WIDTH = 3
TOP_K = 2
NORM_EPS = 1e-6
HEAD_NORM_EPS = 1e-5

V7X_LANES = 128
V7X_SUBLANES = 8
V7X_VMEM_BYTES = 64 * 1024 * 1024

MIX_TILE = 512
HALO = 16
DIAG = 8
FFN_TILE = 512
FFN_CHUNK = 256
MOE_TILE = 256
ROW_PAD = V7X_SUBLANES
EXPERT_ROWS = 2048
EXPERT_VMEM_BYTES = 58 * 1024 * 1024
EXPERT_SUB = 256
EXPERT_FULL = 512
EXPERT_FF = 512


def _dot(a, b):
    return jnp.dot(a, b, preferred_element_type=F32)


def _dot_nt(a, b):
    return lax.dot_general(a, b, (((1,), (1,)), ((), ())), preferred_element_type=F32)


def _dot_tn(a, b):
    return lax.dot_general(a, b, (((0,), (0,)), ((), ())), preferred_element_type=F32)


def _rms_norm(x, g):
    ms = jnp.mean(x * x, axis=-1, keepdims=True)
    return x * lax.rsqrt(ms + NORM_EPS) * g


def _split3(x):
    hi = x.astype(BF16)
    r1 = x - hi.astype(F32)
    mid = r1.astype(BF16)
    lo = (r1 - mid.astype(F32)).astype(BF16)
    return hi, mid, lo


N_MIX_PARAMS = 16
N_MIX_SCRATCH = 16


def _mix_reset(ti, scratch):
    (proj_s, ax_s, p_s, sc_s, sa_s, su_s, ta_s, tu_s, tp_s, lruh_s, st_s, q_s, k_s, lf_s, o_s,
     y_s) = scratch
    GW = GROUP_WIDTH

    @pl.when(ti == 0)
    def _():
        ax_s[0:HALO, :] = jnp.zeros((HALO, GW), F32)
        p_s[0:HALO, :] = jnp.zeros((HALO, GW), F32)
        sc_s[0:HALO, :] = jnp.zeros((HALO, GW), F32)
        sa_s[0:V7X_SUBLANES, :] = jnp.ones((V7X_SUBLANES, GW), F32)
        su_s[0:V7X_SUBLANES, :] = jnp.zeros((V7X_SUBLANES, GW), F32)
        ta_s[0:V7X_SUBLANES, :] = jnp.zeros((V7X_SUBLANES, GW), F32)
        tu_s[0:V7X_SUBLANES, :] = jnp.zeros((V7X_SUBLANES, GW), F32)
        tp_s[0:HALO, :] = jnp.zeros((HALO, GW), F32)
        lruh_s[...] = jnp.zeros(lruh_s.shape, F32)
        st_s[...] = jnp.zeros(st_s.shape, F32)


def _mix_tile(layer, x, ti, params, scratch):
    (gmix_ref, w_in_ref, w_out_ref, cw_ref, cb_ref, wg_ref, bg_ref, lam_ref, lbraw_ref, hg_ref,
     wp_ref, ps_ref, sw_ref, ones_ref, tri_ref, lvl_ref) = params
    (proj_s, ax_s, p_s, sc_s, sa_s, su_s, ta_s, tu_s, tp_s, lruh_s, st_s, q_s, k_s, lf_s, o_s,
     y_s) = scratch
    T = x.shape[0]
    GW = GROUP_WIDTH
    hn = _rms_norm(x, gmix_ref[...]).astype(BF16)
    proj_s[...] = _dot(hn, w_in_ref[...])

    def sl(i):
        return proj_s[:, i * GW:(i + 1) * GW]

    ax_s[HALO:HALO + T, :] = sl(0)
    a_in = cb_ref[...]
    for kk in range(LRU_CONV):
        off = HALO - (LRU_CONV - 1) + kk
        a_in = a_in + cw_ref[kk:kk + 1, :] * ax_s[off:off + T, :]
    gates = _dot(a_in.astype(BF16), wg_ref[...]) + bg_ref[...]
    r_gate = jax.nn.sigmoid(gates[:, :GW])
    i_gate = jax.nn.sigmoid(gates[:, GW:])
    log_a = (-LRU_C) * r_gate * jax.nn.softplus(-lam_ref[...])
    a = jnp.exp(log_a)
    mult = jnp.sqrt(1.0 - a * a)
    u = mult * i_gate * a_in
    SUB = V7X_SUBLANES
    sa_s[SUB:SUB + T, :] = a
    su_s[SUB:SUB + T, :] = u
    a = sa_s[...]
    u = su_s[...]
    d = 1
    while d < SUB:
        ta_s[SUB:, :] = a
        tu_s[SUB:, :] = u
        a_sh = ta_s[SUB - d:SUB - d + SUB + T, :]
        u_sh = tu_s[SUB - d:SUB - d + SUB + T, :]
        u = a * u_sh + u
        a = a * a_sh
        d *= 2
    sa_s[0:SUB, :] = sa_s[T:T + SUB, :]
    su_s[0:SUB, :] = su_s[T:T + SUB, :]
    h_grp = lruh_s[...]
    groups = []
    for gi in range(1, T // SUB + 1):
        h_grp = a[gi * SUB:(gi + 1) * SUB, :] * h_grp + u[gi * SUB:(gi + 1) * SUB, :]
        groups.append(h_grp)
    lruh_s[...] = h_grp
    h_lru = jnp.concatenate(groups, axis=0)
    y_a = h_lru * jax.nn.gelu(sl(1))
    y_s[:, 0:GW] = y_a.astype(BF16)
    ax_s[0:HALO, :] = ax_s[T:T + HALO, :]

    p = sl(6)
    p_s[HALO:HALO + T, :] = p
    lane = lax.broadcasted_iota(I32, (1, GW), 1)
    win = jnp.where(lane < HEAD_DIM, float(POOL_WINDOWS[0]),
                    jnp.where(lane < 2 * HEAD_DIM, float(POOL_WINDOWS[1]),
                              jnp.where(lane < 3 * HEAD_DIM, float(POOL_WINDOWS[2]),
                                        float(POOL_WINDOWS[3]))))
    sums = []
    acc = p_s[...]
    d = 1
    while d < max(POOL_WINDOWS):
        tp_s[HALO:, :] = acc
        acc = acc + tp_s[HALO - d:HALO - d + HALO + T, :]
        sums.append(acc[HALO:, :])
        d *= 2
    wsum = jnp.where(lane < HEAD_DIM, sums[0],
                     jnp.where(lane < 2 * HEAD_DIM, sums[1],
                               jnp.where(lane < 3 * HEAD_DIM, sums[2], sums[3])))
    pos =(ti * T + 1 + lax.broadcasted_iota(I32, (T, 1), 0)).astype(F32)
    dpool = wsum / jnp.minimum(pos, win) - p
    y_c = _dot(dpool.astype(BF16), wp_ref[...]) * ps_ref[...]
    y_s[:, 2 * GW:3 * GW] = y_c.astype(BF16)
    p_s[0:HALO, :] = p_s[T:T + HALO, :]

    sc_s[HALO:HALO + T, :] = sl(8) * sl(9)
    conv = jnp.zeros((T, GW), F32)
    for kk in range(SCONV_WIDTH):
        off = HALO - (SCONV_WIDTH - 1) + kk
        conv = conv + sw_ref[kk:kk + 1, :] * sc_s[off:off + T, :]
    y_s[:, 3 * GW:4 * GW] = (sl(7) * conv).astype(BF16)
    sc_s[0:HALO, :] = sc_s[T:T + HALO, :]

    lbraw = lbraw_ref[...]
    e_lb = jnp.exp(lbraw - jnp.max(lbraw, axis=0, keepdims=True))
    p_lb = e_lb / jnp.sum(e_lb, axis=0, keepdims=True)
    lb = jnp.zeros((1, GW), F32)
    for li in range(1, layer + 1):
        lb = lb + p_lb[li:li + 1, :]
    z = sl(3)
    lf_s[...] = jnp.log(lb + (1.0 - lb) * jax.nn.sigmoid(z))
    k_s[...] = (1.0 - lb) * jax.nn.sigmoid(-z)
    q_s[...] = jax.nn.silu(sl(2))

    L = HGRN_CHUNK
    ones_bd = ones_ref[...]
    ones_f = ones_bd.astype(F32)
    tri = tri_ref[...]
    widths = _level_widths()
    row_l = lax.broadcasted_iota(I32, (L, 1), 0)
    row_d = lax.broadcasted_iota(I32, (DIAG, 1), 0)

    def chunk_body(c, carry):
        r0 = c * L
        lf = lf_s[pl.ds(r0, L), :]
        hi, mid, lo = _split3(lf)
        G = _dot(tri, hi) + _dot(tri, mid) + _dot(tri, lo)
        q = q_s[pl.ds(r0, L), :]
        k = k_s[pl.ds(r0, L), :]
        v = proj_s[pl.ds(r0, L), 4 * GW:5 * GW]
        vb = v.astype(BF16)
        g_last = G[L - 1:L, :]
        st = st_s[...]

        o = _dot_nt((q * jnp.exp(G)).astype(BF16), st.astype(BF16))

        scores = jnp.zeros((N_GROUP_HEADS * L, L), F32)
        for li, w in enumerate(widths):
            gref = G[w - 1:w, :]
            for pair in range(1, L // (2 * w)):
                lastrow = pair * 2 * w + w - 1
                gref = jnp.where(row_l < pair * 2 * w, gref, G[lastrow:lastrow + 1, :])
            qt = q * jnp.exp(jnp.minimum(G - gref, 0.0))
            kt = k * jnp.exp(jnp.minimum(gref - G, 0.0))
            q4 = jnp.concatenate([qt] * N_GROUP_HEADS, axis=0) * ones_f
            s4 = _dot_nt(q4.astype(BF16), kt.astype(BF16))
            scores = scores + s4 * lvl_ref[li]
        o4 = _dot(scores.astype(BF16), vb) * ones_f
        for hh in range(N_GROUP_HEADS):
            o = o + o4[hh * L:(hh + 1) * L, :]

        o_diag = []
        for b in range(L // DIAG):
            gs = G[b * DIAG:(b + 1) * DIAG, :]
            qs = q[b * DIAG:(b + 1) * DIAG, :]
            ks = k[b * DIAG:(b + 1) * DIAG, :]
            vs = v[b * DIAG:(b + 1) * DIAG, :]
            terms = []
            for j in range(DIAG):
                dg = jnp.where(row_d >= j, gs - gs[j:j + 1, :], -jnp.inf)
                terms.append((qs * (ks[j:j + 1, :] * jnp.exp(dg))).astype(BF16))
            e = jnp.concatenate(terms, axis=0)
            pd = _dot(e, ones_bd)
            od = jnp.zeros((DIAG, GW), F32)
            for j in range(DIAG):
                od = od + pd[j * DIAG:(j + 1) * DIAG, :] * vs[j:j + 1, :]
            o_diag.append(od)
        o = o + jnp.concatenate(o_diag, axis=0)
        o_s[pl.ds(r0, L), :] = o

        kd = (k * jnp.exp(g_last - G)).astype(BF16)
        st_s[...] = st * jnp.exp(g_last) + _dot_tn(vb, kd) * ones_f
        return carry

    for c in range(T // L):
        chunk_body(c, 0)

    o = o_s[...]
    ms = _dot((o * o).astype(BF16), ones_bd) * (1.0 / HEAD_DIM)
    o = o * lax.rsqrt(ms + HEAD_NORM_EPS) * hg_ref[...]
    y_s[:, GW:2 * GW] = (o * jax.nn.silu(sl(5))).astype(BF16)

    return x + _dot(y_s[...], w_out_ref[...])


def _mix_kernel(layer, h_ref, *refs):
    params = refs[:N_MIX_PARAMS]
    out_ref = refs[N_MIX_PARAMS]
    scratch = refs[N_MIX_PARAMS + 1:]
    _mix_reset(pl.program_id(1), scratch)
    out_ref[...] = _mix_tile(layer, h_ref[...], pl.program_id(1), params, scratch)


def _level_widths():
    widths = []
    w = DIAG
    while w < HGRN_CHUNK:
        widths.append(w)
        w *= 2
    return widths


def _level_masks():
    L = HGRN_CHUNK
    t = jnp.arange(N_GROUP_HEADS * L)[:, None] % L
    s = jnp.arange(L)[None, :]
    return jnp.stack([(((t // w) % 2 == 1) & (s // w == t // w - 1)).astype(F32)
                      for w in _level_widths()])


def _block_diag(w):
    return jax.scipy.linalg.block_diag(*[w[i] for i in range(w.shape[0])])


def _const_spec(shape):
    nd = len(shape)
    return pl.BlockSpec(shape, lambda *_: (0,) * nd)


def _mix_params(gmix, w_in, w_out, conv_w, conv_b, w_a, b_a, w_x, b_x, lam, lb_raw, norm_g,
                pool_w, pool_scale, sconv_w):
    GW = GROUP_WIDTH
    assert w_in.shape[1] == N_IN_SLICES * GW and POOL_WINDOWS == (2, 4, 8, 16)
    wg = jnp.concatenate([_block_diag(w_a), _block_diag(w_x)], axis=1).astype(BF16)
    bg = jnp.concatenate([b_a, b_x])[None, :]
    head_of = jnp.arange(GW) // HEAD_DIM
    ones_bd = (head_of[:, None] == head_of[None, :]).astype(BF16)
    tri = jnp.tril(jnp.ones((HGRN_CHUNK, HGRN_CHUNK), BF16))
    params = [gmix[None, :], w_in.astype(BF16), w_out.astype(BF16), conv_w, conv_b[None, :], wg, bg,
              lam[None, :], lb_raw, jnp.tile(norm_g, N_GROUP_HEADS)[None, :],
              _block_diag(pool_w).astype(BF16), pool_scale[None, :], sconv_w, ones_bd, tri,
              _level_masks()]
    assert len(params) == N_MIX_PARAMS
    return params


def _mix_scratch(T):
    GW = GROUP_WIDTH
    scratch = [
        pltpu.VMEM((T, N_IN_SLICES * GW), F32),
        pltpu.VMEM((HALO + T, GW), F32),
        pltpu.VMEM((HALO + T, GW), F32),
        pltpu.VMEM((HALO + T, GW), F32),
        pltpu.VMEM((V7X_SUBLANES + T, GW), F32),
        pltpu.VMEM((V7X_SUBLANES + T, GW), F32),
        pltpu.VMEM((2 * V7X_SUBLANES + T, GW), F32),
        pltpu.VMEM((2 * V7X_SUBLANES + T, GW), F32),
        pltpu.VMEM((2 * HALO + T, GW), F32),
        pltpu.VMEM((V7X_SUBLANES, GW), F32),
        pltpu.VMEM((GW, GW), F32),
        pltpu.VMEM((T, GW), F32),
        pltpu.VMEM((T, GW), F32),
        pltpu.VMEM((T, GW), F32),
        pltpu.VMEM((T, GW), F32),
        pltpu.VMEM((T, N_GROUP_HEADS * GW), BF16),
    ]
    assert len(scratch) == N_MIX_SCRATCH
    return scratch


def _mix_layer(layer, h, params):
    B, S, D = h.shape
    T = min(MIX_TILE, S)
    assert S % T == 0 and T % HGRN_CHUNK == 0
    in_specs = [pl.BlockSpec((None, T, D), lambda b, t: (b, t, 0))]
    in_specs += [_const_spec(a.shape) for a in params]
    return pl.pallas_call(
        functools.partial(_mix_kernel, layer),
        out_shape=jax.ShapeDtypeStruct((B, S, D), F32),
        grid=(B, S // T),
        in_specs=in_specs,
        out_specs=pl.BlockSpec((None, T, D), lambda b, t: (b, t, 0)),
        scratch_shapes=_mix_scratch(T),
        compiler_params=pltpu.CompilerParams(
            dimension_semantics=("arbitrary", "arbitrary"),
            vmem_limit_bytes=48 * 1024 * 1024),
        name=f"mix{layer}",
    )(h, *params)


def _ffn_tile(x, g_ref, wg_ref, wu_ref, wd_ref, act_s):
    hn = _rms_norm(x, g_ref[...]).astype(BF16)
    ff = wg_ref.shape[1]
    for c in range(0, ff, FFN_CHUNK):
        gate = _dot(hn, wg_ref[:, c:c + FFN_CHUNK])
        up = _dot(hn, wu_ref[:, c:c + FFN_CHUNK])
        act_s[:, c:c + FFN_CHUNK] = (jax.nn.silu(gate) * up).astype(BF16)
    return x + _dot(act_s[...], wd_ref[...])


def _ffn_kernel(final, h_ref, g_ref, gf_ref, wg_ref, wu_ref, wd_ref, out_ref, act_s):
    y = _ffn_tile(h_ref[...], g_ref, wg_ref, wu_ref, wd_ref, act_s)
    out_ref[...] = _rms_norm(y, gf_ref[...]) if final else y


def _ffn_layer(h2, g, w_gate, w_up, w_down, g_final, final):
    n, d = h2.shape
    ff = w_gate.shape[1]
    tm = min(FFN_TILE, n)
    assert n % tm == 0 and ff % FFN_CHUNK == 0
    once = pl.Buffered(1)
    return pl.pallas_call(
        functools.partial(_ffn_kernel, final),
        out_shape=jax.ShapeDtypeStruct((n, d), F32),
        grid=(n // tm,),
        in_specs=[pl.BlockSpec((tm, d), lambda i: (i, 0)),
                  _const_spec((1, d)),
                  _const_spec((1, d)),
                  pl.BlockSpec((d, ff), lambda i: (0, 0), pipeline_mode=once),
                  pl.BlockSpec((d, ff), lambda i: (0, 0), pipeline_mode=once),
                  pl.BlockSpec((ff, d), lambda i: (0, 0), pipeline_mode=once)],
        out_specs=pl.BlockSpec((tm, d), lambda i: (i, 0)),
        scratch_shapes=[pltpu.VMEM((tm, ff), BF16)],
        compiler_params=pltpu.CompilerParams(
            dimension_semantics=("arbitrary",),
            vmem_limit_bytes=48 * 1024 * 1024),
        name="ffn",
    )(h2, g[None, :], g_final[None, :], w_gate.astype(BF16), w_up.astype(BF16), w_down.astype(BF16))


def _router_kernel(h_ref, g_ref, wr_ref, triu_ref, hn_ref, mi_ref, mf_ref, cnt_ref):
    n_e = wr_ref.shape[0]
    tr = h_ref.shape[0]
    hn = _rms_norm(h_ref[...], g_ref[...])
    hn_ref[...] = hn.astype(BF16)
    xh, xm, xl = _split3(hn)
    wh, wm, wl = _split3(wr_ref[...])
    logits = (_dot_nt(wh, xh) + _dot_nt(wh, xm) + _dot_nt(wm, xh)
              + _dot_nt(wh, xl) + _dot_nt(wl, xh) + _dot_nt(wm, xm))
    eid = lax.broadcasted_iota(I32, (n_e, tr), 0)
    m1 = jnp.max(logits, axis=0, keepdims=True)
    e1 = jnp.min(jnp.where(logits == m1, eid, n_e), axis=0, keepdims=True)
    rest = jnp.where(eid == e1, -jnp.inf, logits)
    m2 = jnp.max(rest, axis=0, keepdims=True)
    e2 = jnp.min(jnp.where(rest == m2, eid, n_e), axis=0, keepdims=True)
    ex = jnp.exp(m2 - m1)
    g1 = 1.0 / (1.0 + ex)
    g2 = ex / (1.0 + ex)
    member = jnp.where((eid == e1) | (eid == e2), 1.0, 0.0)
    incl = _dot(member.astype(BF16), triu_ref[...])
    rank = incl - member
    r1 = jnp.sum(jnp.where(eid == e1, rank, 0.0), axis=0, keepdims=True)
    r2 = jnp.sum(jnp.where(eid == e2, rank, 0.0), axis=0, keepdims=True)
    cnt_ref[...] = jnp.broadcast_to(incl[:, tr - 1:tr], cnt_ref.shape).astype(I32)
    zi = jnp.zeros((1, tr), I32)
    mi_ref[...] = jnp.concatenate(
        [e1, e2, r1.astype(I32), r2.astype(I32), zi, zi, zi, zi], axis=0)
    zf = jnp.zeros((1, tr), F32)
    mf_ref[...] = jnp.concatenate([g1, g2, zf, zf, zf, zf, zf, zf], axis=0)


def _router(h2, g, w_router):
    n, d = h2.shape
    n_e = w_router.shape[1]
    tr = min(MOE_TILE, n)
    assert n % tr == 0 and n_e == V7X_SUBLANES
    tok = jnp.arange(tr)
    triu = (tok[:, None] <= tok[None, :]).astype(BF16)
    return pl.pallas_call(
        _router_kernel,
        out_shape=(jax.ShapeDtypeStruct((n, d), BF16),
                   jax.ShapeDtypeStruct((V7X_SUBLANES, n), I32),
                   jax.ShapeDtypeStruct((V7X_SUBLANES, n), F32),
                   jax.ShapeDtypeStruct((n // tr, n_e, V7X_LANES), I32)),
        grid=(n // tr,),
        in_specs=[pl.BlockSpec((tr, d), lambda i: (i, 0)),
                  _const_spec((1, d)),
                  _const_spec((n_e, d)),
                  _const_spec((tr, tr))],
        out_specs=(pl.BlockSpec((tr, d), lambda i: (i, 0)),
                   pl.BlockSpec((V7X_SUBLANES, tr), lambda i: (0, i)),
                   pl.BlockSpec((V7X_SUBLANES, tr), lambda i: (0, i)),
                   pl.BlockSpec((None, n_e, V7X_LANES), lambda i: (i, 0, 0))),
        compiler_params=pltpu.CompilerParams(dimension_semantics=("arbitrary",)),
        name="router",
    )(h2, g[None, :], w_router.T, triu)


def _segment_copies(step, slot, tables, n_e, tile, local_ref, rows_ref, sem, to_rows, start):
    off_ref, row_ref, pad_ref = tables
    for e in range(n_e):
        off = off_ref[step * n_e + e]
        row = row_ref[step * n_e + e]
        pad = pad_ref[step * n_e + e]
        size = tile
        while size >= ROW_PAD:
            done = pad & (-2 * size)
            local = local_ref.at[slot, pl.ds(pl.multiple_of(off + done, ROW_PAD), size)]
            remote = rows_ref.at[pl.ds(pl.multiple_of(row + done, ROW_PAD), size)]
            cp = (pltpu.make_async_copy(local, remote, sem.at[slot]) if to_rows
                  else pltpu.make_async_copy(remote, local, sem.at[slot]))

            @pl.when((pad & size) != 0)
            def _():
                if start:
                    cp.start()
                else:
                    cp.wait()

            size //= 2


def _stack_rows(mi, off_ref, step, n_e):
    e1, e2, row1, row2 = mi[0:1, :], mi[1:2, :], mi[2:3, :], mi[3:4, :]
    for e in range(n_e):
        off = off_ref[step * n_e + e]
        row1 = row1 + jnp.where(e1 == e, off, 0)
        row2 = row2 + jnp.where(e2 == e, off, 0)
    return row1, row2


def _fill_copies(fill_ref, n_e, zero_s, xs_ref, sem, start):
    half = zero_s.shape[0]
    rb = 2 * half

    def go(cp):
        if start:
            cp.start()
        else:
            cp.wait()

    for e in range(n_e):
        row = fill_ref[e]
        gap = fill_ref[n_e + e]
        size = half
        while size >= ROW_PAD:
            done = gap & (-2 * size)
            cp = pltpu.make_async_copy(
                zero_s.at[pl.ds(0, size)],
                xs_ref.at[pl.ds(pl.multiple_of(row + done, ROW_PAD), size)], sem)
            pl.when((gap & size) != 0)(functools.partial(go, cp))
            size //= 2

    def block(b, c):
        for part in range(2):
            go(pltpu.make_async_copy(
                zero_s, xs_ref.at[pl.ds(pl.multiple_of(b * rb + part * half, half), half)], sem))
        return c

    lax.fori_loop(fill_ref[2 * n_e], xs_ref.shape[0] // rb, block, 0)


def _dispatch_kernel(nt, off_ref, row_ref, pad_ref, fill_ref, hn_ref, mi_ref, xs_ref, slab_s, zero_s,
                     sem, fill_sem):
    i = pl.program_id(0)
    tile = hn_ref.shape[0]
    n_e = V7X_SUBLANES
    r_stack = slab_s.shape[1]
    slot = lax.rem(i, 2)
    copies = functools.partial(_segment_copies, tables=(off_ref, row_ref, pad_ref), n_e=n_e,
                               tile=tile, local_ref=slab_s, rows_ref=xs_ref, sem=sem, to_rows=True)

    @pl.when(i == 0)
    def _():
        zero_s[...] = jnp.zeros(zero_s.shape, F32)
        _fill_copies(fill_ref, n_e, zero_s, xs_ref, fill_sem, start=True)

    @pl.when(i >= 2)
    def _():
        copies(i - 2, slot, start=False)

    row1, row2 = _stack_rows(mi_ref[...], off_ref, i, n_e)
    rid = lax.broadcasted_iota(I32, (r_stack, tile), 0)
    sel = jnp.where(rid == row1, 1.0, jnp.where(rid == row2, 1.0, 0.0)).astype(BF16)
    slab_s[slot] = _dot(sel, hn_ref[...])
    copies(i, slot, start=True)

    @pl.when(i == nt - 1)
    def _():
        if nt > 1:
            copies(i - 1, 1 - slot, start=False)
        copies(i, slot, start=False)
        _fill_copies(fill_ref, n_e, zero_s, xs_ref, fill_sem, start=False)


def _dispatch(hn, mi, tables, fill, n_rows):
    n, d = hn.shape
    tile = min(MOE_TILE, n)
    nt = n // tile
    r_stack = TOP_K * tile + V7X_SUBLANES * ROW_PAD
    return pl.pallas_call(
        functools.partial(_dispatch_kernel, nt),
        out_shape=jax.ShapeDtypeStruct((n_rows, d), F32),
        grid_spec=pltpu.PrefetchScalarGridSpec(
            num_scalar_prefetch=4,
            grid=(nt,),
            in_specs=[pl.BlockSpec((tile, d), lambda i, *_: (i, 0)),
                      pl.BlockSpec((V7X_SUBLANES, tile), lambda i, *_: (0, i))],
            out_specs=pl.BlockSpec(memory_space=pl.ANY),
            scratch_shapes=[pltpu.VMEM((2, r_stack, d), F32),
                            pltpu.VMEM((EXPERT_ROWS // 2, d), F32),
                            pltpu.SemaphoreType.DMA((2,)), pltpu.SemaphoreType.DMA(())]),
        compiler_params=pltpu.CompilerParams(dimension_semantics=("arbitrary",),
                                             has_side_effects=True),
        name="dispatch",
    )(*tables, fill, hn, mi)


def _expert_kernel(be_ref, rows_ref, src_ref, x_ref, w1_ref, w3_ref, w2_ref, out_ref,
                   w1_s, w3_s, w2_s):
    del be_ref, src_ref
    b = pl.program_id(0)
    f = pl.program_id(1)
    rows = rows_ref[b]

    @pl.when(f == 0)
    def _():
        out_ref[...] = jnp.zeros(out_ref.shape, F32)

    def cast_weights():
        w1_s[...] = w1_ref[...].astype(BF16)
        w3_s[...] = w3_ref[...].astype(BF16)
        w2_s[...] = w2_ref[...].astype(BF16)

    def swiglu_rows(r0, n_rows):
        xb = x_ref[pl.ds(r0, n_rows), :].astype(BF16)
        gate = _dot(xb, w1_s[...])
        up = _dot(xb, w3_s[...])
        act = (jax.nn.silu(gate) * up).astype(BF16)
        out_ref[pl.ds(r0, n_rows), :] += _dot(act, w2_s[...])

    @pl.when(rows == EXPERT_ROWS)
    def _():
        cast_weights()
        for s in range(EXPERT_ROWS // EXPERT_FULL):
            swiglu_rows(s * EXPERT_FULL, EXPERT_FULL)

    @pl.when((rows > 0) & (rows < EXPERT_ROWS))
    def _():
        cast_weights()

        n_sub = (rows + EXPERT_SUB - 1) // EXPERT_SUB

        def pair(s, c):
            swiglu_rows(pl.multiple_of(s * EXPERT_FULL, EXPERT_FULL), EXPERT_FULL)
            return c

        lax.fori_loop(0, n_sub // 2, pair, 0)

        @pl.when(n_sub % 2 == 1)
        def _():
            swiglu_rows(pl.multiple_of((n_sub - 1) * EXPERT_SUB, EXPERT_SUB), EXPERT_SUB)


def _experts(xs, blk_e, blk_rows, blk_src, w1, w3, w2):
    n_rows, d = xs.shape
    n_e, _, ff = w1.shape
    rb = EXPERT_ROWS
    fft = min(EXPERT_FF, ff)
    assert n_rows % rb == 0 and ff % fft == 0
    assert EXPERT_FULL == 2 * EXPERT_SUB and rb % EXPERT_FULL == 0
    nf = ff // fft
    nb = n_rows // rb

    def f_eff(b, f, rows):
        return jnp.where(rows[b] > 0, f, nf - 1)

    return pl.pallas_call(
        _expert_kernel,
        out_shape=jax.ShapeDtypeStruct((n_rows, d), F32),
        grid_spec=pltpu.PrefetchScalarGridSpec(
            num_scalar_prefetch=3,
            grid=(nb, nf),
            in_specs=[
                pl.BlockSpec((rb, d), lambda b, f, be, rows, src: (src[b], 0)),
                pl.BlockSpec((None, d, fft), lambda b, f, be, rows, src: (be[b], 0, f_eff(b, f, rows))),
                pl.BlockSpec((None, d, fft), lambda b, f, be, rows, src: (be[b], 0, f_eff(b, f, rows))),
                pl.BlockSpec((None, fft, d), lambda b, f, be, rows, src: (be[b], f_eff(b, f, rows), 0)),
            ],
            out_specs=pl.BlockSpec((rb, d), lambda b, f, be, rows, src: (b, 0)),
            scratch_shapes=[pltpu.VMEM((d, fft), BF16), pltpu.VMEM((d, fft), BF16),
                            pltpu.VMEM((fft, d), BF16)]),
        compiler_params=pltpu.CompilerParams(
            dimension_semantics=("arbitrary", "arbitrary"),
            vmem_limit_bytes=EXPERT_VMEM_BYTES),
        name="experts",
    )(blk_e, blk_rows, blk_src, xs, w1, w3, w2)


def _combine_kernel(final, nt, off_ref, row_ref, pad_ref, h_ref, mi_ref, gt_ref, g_ref, ys_ref,
                    out_ref, stack_s, sem):
    i = pl.program_id(0)
    tile = h_ref.shape[0]
    n_e = V7X_SUBLANES
    r_stack = stack_s.shape[1]
    slot = lax.rem(i, 2)
    copies = functools.partial(_segment_copies, tables=(off_ref, row_ref, pad_ref), n_e=n_e,
                               tile=tile, local_ref=stack_s, rows_ref=ys_ref, sem=sem, to_rows=False)

    @pl.when(i == 0)
    def _():
        stack_s[...] = jnp.zeros(stack_s.shape, F32)
        copies(i, slot, start=True)

    @pl.when(i + 1 < nt)
    def _():
        copies(i + 1, 1 - slot, start=True)

    copies(i, slot, start=False)
    y_rows = stack_s[slot].astype(BF16)
    row1, row2 = _stack_rows(mi_ref[...], off_ref, i, n_e)
    rid = lax.broadcasted_iota(I32, (r_stack, tile), 0)
    y1 = _dot_tn(jnp.where(rid == row1, 1.0, 0.0).astype(BF16), y_rows)
    y2 = _dot_tn(jnp.where(rid == row2, 1.0, 0.0).astype(BF16), y_rows)
    gates = gt_ref[...]
    y = h_ref[...] + y1 * gates[:, 0:1] + y2 * gates[:, 1:2]
    out_ref[...] = _rms_norm(y, g_ref[...]) if final else y


def _combine(h2, mi, gates, tables, ys, g_final, final):
    n, d = h2.shape
    tile = min(MOE_TILE, n)
    nt = n // tile
    r_stack = TOP_K * tile + V7X_SUBLANES * ROW_PAD
    return pl.pallas_call(
        functools.partial(_combine_kernel, final, nt),
        out_shape=jax.ShapeDtypeStruct((n, d), F32),
        grid_spec=pltpu.PrefetchScalarGridSpec(
            num_scalar_prefetch=3,
            grid=(nt,),
            in_specs=[pl.BlockSpec((tile, d), lambda i, *_: (i, 0)),
                      pl.BlockSpec((V7X_SUBLANES, tile), lambda i, *_: (0, i)),
                      pl.BlockSpec((tile, TOP_K), lambda i, *_: (i, 0)),
                      pl.BlockSpec((1, d), lambda i, *_: (0, 0)),
                      pl.BlockSpec(memory_space=pl.ANY)],
            out_specs=pl.BlockSpec((tile, d), lambda i, *_: (i, 0)),
            scratch_shapes=[pltpu.VMEM((2, r_stack, d), F32), pltpu.SemaphoreType.DMA((2,))]),
        compiler_params=pltpu.CompilerParams(dimension_semantics=("arbitrary",)),
        name="combine",
    )(*tables, h2, mi, gates, g_final[None, :], ys)


def _moe_layer(h2, g, w_router, w1, w3, w2, g_final, final=True):
    n, d = h2.shape
    n_e = w_router.shape[1]
    rb = EXPERT_ROWS
    tile = min(MOE_TILE, n)
    nt = n // tile
    hn, mi, mf, cnt = _router(h2, g, w_router)
    pad = (cnt[:, :, 0] + ROW_PAD - 1) // ROW_PAD * ROW_PAD
    off = jnp.cumsum(pad, axis=1) - pad
    total = jnp.sum(pad, axis=0)
    padded = (total + rb - 1) // rb * rb
    pend = jnp.cumsum(padded)
    pstart = pend - padded
    row = pstart[None, :] + jnp.cumsum(pad, axis=0) - pad
    tables = tuple(t.reshape(-1).astype(I32) for t in (off, row, pad))
    nb = (n * TOP_K + nt * n_e * (ROW_PAD - 1)) // rb + n_e
    blk = jnp.arange(nb, dtype=I32)
    n_used = pend[-1] // rb
    last = jnp.maximum(n_used - 1, 0)
    blk_src = jnp.minimum(blk, last)
    blk_e = jnp.minimum(jnp.sum(blk_src[:, None] * rb >= pend[None, :], axis=1), n_e - 1).astype(I32)
    blk_rows = jnp.where(blk < n_used,
                         jnp.clip(total[blk_e] - (blk * rb - pstart[blk_e]), 0, rb), 0).astype(I32)
    fill = jnp.concatenate([pstart + total, padded - total, n_used[None]]).astype(I32)
    xs = _dispatch(hn, mi, tables, fill, nb * rb)
    ys = _experts(xs, blk_e, blk_rows, blk_src, w1, w3, w2)
    return _combine(h2, mi, mf[:TOP_K].T, tables, ys, g_final, final)


def kernel(x, w_in, w_out, lru_conv_w, lru_conv_b, lru_w_a, lru_b_a, lru_w_x, lru_b_x, lru_lambda,
           hgrn_lower_bounds, hgrn_norm_g, pool_w, pool_scale, sconv_w, norm_mix_g, norm_ffn_g,
           ffn_w_gate, ffn_w_up, ffn_w_down, moe_w_router, moe_w1, moe_w3, moe_w2, final_norm_g):
    B, S, D = x.shape
    depth = w_in.shape[0]

    def mix_params(layer):
        return _mix_params(norm_mix_g[layer], w_in[layer], w_out[layer], lru_conv_w[layer],
                           lru_conv_b[layer], lru_w_a[layer], lru_b_a[layer], lru_w_x[layer],
                           lru_b_x[layer], lru_lambda[layer], hgrn_lower_bounds, hgrn_norm_g[layer],
                           pool_w[layer], pool_scale[layer], sconv_w[layer])

    def ffn_layer(h2, layer):
        j = layer // 2
        final = layer == depth - 1
        if layer % 2 == 0:
            return _ffn_layer(h2, norm_ffn_g[layer], ffn_w_gate[j], ffn_w_up[j], ffn_w_down[j],
                              final_norm_g, final)
        return _moe_layer(h2, norm_ffn_g[layer], moe_w_router[j], moe_w1[j], moe_w3[j], moe_w2[j],
                          final_norm_g, final)

    h = x
    for layer in range(depth):
        h = _mix_layer(layer, h, mix_params(layer))
        h = ffn_layer(h.reshape(B * S, D), layer).reshape(B, S, D)
    return h
```

```python
import functools

import jax
import jax.numpy as jnp
from jax import lax
from jax.experimental import pallas as pl
from jax.experimental.pallas import tpu as pltpu

F32 = jnp.float32
BF16 = jnp.bfloat16
I32 = jnp.int32

GROUP_WIDTH = 256
N_GROUP_HEADS = 4
HEAD_DIM = 64
N_IN_SLICES = 10
LRU_CONV = 4
LRU_C = 8.0
HGRN_CHUNK = 64
POOL_WINDOWS = (2, 4, 8, 16)
SCONV_WIDTH = 3
TOP_K = 2
NORM_EPS = 1e-6
HEAD_NORM_EPS = 1e-5

V7X_LANES = 128
V7X_SUBLANES = 8
V7X_VMEM_BYTES = 64 * 1024 * 1024

MIX_TILE = 512
HALO = 16
DIAG = 8
FFN_TILE = 512
FFN_CHUNK = 256
MOE_TILE = 256
ROW_PAD = V7X_SUBLANES
EXPERT_ROWS = 2048
EXPERT_VMEM_BYTES = 58 * 1024 * 1024
EXPERT_SUB = 256
EXPERT_FULL = 512
EXPERT_FF = 512


def _dot(a, b):
    return jnp.dot(a, b, preferred_element_type=F32)


def _dot_nt(a, b):
    return lax.dot_general(a, b, (((1,), (1,)), ((), ())), preferred_element_type=F32)


def _dot_tn(a, b):
    return lax.dot_general(a, b, (((0,), (0,)), ((), ())), preferred_element_type=F32)


def _rms_norm(x, g):
    ms = jnp.mean(x * x, axis=-1, keepdims=True)
    return x * lax.rsqrt(ms + NORM_EPS) * g


def _split3(x):
    hi = x.astype(BF16)
    r1 = x - hi.astype(F32)
    mid = r1.astype(BF16)
    lo = (r1 - mid.astype(F32)).astype(BF16)
    return hi, mid, lo


N_MIX_PARAMS = 16
N_MIX_SCRATCH = 16


def _mix_reset(ti, scratch):
    (proj_s, ax_s, p_s, sc_s, sa_s, su_s, ta_s, tu_s, tp_s, lruh_s, st_s, q_s, k_s, lf_s, o_s,
     y_s) = scratch
    GW = GROUP_WIDTH

    @pl.when(ti == 0)
    def _():
        ax_s[0:HALO, :] = jnp.zeros((HALO, GW), F32)
        p_s[0:HALO, :] = jnp.zeros((HALO, GW), F32)
        sc_s[0:HALO, :] = jnp.zeros((HALO, GW), F32)
        sa_s[0:V7X_SUBLANES, :] = jnp.ones((V7X_SUBLANES, GW), F32)
        su_s[0:V7X_SUBLANES, :] = jnp.zeros((V7X_SUBLANES, GW), F32)
        ta_s[0:V7X_SUBLANES, :] = jnp.zeros((V7X_SUBLANES, GW), F32)
        tu_s[0:V7X_SUBLANES, :] = jnp.zeros((V7X_SUBLANES, GW), F32)
        tp_s[0:HALO, :] = jnp.zeros((HALO, GW), F32)
        lruh_s[...] = jnp.zeros(lruh_s.shape, F32)
        st_s[...] = jnp.zeros(st_s.shape, F32)


def _mix_tile(layer, x, ti, params, scratch):
    (gmix_ref, w_in_ref, w_out_ref, cw_ref, cb_ref, wg_ref, bg_ref, lam_ref, lbraw_ref, hg_ref,
     wp_ref, ps_ref, sw_ref, ones_ref, tri_ref, lvl_ref) = params
    (proj_s, ax_s, p_s, sc_s, sa_s, su_s, ta_s, tu_s, tp_s, lruh_s, st_s, q_s, k_s, lf_s, o_s,
     y_s) = scratch
    T = x.shape[0]
    GW = GROUP_WIDTH
    hn = _rms_norm(x, gmix_ref[...]).astype(BF16)
    proj_s[...] = _dot(hn, w_in_ref[...])

    def sl(i):
        return proj_s[:, i * GW:(i + 1) * GW]

    ax_s[HALO:HALO + T, :] = sl(0)
    a_in = cb_ref[...]
    for kk in range(LRU_CONV):
        off = HALO - (LRU_CONV - 1) + kk
        a_in = a_in + cw_ref[kk:kk + 1, :] * ax_s[off:off + T, :]
    gates = _dot(a_in.astype(BF16), wg_ref[...]) + bg_ref[...]
    r_gate = jax.nn.sigmoid(gates[:, :GW])
    i_gate = jax.nn.sigmoid(gates[:, GW:])
    log_a = (-LRU_C) * r_gate * jax.nn.softplus(-lam_ref[...])
    a = jnp.exp(log_a)
    mult = jnp.sqrt(1.0 - a * a)
    u = mult * i_gate * a_in
    SUB = V7X_SUBLANES
    sa_s[SUB:SUB + T, :] = a
    su_s[SUB:SUB + T, :] = u
    a = sa_s[...]
    u = su_s[...]
    d = 1
    while d < SUB:
        ta_s[SUB:, :] = a
        tu_s[SUB:, :] = u
        a_sh = ta_s[SUB - d:SUB - d + SUB + T, :]
        u_sh = tu_s[SUB - d:SUB - d + SUB + T, :]
        u = a * u_sh + u
        a = a * a_sh
        d *= 2
    sa_s[0:SUB, :] = sa_s[T:T + SUB, :]
    su_s[0:SUB, :] = su_s[T:T + SUB, :]
    h_grp = lruh_s[...]
    groups = []
    for gi in range(1, T // SUB + 1):
        h_grp = a[gi * SUB:(gi + 1) * SUB, :] * h_grp + u[gi * SUB:(gi + 1) * SUB, :]
        groups.append(h_grp)
    lruh_s[...] = h_grp
    h_lru = jnp.concatenate(groups, axis=0)
    y_a = h_lru * jax.nn.gelu(sl(1))
    y_s[:, 0:GW] = y_a.astype(BF16)
    ax_s[0:HALO, :] = ax_s[T:T + HALO, :]

    p = sl(6)
    p_s[HALO:HALO + T, :] = p
    lane = lax.broadcasted_iota(I32, (1, GW), 1)
    win = jnp.where(lane < HEAD_DIM, float(POOL_WINDOWS[0]),
                    jnp.where(lane < 2 * HEAD_DIM, float(POOL_WINDOWS[1]),
                              jnp.where(lane < 3 * HEAD_DIM, float(POOL_WINDOWS[2]),
                                        float(POOL_WINDOWS[3]))))
    sums = []
    acc = p_s[...]
    d = 1
    while d < max(POOL_WINDOWS):
        tp_s[HALO:, :] = acc
        acc = acc + tp_s[HALO - d:HALO - d + HALO + T, :]
        sums.append(acc[HALO:, :])
        d *= 2
    wsum = jnp.where(lane < HEAD_DIM, sums[0],
                     jnp.where(lane < 2 * HEAD_DIM, sums[1],
                               jnp.where(lane < 3 * HEAD_DIM, sums[2], sums[3])))
    pos =(ti * T + 1 + lax.broadcasted_iota(I32, (T, 1), 0)).astype(F32)
    dpool = wsum / jnp.minimum(pos, win) - p
    y_c = _dot(dpool.astype(BF16), wp_ref[...]) * ps_ref[...]
    y_s[:, 2 * GW:3 * GW] = y_c.astype(BF16)
    p_s[0:HALO, :] = p_s[T:T + HALO, :]

    sc_s[HALO:HALO + T, :] = sl(8) * sl(9)
    conv = jnp.zeros((T, GW), F32)
    for kk in range(SCONV_WIDTH):
        off = HALO - (SCONV_WIDTH - 1) + kk
        conv = conv + sw_ref[kk:kk + 1, :] * sc_s[off:off + T, :]
    y_s[:, 3 * GW:4 * GW] = (sl(7) * conv).astype(BF16)
    sc_s[0:HALO, :] = sc_s[T:T + HALO, :]

    lbraw = lbraw_ref[...]
    e_lb = jnp.exp(lbraw - jnp.max(lbraw, axis=0, keepdims=True))
    p_lb = e_lb / jnp.sum(e_lb, axis=0, keepdims=True)
    lb = jnp.zeros((1, GW), F32)
    for li in range(1, layer + 1):
        lb = lb + p_lb[li:li + 1, :]
    z = sl(3)
    lf_s[...] = jnp.log(lb + (1.0 - lb) * jax.nn.sigmoid(z))
    k_s[...] = (1.0 - lb) * jax.nn.sigmoid(-z)
    q_s[...] = jax.nn.silu(sl(2))

    L = HGRN_CHUNK
    ones_bd = ones_ref[...]
    ones_f = ones_bd.astype(F32)
    tri = tri_ref[...]
    widths = _level_widths()
    row_l = lax.broadcasted_iota(I32, (L, 1), 0)
    row_d = lax.broadcasted_iota(I32, (DIAG, 1), 0)

    def chunk_body(c, carry):
        r0 = c * L
        lf = lf_s[pl.ds(r0, L), :]
        hi, mid, lo = _split3(lf)
        G = _dot(tri, hi) + _dot(tri, mid) + _dot(tri, lo)
        q = q_s[pl.ds(r0, L), :]
        k = k_s[pl.ds(r0, L), :]
        v = proj_s[pl.ds(r0, L), 4 * GW:5 * GW]
        vb = v.astype(BF16)
        g_last = G[L - 1:L, :]
        st = st_s[...]

        o = _dot_nt((q * jnp.exp(G)).astype(BF16), st.astype(BF16))

        scores = jnp.zeros((N_GROUP_HEADS * L, L), F32)
        for li, w in enumerate(widths):
            gref = G[w - 1:w, :]
            for pair in range(1, L // (2 * w)):
                lastrow = pair * 2 * w + w - 1
                gref = jnp.where(row_l < pair * 2 * w, gref, G[lastrow:lastrow + 1, :])
            qt = q * jnp.exp(jnp.minimum(G - gref, 0.0))
            kt = k * jnp.exp(jnp.minimum(gref - G, 0.0))
            q4 = jnp.concatenate([qt] * N_GROUP_HEADS, axis=0) * ones_f
            s4 = _dot_nt(q4.astype(BF16), kt.astype(BF16))
            scores = scores + s4 * lvl_ref[li]
        o4 = _dot(scores.astype(BF16), vb) * ones_f
        for hh in range(N_GROUP_HEADS):
            o = o + o4[hh * L:(hh + 1) * L, :]

        o_diag = []
        for b in range(L // DIAG):
            gs = G[b * DIAG:(b + 1) * DIAG, :]
            qs = q[b * DIAG:(b + 1) * DIAG, :]
            ks = k[b * DIAG:(b + 1) * DIAG, :]
            vs = v[b * DIAG:(b + 1) * DIAG, :]
            terms = []
            for j in range(DIAG):
                dg = jnp.where(row_d >= j, gs - gs[j:j + 1, :], -jnp.inf)
                terms.append((qs * (ks[j:j + 1, :] * jnp.exp(dg))).astype(BF16))
            e = jnp.concatenate(terms, axis=0)
            pd = _dot(e, ones_bd)
            od = jnp.zeros((DIAG, GW), F32)
            for j in range(DIAG):
                od = od + pd[j * DIAG:(j + 1) * DIAG, :] * vs[j:j + 1, :]
            o_diag.append(od)
        o = o + jnp.concatenate(o_diag, axis=0)
        o_s[pl.ds(r0, L), :] = o

        kd = (k * jnp.exp(g_last - G)).astype(BF16)
        st_s[...] = st * jnp.exp(g_last) + _dot_tn(vb, kd) * ones_f
        return carry

    for c in range(T // L):
        chunk_body(c, 0)

    o = o_s[...]
    ms = _dot((o * o).astype(BF16), ones_bd) * (1.0 / HEAD_DIM)
    o = o * lax.rsqrt(ms + HEAD_NORM_EPS) * hg_ref[...]
    y_s[:, GW:2 * GW] = (o * jax.nn.silu(sl(5))).astype(BF16)

    return x + _dot(y_s[...], w_out_ref[...])


def _mix_kernel(layer, h_ref, *refs):
    params = refs[:N_MIX_PARAMS]
    out_ref = refs[N_MIX_PARAMS]
    scratch = refs[N_MIX_PARAMS + 1:]
    _mix_reset(pl.program_id(1), scratch)
    out_ref[...] = _mix_tile(layer, h_ref[...], pl.program_id(1), params, scratch)


def _level_widths():
    widths = []
    w = DIAG
    while w < HGRN_CHUNK:
        widths.append(w)
        w *= 2
    return widths


def _level_masks():
    L = HGRN_CHUNK
    t = jnp.arange(N_GROUP_HEADS * L)[:, None] % L
    s = jnp.arange(L)[None, :]
    return jnp.stack([(((t // w) % 2 == 1) & (s // w == t // w - 1)).astype(F32)
                      for w in _level_widths()])


def _block_diag(w):
    return jax.scipy.linalg.block_diag(*[w[i] for i in range(w.shape[0])])


def _const_spec(shape):
    nd = len(shape)
    return pl.BlockSpec(shape, lambda *_: (0,) * nd)


def _mix_params(gmix, w_in, w_out, conv_w, conv_b, w_a, b_a, w_x, b_x, lam, lb_raw, norm_g,
                pool_w, pool_scale, sconv_w):
    GW = GROUP_WIDTH
    assert w_in.shape[1] == N_IN_SLICES * GW and POOL_WINDOWS == (2, 4, 8, 16)
    wg = jnp.concatenate([_block_diag(w_a), _block_diag(w_x)], axis=1).astype(BF16)
    bg = jnp.concatenate([b_a, b_x])[None, :]
    head_of = jnp.arange(GW) // HEAD_DIM
    ones_bd = (head_of[:, None] == head_of[None, :]).astype(BF16)
    tri = jnp.tril(jnp.ones((HGRN_CHUNK, HGRN_CHUNK), BF16))
    params = [gmix[None, :], w_in.astype(BF16), w_out.astype(BF16), conv_w, conv_b[None, :], wg, bg,
              lam[None, :], lb_raw, jnp.tile(norm_g, N_GROUP_HEADS)[None, :],
              _block_diag(pool_w).astype(BF16), pool_scale[None, :], sconv_w, ones_bd, tri,
              _level_masks()]
    assert len(params) == N_MIX_PARAMS
    return params


def _mix_scratch(T):
    GW = GROUP_WIDTH
    scratch = [
        pltpu.VMEM((T, N_IN_SLICES * GW), F32),
        pltpu.VMEM((HALO + T, GW), F32),
        pltpu.VMEM((HALO + T, GW), F32),
        pltpu.VMEM((HALO + T, GW), F32),
        pltpu.VMEM((V7X_SUBLANES + T, GW), F32),
        pltpu.VMEM((V7X_SUBLANES + T, GW), F32),
        pltpu.VMEM((2 * V7X_SUBLANES + T, GW), F32),
        pltpu.VMEM((2 * V7X_SUBLANES + T, GW), F32),
        pltpu.VMEM((2 * HALO + T, GW), F32),
        pltpu.VMEM((V7X_SUBLANES, GW), F32),
        pltpu.VMEM((GW, GW), F32),
        pltpu.VMEM((T, GW), F32),
        pltpu.VMEM((T, GW), F32),
        pltpu.VMEM((T, GW), F32),
        pltpu.VMEM((T, GW), F32),
        pltpu.VMEM((T, N_GROUP_HEADS * GW), BF16),
    ]
    assert len(scratch) == N_MIX_SCRATCH
    return scratch


def _mix_layer(layer, h, params):
    B, S, D = h.shape
    T = min(MIX_TILE, S)
    assert S % T == 0 and T % HGRN_CHUNK == 0
    in_specs = [pl.BlockSpec((None, T, D), lambda b, t: (b, t, 0))]
    in_specs += [_const_spec(a.shape) for a in params]
    return pl.pallas_call(
        functools.partial(_mix_kernel, layer),
        out_shape=jax.ShapeDtypeStruct((B, S, D), F32),
        grid=(B, S // T),
        in_specs=in_specs,
        out_specs=pl.BlockSpec((None, T, D), lambda b, t: (b, t, 0)),
        scratch_shapes=_mix_scratch(T),
        compiler_params=pltpu.CompilerParams(
            dimension_semantics=("arbitrary", "arbitrary"),
            vmem_limit_bytes=48 * 1024 * 1024),
        name=f"mix{layer}",
    )(h, *params)


def _ffn_tile(x, g_ref, wg_ref, wu_ref, wd_ref, act_s):
    hn = _rms_norm(x, g_ref[...]).astype(BF16)
    ff = wg_ref.shape[1]
    for c in range(0, ff, FFN_CHUNK):
        gate = _dot(hn, wg_ref[:, c:c + FFN_CHUNK])
        up = _dot(hn, wu_ref[:, c:c + FFN_CHUNK])
        act_s[:, c:c + FFN_CHUNK] = (jax.nn.silu(gate) * up).astype(BF16)
    return x + _dot(act_s[...], wd_ref[...])


def _ffn_kernel(final, h_ref, g_ref, gf_ref, wg_ref, wu_ref, wd_ref, out_ref, act_s):
    y = _ffn_tile(h_ref[...], g_ref, wg_ref, wu_ref, wd_ref, act_s)
    out_ref[...] = _rms_norm(y, gf_ref[...]) if final else y


def _ffn_layer(h2, g, w_gate, w_up, w_down, g_final, final):
    n, d = h2.shape
    ff = w_gate.shape[1]
    tm = min(FFN_TILE, n)
    assert n % tm == 0 and ff % FFN_CHUNK == 0
    once = pl.Buffered(1)
    return pl.pallas_call(
        functools.partial(_ffn_kernel, final),
        out_shape=jax.ShapeDtypeStruct((n, d), F32),
        grid=(n // tm,),
        in_specs=[pl.BlockSpec((tm, d), lambda i: (i, 0)),
                  _const_spec((1, d)),
                  _const_spec((1, d)),
                  pl.BlockSpec((d, ff), lambda i: (0, 0), pipeline_mode=once),
                  pl.BlockSpec((d, ff), lambda i: (0, 0), pipeline_mode=once),
                  pl.BlockSpec((ff, d), lambda i: (0, 0), pipeline_mode=once)],
        out_specs=pl.BlockSpec((tm, d), lambda i: (i, 0)),
        scratch_shapes=[pltpu.VMEM((tm, ff), BF16)],
        compiler_params=pltpu.CompilerParams(
            dimension_semantics=("arbitrary",),
            vmem_limit_bytes=48 * 1024 * 1024),
        name="ffn",
    )(h2, g[None, :], g_final[None, :], w_gate.astype(BF16), w_up.astype(BF16), w_down.astype(BF16))


def _router_kernel(h_ref, g_ref, wr_ref, triu_ref, hn_ref, mi_ref, mf_ref, cnt_ref):
    n_e = wr_ref.shape[0]
    tr = h_ref.shape[0]
    hn = _rms_norm(h_ref[...], g_ref[...])
    xh = hn.astype(BF16)
    hn_ref[...] = xh
    xm = (hn - xh.astype(F32)).astype(BF16)
    w = wr_ref[...]
    wh = w.astype(BF16)
    wm = (w - wh.astype(F32)).astype(BF16)
    logits = _dot_nt(wh, xh) + _dot_nt(wh, xm) + _dot_nt(wm, xh)
    eid = lax.broadcasted_iota(I32, (n_e, tr), 0)
    m1 = jnp.max(logits, axis=0, keepdims=True)
    e1 = jnp.min(jnp.where(logits == m1, eid, n_e), axis=0, keepdims=True)
    rest = jnp.where(eid == e1, -jnp.inf, logits)
    m2 = jnp.max(rest, axis=0, keepdims=True)
    e2 = jnp.min(jnp.where(rest == m2, eid, n_e), axis=0, keepdims=True)
    ex = jnp.exp(m2 - m1)
    g1 = 1.0 / (1.0 + ex)
    g2 = ex / (1.0 + ex)
    member = jnp.where((eid == e1) | (eid == e2), 1.0, 0.0)
    incl = _dot(member.astype(BF16), triu_ref[...])
    rank = incl - member
    r1 = jnp.sum(jnp.where(eid == e1, rank, 0.0), axis=0, keepdims=True)
    r2 = jnp.sum(jnp.where(eid == e2, rank, 0.0), axis=0, keepdims=True)
    cnt_ref[...] = jnp.broadcast_to(incl[:, tr - 1:tr], cnt_ref.shape).astype(I32)
    zi = jnp.zeros((1, tr), I32)
    mi_ref[...] = jnp.concatenate(
        [e1, e2, r1.astype(I32), r2.astype(I32), zi, zi, zi, zi], axis=0)
    zf = jnp.zeros((1, tr), F32)
    mf_ref[...] = jnp.concatenate([g1, g2, zf, zf, zf, zf, zf, zf], axis=0)


def _router(h2, g, w_router):
    n, d = h2.shape
    n_e = w_router.shape[1]
    tr = min(MOE_TILE, n)
    assert n % tr == 0 and n_e == V7X_SUBLANES
    tok = jnp.arange(tr)
    triu = (tok[:, None] <= tok[None, :]).astype(BF16)
    return pl.pallas_call(
        _router_kernel,
        out_shape=(jax.ShapeDtypeStruct((n, d), BF16),
                   jax.ShapeDtypeStruct((V7X_SUBLANES, n), I32),
                   jax.ShapeDtypeStruct((V7X_SUBLANES, n), F32),
                   jax.ShapeDtypeStruct((n // tr, n_e, V7X_LANES), I32)),
        grid=(n // tr,),
        in_specs=[pl.BlockSpec((tr, d), lambda i: (i, 0)),
                  _const_spec((1, d)),
                  _const_spec((n_e, d)),
                  _const_spec((tr, tr))],
        out_specs=(pl.BlockSpec((tr, d), lambda i: (i, 0)),
                   pl.BlockSpec((V7X_SUBLANES, tr), lambda i: (0, i)),
                   pl.BlockSpec((V7X_SUBLANES, tr), lambda i: (0, i)),
                   pl.BlockSpec((None, n_e, V7X_LANES), lambda i: (i, 0, 0))),
        compiler_params=pltpu.CompilerParams(dimension_semantics=("arbitrary",)),
        name="router",
    )(h2, g[None, :], w_router.T, triu)


def _segment_copies(step, slot, tables, n_e, tile, local_ref, rows_ref, sem, to_rows, start):
    off_ref, row_ref, pad_ref = tables

    def copy(off, row, size):
        local = local_ref.at[slot, pl.ds(pl.multiple_of(off, ROW_PAD), size)]
        remote = rows_ref.at[pl.ds(pl.multiple_of(row, ROW_PAD), size)]
        return (pltpu.make_async_copy(local, remote, sem.at[slot]) if to_rows
                else pltpu.make_async_copy(remote, local, sem.at[slot]))

    if not start:
        last = step * n_e + n_e - 1
        total = off_ref[last] + pad_ref[last]
        size = TOP_K * tile
        while size >= ROW_PAD:
            pl.when((total & size) != 0)(copy(0, 0, size).wait)
            size //= 2
        return
    for e in range(n_e):
        off = off_ref[step * n_e + e]
        row = row_ref[step * n_e + e]
        pad = pad_ref[step * n_e + e]
        size = tile
        while size >= ROW_PAD:
            done = pad & (-2 * size)
            pl.when((pad & size) != 0)(copy(off + done, row + done, size).start)
            size //= 2


def _stack_rows(mi, off_ref, step, n_e):
    e1, e2, row1, row2 = mi[0:1, :], mi[1:2, :], mi[2:3, :], mi[3:4, :]
    for e in range(n_e):
        off = off_ref[step * n_e + e]
        row1 = row1 + jnp.where(e1 == e, off, 0)
        row2 = row2 + jnp.where(e2 == e, off, 0)
    return row1, row2


def _fill_copies(fill_ref, n_e, zero_s, xs_ref, sem, start):
    half = zero_s.shape[0]
    rb = 2 * half

    def go(cp):
        if start:
            cp.start()
        else:
            cp.wait()

    for e in range(n_e):
        row = fill_ref[e]
        gap = fill_ref[n_e + e]
        size = half
        while size >= ROW_PAD:
            done = gap & (-2 * size)
            cp = pltpu.make_async_copy(
                zero_s.at[pl.ds(0, size)],
                xs_ref.at[pl.ds(pl.multiple_of(row + done, ROW_PAD), size)], sem)
            pl.when((gap & size) != 0)(functools.partial(go, cp))
            size //= 2

    def block(b, c):
        for part in range(2):
            go(pltpu.make_async_copy(
                zero_s, xs_ref.at[pl.ds(pl.multiple_of(b * rb + part * half, half), half)], sem))
        return c

    lax.fori_loop(fill_ref[2 * n_e], xs_ref.shape[0] // rb, block, 0)


def _dispatch_kernel(nt, off_ref, row_ref, pad_ref, fill_ref, hn_ref, mi_ref, xs_ref, slab_s, zero_s,
                     sem, fill_sem):
    i = pl.program_id(0)
    tile = hn_ref.shape[0]
    n_e = V7X_SUBLANES
    r_stack = slab_s.shape[1]
    slot = lax.rem(i, 2)
    copies = functools.partial(_segment_copies, tables=(off_ref, row_ref, pad_ref), n_e=n_e,
                               tile=tile, local_ref=slab_s, rows_ref=xs_ref, sem=sem, to_rows=True)

    @pl.when(i == 0)
    def _():
        zero_s[...] = jnp.zeros(zero_s.shape, F32)
        _fill_copies(fill_ref, n_e, zero_s, xs_ref, fill_sem, start=True)

    @pl.when(i >= 2)
    def _():
        copies(i - 2, slot, start=False)

    row1, row2 = _stack_rows(mi_ref[...], off_ref, i, n_e)
    rid = lax.broadcasted_iota(I32, (r_stack, tile), 0)
    sel = jnp.where(rid == row1, 1.0, jnp.where(rid == row2, 1.0, 0.0)).astype(BF16)
    slab_s[slot] = _dot(sel, hn_ref[...])
    copies(i, slot, start=True)

    @pl.when(i == nt - 1)
    def _():
        if nt > 1:
            copies(i - 1, 1 - slot, start=False)
        copies(i, slot, start=False)
        _fill_copies(fill_ref, n_e, zero_s, xs_ref, fill_sem, start=False)


def _dispatch(hn, mi, tables, fill, n_rows):
    n, d = hn.shape
    tile = min(MOE_TILE, n)
    nt = n // tile
    r_stack = TOP_K * tile + V7X_SUBLANES * ROW_PAD
    return pl.pallas_call(
        functools.partial(_dispatch_kernel, nt),
        out_shape=jax.ShapeDtypeStruct((n_rows, d), F32),
        grid_spec=pltpu.PrefetchScalarGridSpec(
            num_scalar_prefetch=4,
            grid=(nt,),
            in_specs=[pl.BlockSpec((tile, d), lambda i, *_: (i, 0)),
                      pl.BlockSpec((V7X_SUBLANES, tile), lambda i, *_: (0, i))],
            out_specs=pl.BlockSpec(memory_space=pl.ANY),
            scratch_shapes=[pltpu.VMEM((2, r_stack, d), F32),
                            pltpu.VMEM((EXPERT_ROWS // 2, d), F32),
                            pltpu.SemaphoreType.DMA((2,)), pltpu.SemaphoreType.DMA(())]),
        compiler_params=pltpu.CompilerParams(dimension_semantics=("arbitrary",),
                                             has_side_effects=True),
        name="dispatch",
    )(*tables, fill, hn, mi)


def _expert_kernel(be_ref, rows_ref, src_ref, x_ref, w1_ref, w3_ref, w2_ref, out_ref,
                   w1_s, w3_s, w2_s):
    del be_ref, src_ref
    b = pl.program_id(0)
    f = pl.program_id(1)
    rows = rows_ref[b]

    @pl.when(f == 0)
    def _():
        out_ref[...] = jnp.zeros(out_ref.shape, F32)

    def cast_weights():
        w1_s[...] = w1_ref[...].astype(BF16)
        w3_s[...] = w3_ref[...].astype(BF16)
        w2_s[...] = w2_ref[...].astype(BF16)

    def swiglu_rows(r0, n_rows):
        xb = x_ref[pl.ds(r0, n_rows), :].astype(BF16)
        gate = _dot(xb, w1_s[...])
        up = _dot(xb, w3_s[...])
        act = (jax.nn.silu(gate) * up).astype(BF16)
        out_ref[pl.ds(r0, n_rows), :] += _dot(act, w2_s[...])

    @pl.when(rows == EXPERT_ROWS)
    def _():
        cast_weights()
        for s in range(EXPERT_ROWS // EXPERT_FULL):
            swiglu_rows(s * EXPERT_FULL, EXPERT_FULL)

    @pl.when((rows > 0) & (rows < EXPERT_ROWS))
    def _():
        cast_weights()

        n_sub = (rows + EXPERT_SUB - 1) // EXPERT_SUB

        def pair(s, c):
            swiglu_rows(pl.multiple_of(s * EXPERT_FULL, EXPERT_FULL), EXPERT_FULL)
            return c

        lax.fori_loop(0, n_sub // 2, pair, 0)

        @pl.when(n_sub % 2 == 1)
        def _():
            swiglu_rows(pl.multiple_of((n_sub - 1) * EXPERT_SUB, EXPERT_SUB), EXPERT_SUB)


def _experts(xs, blk_e, blk_rows, blk_src, w1, w3, w2):
    n_rows, d = xs.shape
    n_e, _, ff = w1.shape
    rb = EXPERT_ROWS
    fft = min(EXPERT_FF, ff)
    assert n_rows % rb == 0 and ff % fft == 0
    assert EXPERT_FULL == 2 * EXPERT_SUB and rb % EXPERT_FULL == 0
    nf = ff // fft
    nb = n_rows // rb

    def f_eff(b, f, rows):
        return jnp.where(rows[b] > 0, f, nf - 1)

    return pl.pallas_call(
        _expert_kernel,
        out_shape=jax.ShapeDtypeStruct((n_rows, d), F32),
        grid_spec=pltpu.PrefetchScalarGridSpec(
            num_scalar_prefetch=3,
            grid=(nb, nf),
            in_specs=[
                pl.BlockSpec((rb, d), lambda b, f, be, rows, src: (src[b], 0)),
                pl.BlockSpec((None, d, fft), lambda b, f, be, rows, src: (be[b], 0, f_eff(b, f, rows))),
                pl.BlockSpec((None, d, fft), lambda b, f, be, rows, src: (be[b], 0, f_eff(b, f, rows))),
                pl.BlockSpec((None, fft, d), lambda b, f, be, rows, src: (be[b], f_eff(b, f, rows), 0)),
            ],
            out_specs=pl.BlockSpec((rb, d), lambda b, f, be, rows, src: (b, 0)),
            scratch_shapes=[pltpu.VMEM((d, fft), BF16), pltpu.VMEM((d, fft), BF16),
                            pltpu.VMEM((fft, d), BF16)]),
        compiler_params=pltpu.CompilerParams(
            dimension_semantics=("arbitrary", "arbitrary"),
            vmem_limit_bytes=EXPERT_VMEM_BYTES),
        name="experts",
    )(blk_e, blk_rows, blk_src, xs, w1, w3, w2)


def _combine_kernel(final, nt, off_ref, row_ref, pad_ref, h_ref, mi_ref, gt_ref, g_ref, ys_ref,
                    out_ref, stack_s, sem):
    i = pl.program_id(0)
    tile = h_ref.shape[0]
    n_e = V7X_SUBLANES
    r_stack = stack_s.shape[1]
    slot = lax.rem(i, 2)
    copies = functools.partial(_segment_copies, tables=(off_ref, row_ref, pad_ref), n_e=n_e,
                               tile=tile, local_ref=stack_s, rows_ref=ys_ref, sem=sem, to_rows=False)

    @pl.when(i == 0)
    def _():
        stack_s[...] = jnp.zeros(stack_s.shape, F32)
        copies(i, slot, start=True)

    @pl.when(i + 1 < nt)
    def _():
        copies(i + 1, 1 - slot, start=True)

    copies(i, slot, start=False)
    y_rows = stack_s[slot].astype(BF16)
    row1, row2 = _stack_rows(mi_ref[...], off_ref, i, n_e)
    rid = lax.broadcasted_iota(I32, (r_stack, TOP_K * tile), 0)
    rows12 = jnp.concatenate([row1, row2], axis=1)
    y12 = _dot_tn(jnp.where(rid == rows12, 1.0, 0.0).astype(BF16), y_rows)
    gates = gt_ref[...]
    y = h_ref[...] + y12[:tile] * gates[:, 0:1] + y12[tile:] * gates[:, 1:2]
    out_ref[...] = _rms_norm(y, g_ref[...]) if final else y


def _combine(h2, mi, gates, tables, ys, g_final, final):
    n, d = h2.shape
    tile = min(MOE_TILE, n)
    nt = n // tile
    r_stack = TOP_K * tile + V7X_SUBLANES * ROW_PAD
    return pl.pallas_call(
        functools.partial(_combine_kernel, final, nt),
        out_shape=jax.ShapeDtypeStruct((n, d), F32),
        grid_spec=pltpu.PrefetchScalarGridSpec(
            num_scalar_prefetch=3,
            grid=(nt,),
            in_specs=[pl.BlockSpec((tile, d), lambda i, *_: (i, 0)),
                      pl.BlockSpec((V7X_SUBLANES, tile), lambda i, *_: (0, i)),
                      pl.BlockSpec((tile, TOP_K), lambda i, *_: (i, 0)),
                      pl.BlockSpec((1, d), lambda i, *_: (0, 0)),
                      pl.BlockSpec(memory_space=pl.ANY)],
            out_specs=pl.BlockSpec((tile, d), lambda i, *_: (i, 0)),
            scratch_shapes=[pltpu.VMEM((2, r_stack, d), F32), pltpu.SemaphoreType.DMA((2,))]),
        compiler_params=pltpu.CompilerParams(dimension_semantics=("arbitrary",)),
        name="combine",
    )(*tables, h2, mi, gates, g_final[None, :], ys)


def _moe_layer(h2, g, w_router, w1, w3, w2, g_final, final=True):
    n, d = h2.shape
    n_e = w_router.shape[1]
    rb = EXPERT_ROWS
    tile = min(MOE_TILE, n)
    nt = n // tile
    hn, mi, mf, cnt = _router(h2, g, w_router)
    pad = (cnt[:, :, 0] + ROW_PAD - 1) // ROW_PAD * ROW_PAD
    off = jnp.cumsum(pad, axis=1) - pad
    total = jnp.sum(pad, axis=0)
    padded = (total + rb - 1) // rb * rb
    pend = jnp.cumsum(padded)
    pstart = pend - padded
    row = pstart[None, :] + jnp.cumsum(pad, axis=0) - pad
    tables = tuple(t.reshape(-1).astype(I32) for t in (off, row, pad))
    nb = (n * TOP_K + nt * n_e * (ROW_PAD - 1)) // rb + n_e
    blk = jnp.arange(nb, dtype=I32)
    n_used = pend[-1] // rb
    last = jnp.maximum(n_used - 1, 0)
    blk_src = jnp.minimum(blk, last)
    blk_e = jnp.minimum(jnp.sum(blk_src[:, None] * rb >= pend[None, :], axis=1), n_e - 1).astype(I32)
    blk_rows = jnp.where(blk < n_used,
                         jnp.clip(total[blk_e] - (blk * rb - pstart[blk_e]), 0, rb), 0).astype(I32)
    fill = jnp.concatenate([pstart + total, padded - total, n_used[None]]).astype(I32)
    xs = _dispatch(hn, mi, tables, fill, nb * rb)
    ys = _experts(xs, blk_e, blk_rows, blk_src, w1, w3, w2)
    return _combine(h2, mi, mf[:TOP_K].T, tables, ys, g_final, final)


def kernel(x, w_in, w_out, lru_conv_w, lru_conv_b, lru_w_a, lru_b_a, lru_w_x, lru_b_x, lru_lambda,
           hgrn_lower_bounds, hgrn_norm_g, pool_w, pool_scale, sconv_w, norm_mix_g, norm_ffn_g,
           ffn_w_gate, ffn_w_up, ffn_w_down, moe_w_router, moe_w1, moe_w3, moe_w2, final_norm_g):
    B, S, D = x.shape
    depth = w_in.shape[0]

    def mix_params(layer):
        return _mix_params(norm_mix_g[layer], w_in[layer], w_out[layer], lru_conv_w[layer],
                           lru_conv_b[layer], lru_w_a[layer], lru_b_a[layer], lru_w_x[layer],
                           lru_b_x[layer], lru_lambda[layer], hgrn_lower_bounds, hgrn_norm_g[layer],
                           pool_w[layer], pool_scale[layer], sconv_w[layer])

    def ffn_layer(h2, layer):
        j = layer // 2
        final = layer == depth - 1
        if layer % 2 == 0:
            return _ffn_layer(h2, norm_ffn_g[layer], ffn_w_gate[j], ffn_w_up[j], ffn_w_down[j],
                              final_norm_g, final)
        return _moe_layer(h2, norm_ffn_g[layer], moe_w_router[j], moe_w1[j], moe_w3[j], moe_w2[j],
                          final_norm_g, final)

    h = x
    for layer in range(depth):
        h = _mix_layer(layer, h, mix_params(layer))
        h = ffn_layer(h.reshape(B * S, D), layer).reshape(B, S, D)
    return h
```

```python
import functools

import jax
import jax.numpy as jnp
from jax import lax
from jax.experimental import pallas as pl
from jax.experimental.pallas import tpu as pltpu

F32 = jnp.float32
BF16 = jnp.bfloat16
I32 = jnp.int32

GROUP_WIDTH = 256
N_GROUP_HEADS = 4
HEAD_DIM = 64
N_IN_SLICES = 10
LRU_CONV = 4
LRU_C = 8.0
HGRN_CHUNK = 64
POOL_WINDOWS = (2, 4, 8, 16)
SCONV_WIDTH = 3
TOP_K = 2
NORM_EPS = 1e-6
HEAD_NORM_EPS = 1e-5

V7X_LANES = 128
V7X_SUBLANES = 8
V7X_VMEM_BYTES = 64 * 1024 * 1024

MIX_TILE = 512
HALO = 16
DIAG = 8
FFN_TILE = 512
FFN_CHUNK = 256
MOE_TILE = 256
ROW_PAD = V7X_SUBLANES
EXPERT_ROWS = 2560
EXPERT_VMEM_BYTES = 60 * 1024 * 1024
EXPERT_SUB = 256
EXPERT_FULL = 512
EXPERT_FF = 512


def _dot(a, b):
    return jnp.dot(a, b, preferred_element_type=F32)


def _dot_nt(a, b):
    return lax.dot_general(a, b, (((1,), (1,)), ((), ())), preferred_element_type=F32)


def _dot_tn(a, b):
    return lax.dot_general(a, b, (((0,), (0,)), ((), ())), preferred_element_type=F32)


def _rms_norm(x, g):
    ms = jnp.mean(x * x, axis=-1, keepdims=True)
    return x * lax.rsqrt(ms + NORM_EPS) * g


def _split3(x):
    hi = x.astype(BF16)
    r1 = x - hi.astype(F32)
    mid = r1.astype(BF16)
    lo = (r1 - mid.astype(F32)).astype(BF16)
    return hi, mid, lo


N_MIX_PARAMS = 16
N_MIX_SCRATCH = 16


def _mix_reset(ti, scratch):
    (proj_s, ax_s, p_s, sc_s, sa_s, su_s, ta_s, tu_s, tp_s, lruh_s, st_s, q_s, k_s, lf_s, o_s,
     y_s) = scratch
    GW = GROUP_WIDTH

    @pl.when(ti == 0)
    def _():
        ax_s[0:HALO, :] = jnp.zeros((HALO, GW), F32)
        p_s[0:HALO, :] = jnp.zeros((HALO, GW), F32)
        sc_s[0:HALO, :] = jnp.zeros((HALO, GW), F32)
        sa_s[0:V7X_SUBLANES, :] = jnp.ones((V7X_SUBLANES, GW), F32)
        su_s[0:V7X_SUBLANES, :] = jnp.zeros((V7X_SUBLANES, GW), F32)
        ta_s[0:V7X_SUBLANES, :] = jnp.zeros((V7X_SUBLANES, GW), F32)
        tu_s[0:V7X_SUBLANES, :] = jnp.zeros((V7X_SUBLANES, GW), F32)
        tp_s[0:HALO, :] = jnp.zeros((HALO, GW), F32)
        lruh_s[...] = jnp.zeros(lruh_s.shape, F32)
        st_s[...] = jnp.zeros(st_s.shape, F32)


def _mix_tile(layer, x, ti, params, scratch):
    (gmix_ref, w_in_ref, w_out_ref, cw_ref, cb_ref, wg_ref, bg_ref, lam_ref, lbraw_ref, hg_ref,
     wp_ref, ps_ref, sw_ref, ones_ref, tri_ref, lvl_ref) = params
    (proj_s, ax_s, p_s, sc_s, sa_s, su_s, ta_s, tu_s, tp_s, lruh_s, st_s, q_s, k_s, lf_s, o_s,
     y_s) = scratch
    T = x.shape[0]
    GW = GROUP_WIDTH
    hn = _rms_norm(x, gmix_ref[...]).astype(BF16)
    proj_s[...] = _dot(hn, w_in_ref[...])

    def sl(i):
        return proj_s[:, i * GW:(i + 1) * GW]

    ax_s[HALO:HALO + T, :] = sl(0)
    a_in = cb_ref[...]
    for kk in range(LRU_CONV):
        off = HALO - (LRU_CONV - 1) + kk
        a_in = a_in + cw_ref[kk:kk + 1, :] * ax_s[off:off + T, :]
    gates = _dot(a_in.astype(BF16), wg_ref[...]) + bg_ref[...]
    r_gate = jax.nn.sigmoid(gates[:, :GW])
    i_gate = jax.nn.sigmoid(gates[:, GW:])
    log_a = (-LRU_C) * r_gate * jax.nn.softplus(-lam_ref[...])
    a = jnp.exp(log_a)
    mult = jnp.sqrt(1.0 - a * a)
    u = mult * i_gate * a_in
    SUB = V7X_SUBLANES
    sa_s[SUB:SUB + T, :] = a
    su_s[SUB:SUB + T, :] = u
    a = sa_s[...]
    u = su_s[...]
    d = 1
    while d < SUB:
        ta_s[SUB:, :] = a
        tu_s[SUB:, :] = u
        a_sh = ta_s[SUB - d:SUB - d + SUB + T, :]
        u_sh = tu_s[SUB - d:SUB - d + SUB + T, :]
        u = a * u_sh + u
        a = a * a_sh
        d *= 2
    sa_s[0:SUB, :] = sa_s[T:T + SUB, :]
    su_s[0:SUB, :] = su_s[T:T + SUB, :]
    h_grp = lruh_s[...]
    groups = []
    for gi in range(1, T // SUB + 1):
        h_grp = a[gi * SUB:(gi + 1) * SUB, :] * h_grp + u[gi * SUB:(gi + 1) * SUB, :]
        groups.append(h_grp)
    lruh_s[...] = h_grp
    h_lru = jnp.concatenate(groups, axis=0)
    y_a = h_lru * jax.nn.gelu(sl(1))
    y_s[:, 0:GW] = y_a.astype(BF16)
    ax_s[0:HALO, :] = ax_s[T:T + HALO, :]

    p = sl(6)
    p_s[HALO:HALO + T, :] = p
    lane = lax.broadcasted_iota(I32, (1, GW), 1)
    win = jnp.where(lane < HEAD_DIM, float(POOL_WINDOWS[0]),
                    jnp.where(lane < 2 * HEAD_DIM, float(POOL_WINDOWS[1]),
                              jnp.where(lane < 3 * HEAD_DIM, float(POOL_WINDOWS[2]),
                                        float(POOL_WINDOWS[3]))))
    sums = []
    acc = p_s[...]
    d = 1
    while d < max(POOL_WINDOWS):
        tp_s[HALO:, :] = acc
        acc = acc + tp_s[HALO - d:HALO - d + HALO + T, :]
        sums.append(acc[HALO:, :])
        d *= 2
    wsum = jnp.where(lane < HEAD_DIM, sums[0],
                     jnp.where(lane < 2 * HEAD_DIM, sums[1],
                               jnp.where(lane < 3 * HEAD_DIM, sums[2], sums[3])))
    pos =(ti * T + 1 + lax.broadcasted_iota(I32, (T, 1), 0)).astype(F32)
    dpool = wsum / jnp.minimum(pos, win) - p
    y_c = _dot(dpool.astype(BF16), wp_ref[...]) * ps_ref[...]
    y_s[:, 2 * GW:3 * GW] = y_c.astype(BF16)
    p_s[0:HALO, :] = p_s[T:T + HALO, :]

    sc_s[HALO:HALO + T, :] = sl(8) * sl(9)
    conv = jnp.zeros((T, GW), F32)
    for kk in range(SCONV_WIDTH):
        off = HALO - (SCONV_WIDTH - 1) + kk
        conv = conv + sw_ref[kk:kk + 1, :] * sc_s[off:off + T, :]
    y_s[:, 3 * GW:4 * GW] = (sl(7) * conv).astype(BF16)
    sc_s[0:HALO, :] = sc_s[T:T + HALO, :]

    lbraw = lbraw_ref[...]
    e_lb = jnp.exp(lbraw - jnp.max(lbraw, axis=0, keepdims=True))
    p_lb = e_lb / jnp.sum(e_lb, axis=0, keepdims=True)
    lb = jnp.zeros((1, GW), F32)
    for li in range(1, layer + 1):
        lb = lb + p_lb[li:li + 1, :]
    z = sl(3)
    lf_s[...] = jnp.log(lb + (1.0 - lb) * jax.nn.sigmoid(z))
    k_s[...] = (1.0 - lb) * jax.nn.sigmoid(-z)
    q_s[...] = jax.nn.silu(sl(2))

    L = HGRN_CHUNK
    ones_bd = ones_ref[...]
    ones_f = ones_bd.astype(F32)
    tri = tri_ref[...]
    widths = _level_widths()
    row_l = lax.broadcasted_iota(I32, (L, 1), 0)
    row_d = lax.broadcasted_iota(I32, (DIAG, 1), 0)

    def chunk_body(c, carry):
        r0 = c * L
        lf = lf_s[pl.ds(r0, L), :]
        hi, mid, lo = _split3(lf)
        G = _dot(tri, hi) + _dot(tri, mid) + _dot(tri, lo)
        q = q_s[pl.ds(r0, L), :]
        k = k_s[pl.ds(r0, L), :]
        v = proj_s[pl.ds(r0, L), 4 * GW:5 * GW]
        vb = v.astype(BF16)
        g_last = G[L - 1:L, :]
        st = st_s[...]

        o = _dot_nt((q * jnp.exp(G)).astype(BF16), st.astype(BF16))

        scores = jnp.zeros((N_GROUP_HEADS * L, L), F32)
        for li, w in enumerate(widths):
            gref = G[w - 1:w, :]
            for pair in range(1, L // (2 * w)):
                lastrow = pair * 2 * w + w - 1
                gref = jnp.where(row_l < pair * 2 * w, gref, G[lastrow:lastrow + 1, :])
            qt = q * jnp.exp(jnp.minimum(G - gref, 0.0))
            kt = k * jnp.exp(jnp.minimum(gref - G, 0.0))
            q4 = jnp.concatenate([qt.astype(BF16)] * N_GROUP_HEADS, axis=0) * ones_bd
            s4 = _dot_nt(q4, kt.astype(BF16))
            scores = scores + s4 * lvl_ref[li]
        o4 = _dot(scores.astype(BF16), vb) * ones_f
        for hh in range(N_GROUP_HEADS):
            o = o + o4[hh * L:(hh + 1) * L, :]

        o_diag = []
        for b in range(L // DIAG):
            gs = G[b * DIAG:(b + 1) * DIAG, :]
            qs = q[b * DIAG:(b + 1) * DIAG, :]
            ks = k[b * DIAG:(b + 1) * DIAG, :]
            vs = v[b * DIAG:(b + 1) * DIAG, :]
            terms = []
            for j in range(DIAG):
                dg = jnp.where(row_d >= j, gs - gs[j:j + 1, :], -jnp.inf)
                terms.append((qs * (ks[j:j + 1, :] * jnp.exp(dg))).astype(BF16))
            e = jnp.concatenate(terms, axis=0)
            pd = _dot(e, ones_bd)
            od = jnp.zeros((DIAG, GW), F32)
            for j in range(DIAG):
                od = od + pd[j * DIAG:(j + 1) * DIAG, :] * vs[j:j + 1, :]
            o_diag.append(od)
        o = o + jnp.concatenate(o_diag, axis=0)
        o_s[pl.ds(r0, L), :] = o

        kd = (k * jnp.exp(g_last - G)).astype(BF16)
        st_s[...] = st * jnp.exp(g_last) + _dot_tn(vb, kd) * ones_f
        return carry

    for c in range(T // L):
        chunk_body(c, 0)

    o = o_s[...]
    ms = _dot((o * o).astype(BF16), ones_bd) * (1.0 / HEAD_DIM)
    o = o * lax.rsqrt(ms + HEAD_NORM_EPS) * hg_ref[...]
    y_s[:, GW:2 * GW] = (o * jax.nn.silu(sl(5))).astype(BF16)

    return x + _dot(y_s[...], w_out_ref[...])


def _mix_kernel(layer, h_ref, *refs):
    params = refs[:N_MIX_PARAMS]
    out_ref = refs[N_MIX_PARAMS]
    scratch = refs[N_MIX_PARAMS + 1:]
    _mix_reset(pl.program_id(1), scratch)
    out_ref[...] = _mix_tile(layer, h_ref[...], pl.program_id(1), params, scratch)


def _level_widths():
    widths = []
    w = DIAG
    while w < HGRN_CHUNK:
        widths.append(w)
        w *= 2
    return widths


def _level_masks():
    L = HGRN_CHUNK
    t = jnp.arange(N_GROUP_HEADS * L)[:, None] % L
    s = jnp.arange(L)[None, :]
    return jnp.stack([(((t // w) % 2 == 1) & (s // w == t // w - 1)).astype(F32)
                      for w in _level_widths()])


def _block_diag(w):
    return jax.scipy.linalg.block_diag(*[w[i] for i in range(w.shape[0])])


def _const_spec(shape):
    nd = len(shape)
    return pl.BlockSpec(shape, lambda *_: (0,) * nd)


def _mix_params(gmix, w_in, w_out, conv_w, conv_b, w_a, b_a, w_x, b_x, lam, lb_raw, norm_g,
                pool_w, pool_scale, sconv_w):
    GW = GROUP_WIDTH
    assert w_in.shape[1] == N_IN_SLICES * GW and POOL_WINDOWS == (2, 4, 8, 16)
    wg = jnp.concatenate([_block_diag(w_a), _block_diag(w_x)], axis=1).astype(BF16)
    bg = jnp.concatenate([b_a, b_x])[None, :]
    head_of = jnp.arange(GW) // HEAD_DIM
    ones_bd = (head_of[:, None] == head_of[None, :]).astype(BF16)
    tri = jnp.tril(jnp.ones((HGRN_CHUNK, HGRN_CHUNK), BF16))
    params = [gmix[None, :], w_in.astype(BF16), w_out.astype(BF16), conv_w, conv_b[None, :], wg, bg,
              lam[None, :], lb_raw, jnp.tile(norm_g, N_GROUP_HEADS)[None, :],
              _block_diag(pool_w).astype(BF16), pool_scale[None, :], sconv_w, ones_bd, tri,
              _level_masks()]
    assert len(params) == N_MIX_PARAMS
    return params


def _mix_scratch(T):
    GW = GROUP_WIDTH
    scratch = [
        pltpu.VMEM((T, N_IN_SLICES * GW), F32),
        pltpu.VMEM((HALO + T, GW), F32),
        pltpu.VMEM((HALO + T, GW), F32),
        pltpu.VMEM((HALO + T, GW), F32),
        pltpu.VMEM((V7X_SUBLANES + T, GW), F32),
        pltpu.VMEM((V7X_SUBLANES + T, GW), F32),
        pltpu.VMEM((2 * V7X_SUBLANES + T, GW), F32),
        pltpu.VMEM((2 * V7X_SUBLANES + T, GW), F32),
        pltpu.VMEM((2 * HALO + T, GW), F32),
        pltpu.VMEM((V7X_SUBLANES, GW), F32),
        pltpu.VMEM((GW, GW), F32),
        pltpu.VMEM((T, GW), F32),
        pltpu.VMEM((T, GW), F32),
        pltpu.VMEM((T, GW), F32),
        pltpu.VMEM((T, GW), F32),
        pltpu.VMEM((T, N_GROUP_HEADS * GW), BF16),
    ]
    assert len(scratch) == N_MIX_SCRATCH
    return scratch


def _mix_layer(layer, h, params):
    B, S, D = h.shape
    T = min(MIX_TILE, S)
    assert S % T == 0 and T % HGRN_CHUNK == 0
    in_specs = [pl.BlockSpec((None, T, D), lambda b, t: (b, t, 0))]
    in_specs += [_const_spec(a.shape) for a in params]
    return pl.pallas_call(
        functools.partial(_mix_kernel, layer),
        out_shape=jax.ShapeDtypeStruct((B, S, D), F32),
        grid=(B, S // T),
        in_specs=in_specs,
        out_specs=pl.BlockSpec((None, T, D), lambda b, t: (b, t, 0)),
        scratch_shapes=_mix_scratch(T),
        compiler_params=pltpu.CompilerParams(
            dimension_semantics=("arbitrary", "arbitrary"),
            vmem_limit_bytes=48 * 1024 * 1024),
        name=f"mix{layer}",
    )(h, *params)


def _ffn_tile(x, g_ref, wg_ref, wu_ref, wd_ref, act_s):
    hn = _rms_norm(x, g_ref[...]).astype(BF16)
    ff = wg_ref.shape[1]
    for c in range(0, ff, FFN_CHUNK):
        gate = _dot(hn, wg_ref[:, c:c + FFN_CHUNK])
        up = _dot(hn, wu_ref[:, c:c + FFN_CHUNK])
        act_s[:, c:c + FFN_CHUNK] = (jax.nn.silu(gate) * up).astype(BF16)
    return x + _dot(act_s[...], wd_ref[...])


def _ffn_kernel(final, h_ref, g_ref, gf_ref, wg_ref, wu_ref, wd_ref, out_ref, act_s):
    y = _ffn_tile(h_ref[...], g_ref, wg_ref, wu_ref, wd_ref, act_s)
    out_ref[...] = _rms_norm(y, gf_ref[...]) if final else y


def _ffn_layer(h2, g, w_gate, w_up, w_down, g_final, final):
    n, d = h2.shape
    ff = w_gate.shape[1]
    tm = min(FFN_TILE, n)
    assert n % tm == 0 and ff % FFN_CHUNK == 0
    once = pl.Buffered(1)
    return pl.pallas_call(
        functools.partial(_ffn_kernel, final),
        out_shape=jax.ShapeDtypeStruct((n, d), F32),
        grid=(n // tm,),
        in_specs=[pl.BlockSpec((tm, d), lambda i: (i, 0)),
                  _const_spec((1, d)),
                  _const_spec((1, d)),
                  pl.BlockSpec((d, ff), lambda i: (0, 0), pipeline_mode=once),
                  pl.BlockSpec((d, ff), lambda i: (0, 0), pipeline_mode=once),
                  pl.BlockSpec((ff, d), lambda i: (0, 0), pipeline_mode=once)],
        out_specs=pl.BlockSpec((tm, d), lambda i: (i, 0)),
        scratch_shapes=[pltpu.VMEM((tm, ff), BF16)],
        compiler_params=pltpu.CompilerParams(
            dimension_semantics=("arbitrary",),
            vmem_limit_bytes=48 * 1024 * 1024),
        name="ffn",
    )(h2, g[None, :], g_final[None, :], w_gate.astype(BF16), w_up.astype(BF16), w_down.astype(BF16))


def _router_kernel(h_ref, g_ref, wr_ref, triu_ref, hn_ref, mi_ref, mf_ref, cnt_ref):
    n_e = wr_ref.shape[0]
    tr = h_ref.shape[0]
    hn = _rms_norm(h_ref[...], g_ref[...])
    xh = hn.astype(BF16)
    hn_ref[...] = xh
    xm = (hn - xh.astype(F32)).astype(BF16)
    w = wr_ref[...]
    wh = w.astype(BF16)
    wm = (w - wh.astype(F32)).astype(BF16)
    logits = _dot_nt(wh, xh) + _dot_nt(wh, xm) + _dot_nt(wm, xh)
    eid = lax.broadcasted_iota(I32, (n_e, tr), 0)
    m1 = jnp.max(logits, axis=0, keepdims=True)
    e1 = jnp.min(jnp.where(logits == m1, eid, n_e), axis=0, keepdims=True)
    rest = jnp.where(eid == e1, -jnp.inf, logits)
    m2 = jnp.max(rest, axis=0, keepdims=True)
    e2 = jnp.min(jnp.where(rest == m2, eid, n_e), axis=0, keepdims=True)
    ex = jnp.exp(m2 - m1)
    g1 = 1.0 / (1.0 + ex)
    g2 = ex / (1.0 + ex)
    member = jnp.where((eid == e1) | (eid == e2), 1.0, 0.0)
    incl = _dot(member.astype(BF16), triu_ref[...])
    rank = incl - member
    r1 = jnp.sum(jnp.where(eid == e1, rank, 0.0), axis=0, keepdims=True)
    r2 = jnp.sum(jnp.where(eid == e2, rank, 0.0), axis=0, keepdims=True)
    cnt_ref[...] = jnp.broadcast_to(incl[:, tr - 1:tr], cnt_ref.shape).astype(I32)
    zi = jnp.zeros((1, tr), I32)
    mi_ref[...] = jnp.concatenate(
        [e1, e2, r1.astype(I32), r2.astype(I32), zi, zi, zi, zi], axis=0)
    zf = jnp.zeros((1, tr), F32)
    mf_ref[...] = jnp.concatenate([g1, g2, zf, zf, zf, zf, zf, zf], axis=0)


def _router(h2, g, w_router):
    n, d = h2.shape
    n_e = w_router.shape[1]
    tr = min(MOE_TILE, n)
    assert n % tr == 0 and n_e == V7X_SUBLANES
    tok = jnp.arange(tr)
    triu = (tok[:, None] <= tok[None, :]).astype(BF16)
    return pl.pallas_call(
        _router_kernel,
        out_shape=(jax.ShapeDtypeStruct((n, d), BF16),
                   jax.ShapeDtypeStruct((V7X_SUBLANES, n), I32),
                   jax.ShapeDtypeStruct((V7X_SUBLANES, n), F32),
                   jax.ShapeDtypeStruct((n // tr, n_e, V7X_LANES), I32)),
        grid=(n // tr,),
        in_specs=[pl.BlockSpec((tr, d), lambda i: (i, 0)),
                  _const_spec((1, d)),
                  _const_spec((n_e, d)),
                  _const_spec((tr, tr))],
        out_specs=(pl.BlockSpec((tr, d), lambda i: (i, 0)),
                   pl.BlockSpec((V7X_SUBLANES, tr), lambda i: (0, i)),
                   pl.BlockSpec((V7X_SUBLANES, tr), lambda i: (0, i)),
                   pl.BlockSpec((None, n_e, V7X_LANES), lambda i: (i, 0, 0))),
        compiler_params=pltpu.CompilerParams(dimension_semantics=("arbitrary",)),
        name="router",
    )(h2, g[None, :], w_router.T, triu)


def _segment_copies(step, slot, tables, n_e, tile, local_ref, rows_ref, sem, to_rows, start):
    off_ref, row_ref, pad_ref = tables

    def copy(off, row, size):
        local = local_ref.at[slot, pl.ds(pl.multiple_of(off, ROW_PAD), size)]
        remote = rows_ref.at[pl.ds(pl.multiple_of(row, ROW_PAD), size)]
        return (pltpu.make_async_copy(local, remote, sem.at[slot]) if to_rows
                else pltpu.make_async_copy(remote, local, sem.at[slot]))

    if not start:
        last = step * n_e + n_e - 1
        total = off_ref[last] + pad_ref[last]
        size = TOP_K * tile
        while size >= ROW_PAD:
            pl.when((total & size) != 0)(copy(0, 0, size).wait)
            size //= 2
        return
    for e in range(n_e):
        off = off_ref[step * n_e + e]
        row = row_ref[step * n_e + e]
        pad = pad_ref[step * n_e + e]
        size = tile
        while size >= ROW_PAD:
            done = pad & (-2 * size)
            pl.when((pad & size) != 0)(copy(off + done, row + done, size).start)
            size //= 2


def _stack_rows(mi, off_ref, step, n_e):
    e1, e2, row1, row2 = mi[0:1, :], mi[1:2, :], mi[2:3, :], mi[3:4, :]
    for e in range(n_e):
        off = off_ref[step * n_e + e]
        row1 = row1 + jnp.where(e1 == e, off, 0)
        row2 = row2 + jnp.where(e2 == e, off, 0)
    return row1, row2


def _fill_copies(fill_ref, n_e, zero_s, xs_ref, sem, start):
    zrows = zero_s.shape[0]
    rb = EXPERT_ROWS

    def go(cp):
        if start:
            cp.start()
        else:
            cp.wait()

    def zero_copy(row, size):
        return pltpu.make_async_copy(
            zero_s.at[pl.ds(0, size)], xs_ref.at[pl.ds(pl.multiple_of(row, ROW_PAD), size)], sem)

    for e in range(n_e):
        row = fill_ref[e]
        gap = fill_ref[n_e + e]
        size = zrows
        while size >= ROW_PAD:
            done = gap & (-2 * size)
            pl.when((gap & size) != 0)(functools.partial(go, zero_copy(row + done, size)))
            size //= 2

    def block(b, c):
        done = 0
        size = zrows
        while done < rb:
            if size <= rb - done:
                go(zero_copy(b * rb + done, size))
                done += size
            else:
                size //= 2
        return c

    lax.fori_loop(fill_ref[2 * n_e], xs_ref.shape[0] // rb, block, 0)


def _dispatch_kernel(nt, off_ref, row_ref, pad_ref, fill_ref, hn_ref, mi_ref, xs_ref, slab_s, zero_s,
                     sem, fill_sem):
    i = pl.program_id(0)
    tile = hn_ref.shape[0]
    n_e = V7X_SUBLANES
    r_stack = slab_s.shape[1]
    slot = lax.rem(i, 2)
    copies = functools.partial(_segment_copies, tables=(off_ref, row_ref, pad_ref), n_e=n_e,
                               tile=tile, local_ref=slab_s, rows_ref=xs_ref, sem=sem, to_rows=True)

    @pl.when(i == 0)
    def _():
        zero_s[...] = jnp.zeros(zero_s.shape, F32)
        _fill_copies(fill_ref, n_e, zero_s, xs_ref, fill_sem, start=True)

    @pl.when(i >= 2)
    def _():
        copies(i - 2, slot, start=False)

    row1, row2 = _stack_rows(mi_ref[...], off_ref, i, n_e)
    rid = lax.broadcasted_iota(I32, (r_stack, tile), 0)
    sel = jnp.where(rid == row1, 1.0, jnp.where(rid == row2, 1.0, 0.0)).astype(BF16)
    slab_s[slot] = _dot(sel, hn_ref[...])
    copies(i, slot, start=True)

    @pl.when(i == nt - 1)
    def _():
        if nt > 1:
            copies(i - 1, 1 - slot, start=False)
        copies(i, slot, start=False)
        _fill_copies(fill_ref, n_e, zero_s, xs_ref, fill_sem, start=False)


def _dispatch(hn, mi, tables, fill, n_rows):
    n, d = hn.shape
    tile = min(MOE_TILE, n)
    nt = n // tile
    r_stack = TOP_K * tile + V7X_SUBLANES * ROW_PAD
    return pl.pallas_call(
        functools.partial(_dispatch_kernel, nt),
        out_shape=jax.ShapeDtypeStruct((n_rows, d), F32),
        grid_spec=pltpu.PrefetchScalarGridSpec(
            num_scalar_prefetch=4,
            grid=(nt,),
            in_specs=[pl.BlockSpec((tile, d), lambda i, *_: (i, 0)),
                      pl.BlockSpec((V7X_SUBLANES, tile), lambda i, *_: (0, i))],
            out_specs=pl.BlockSpec(memory_space=pl.ANY),
            scratch_shapes=[pltpu.VMEM((2, r_stack, d), F32),
                            pltpu.VMEM((pl.next_power_of_2(EXPERT_ROWS) // 2, d), F32),
                            pltpu.SemaphoreType.DMA((2,)), pltpu.SemaphoreType.DMA(())]),
        compiler_params=pltpu.CompilerParams(dimension_semantics=("arbitrary",),
                                             has_side_effects=True),
        name="dispatch",
    )(*tables, fill, hn, mi)


def _expert_kernel(be_ref, rows_ref, src_ref, x_ref, w1_ref, w3_ref, w2_ref, out_ref,
                   w1_s, w3_s, w2_s):
    del be_ref, src_ref
    b = pl.program_id(0)
    f = pl.program_id(1)
    rows = rows_ref[b]

    @pl.when(f == 0)
    def _():
        out_ref[...] = jnp.zeros(out_ref.shape, F32)

    def cast_weights():
        w1_s[...] = w1_ref[...].astype(BF16)
        w3_s[...] = w3_ref[...].astype(BF16)
        w2_s[...] = w2_ref[...].astype(BF16)

    def swiglu_rows(r0, n_rows):
        xb = x_ref[pl.ds(r0, n_rows), :].astype(BF16)
        gate = _dot(xb, w1_s[...])
        up = _dot(xb, w3_s[...])
        act = (jax.nn.silu(gate) * up).astype(BF16)
        out_ref[pl.ds(r0, n_rows), :] += _dot(act, w2_s[...])

    @pl.when(rows == EXPERT_ROWS)
    def _():
        cast_weights()
        for s in range(EXPERT_ROWS // EXPERT_FULL):
            swiglu_rows(s * EXPERT_FULL, EXPERT_FULL)

    @pl.when((rows > 0) & (rows < EXPERT_ROWS))
    def _():
        cast_weights()

        n_sub = (rows + EXPERT_SUB - 1) // EXPERT_SUB

        def pair(s, c):
            swiglu_rows(pl.multiple_of(s * EXPERT_FULL, EXPERT_FULL), EXPERT_FULL)
            return c

        lax.fori_loop(0, n_sub // 2, pair, 0)

        @pl.when(n_sub % 2 == 1)
        def _():
            swiglu_rows(pl.multiple_of((n_sub - 1) * EXPERT_SUB, EXPERT_SUB), EXPERT_SUB)


def _experts(xs, blk_e, blk_rows, blk_src, w1, w3, w2):
    n_rows, d = xs.shape
    n_e, _, ff = w1.shape
    rb = EXPERT_ROWS
    fft = min(EXPERT_FF, ff)
    assert n_rows % rb == 0 and ff % fft == 0
    assert EXPERT_FULL == 2 * EXPERT_SUB and rb % EXPERT_FULL == 0
    nf = ff // fft
    nb = n_rows // rb

    def f_eff(b, f, rows):
        return jnp.where(rows[b] > 0, f, nf - 1)

    return pl.pallas_call(
        _expert_kernel,
        out_shape=jax.ShapeDtypeStruct((n_rows, d), F32),
        grid_spec=pltpu.PrefetchScalarGridSpec(
            num_scalar_prefetch=3,
            grid=(nb, nf),
            in_specs=[
                pl.BlockSpec((rb, d), lambda b, f, be, rows, src: (src[b], 0)),
                pl.BlockSpec((None, d, fft), lambda b, f, be, rows, src: (be[b], 0, f_eff(b, f, rows))),
                pl.BlockSpec((None, d, fft), lambda b, f, be, rows, src: (be[b], 0, f_eff(b, f, rows))),
                pl.BlockSpec((None, fft, d), lambda b, f, be, rows, src: (be[b], f_eff(b, f, rows), 0)),
            ],
            out_specs=pl.BlockSpec((rb, d), lambda b, f, be, rows, src: (b, 0)),
            scratch_shapes=[pltpu.VMEM((d, fft), BF16), pltpu.VMEM((d, fft), BF16),
                            pltpu.VMEM((fft, d), BF16)]),
        compiler_params=pltpu.CompilerParams(
            dimension_semantics=("arbitrary", "arbitrary"),
            vmem_limit_bytes=EXPERT_VMEM_BYTES),
        name="experts",
    )(blk_e, blk_rows, blk_src, xs, w1, w3, w2)


def _combine_kernel(final, nt, off_ref, row_ref, pad_ref, h_ref, mi_ref, gt_ref, g_ref, ys_ref,
                    out_ref, stack_s, sem):
    i = pl.program_id(0)
    tile = h_ref.shape[0]
    n_e = V7X_SUBLANES
    r_stack = stack_s.shape[1]
    slot = lax.rem(i, 2)
    copies = functools.partial(_segment_copies, tables=(off_ref, row_ref, pad_ref), n_e=n_e,
                               tile=tile, local_ref=stack_s, rows_ref=ys_ref, sem=sem, to_rows=False)

    @pl.when(i == 0)
    def _():
        stack_s[...] = jnp.zeros(stack_s.shape, F32)
        copies(i, slot, start=True)

    @pl.when(i + 1 < nt)
    def _():
        copies(i + 1, 1 - slot, start=True)

    copies(i, slot, start=False)
    y_rows = stack_s[slot].astype(BF16)
    row1, row2 = _stack_rows(mi_ref[...], off_ref, i, n_e)
    rid = lax.broadcasted_iota(I32, (r_stack, TOP_K * tile), 0)
    rows12 = jnp.concatenate([row1, row2], axis=1)
    y12 = _dot_tn(jnp.where(rid == rows12, 1.0, 0.0).astype(BF16), y_rows)
    gates = gt_ref[...]
    y = h_ref[...] + y12[:tile] * gates[:, 0:1] + y12[tile:] * gates[:, 1:2]
    out_ref[...] = _rms_norm(y, g_ref[...]) if final else y


def _combine(h2, mi, gates, tables, ys, g_final, final):
    n, d = h2.shape
    tile = min(MOE_TILE, n)
    nt = n // tile
    r_stack = TOP_K * tile + V7X_SUBLANES * ROW_PAD
    return pl.pallas_call(
        functools.partial(_combine_kernel, final, nt),
        out_shape=jax.ShapeDtypeStruct((n, d), F32),
        grid_spec=pltpu.PrefetchScalarGridSpec(
            num_scalar_prefetch=3,
            grid=(nt,),
            in_specs=[pl.BlockSpec((tile, d), lambda i, *_: (i, 0)),
                      pl.BlockSpec((V7X_SUBLANES, tile), lambda i, *_: (0, i)),
                      pl.BlockSpec((tile, TOP_K), lambda i, *_: (i, 0)),
                      pl.BlockSpec((1, d), lambda i, *_: (0, 0)),
                      pl.BlockSpec(memory_space=pl.ANY)],
            out_specs=pl.BlockSpec((tile, d), lambda i, *_: (i, 0)),
            scratch_shapes=[pltpu.VMEM((2, r_stack, d), F32), pltpu.SemaphoreType.DMA((2,))]),
        compiler_params=pltpu.CompilerParams(dimension_semantics=("arbitrary",)),
        name="combine",
    )(*tables, h2, mi, gates, g_final[None, :], ys)


def _moe_layer(h2, g, w_router, w1, w3, w2, g_final, final=True):
    n, d = h2.shape
    n_e = w_router.shape[1]
    rb = EXPERT_ROWS
    tile = min(MOE_TILE, n)
    nt = n // tile
    hn, mi, mf, cnt = _router(h2, g, w_router)
    pad = (cnt[:, :, 0] + ROW_PAD - 1) // ROW_PAD * ROW_PAD
    off = jnp.cumsum(pad, axis=1) - pad
    total = jnp.sum(pad, axis=0)
    padded = (total + rb - 1) // rb * rb
    pend = jnp.cumsum(padded)
    pstart = pend - padded
    row = pstart[None, :] + jnp.cumsum(pad, axis=0) - pad
    tables = tuple(t.reshape(-1).astype(I32) for t in (off, row, pad))
    nb = (n * TOP_K + nt * n_e * (ROW_PAD - 1)) // rb + n_e
    blk = jnp.arange(nb, dtype=I32)
    n_used = pend[-1] // rb
    last = jnp.maximum(n_used - 1, 0)
    blk_src = jnp.minimum(blk, last)
    blk_e = jnp.minimum(jnp.sum(blk_src[:, None] * rb >= pend[None, :], axis=1), n_e - 1).astype(I32)
    blk_rows = jnp.where(blk < n_used,
                         jnp.clip(total[blk_e] - (blk * rb - pstart[blk_e]), 0, rb), 0).astype(I32)
    fill = jnp.concatenate([pstart + total, padded - total, n_used[None]]).astype(I32)
    xs = _dispatch(hn, mi, tables, fill, nb * rb)
    ys = _experts(xs, blk_e, blk_rows, blk_src, w1, w3, w2)
    return _combine(h2, mi, mf[:TOP_K].T, tables, ys, g_final, final)


def kernel(x, w_in, w_out, lru_conv_w, lru_conv_b, lru_w_a, lru_b_a, lru_w_x, lru_b_x, lru_lambda,
           hgrn_lower_bounds, hgrn_norm_g, pool_w, pool_scale, sconv_w, norm_mix_g, norm_ffn_g,
           ffn_w_gate, ffn_w_up, ffn_w_down, moe_w_router, moe_w1, moe_w3, moe_w2, final_norm_g):
    B, S, D = x.shape
    depth = w_in.shape[0]

    def mix_params(layer):
        return _mix_params(norm_mix_g[layer], w_in[layer], w_out[layer], lru_conv_w[layer],
                           lru_conv_b[layer], lru_w_a[layer], lru_b_a[layer], lru_w_x[layer],
                           lru_b_x[layer], lru_lambda[layer], hgrn_lower_bounds, hgrn_norm_g[layer],
                           pool_w[layer], pool_scale[layer], sconv_w[layer])

    def ffn_layer(h2, layer):
        j = layer // 2
        final = layer == depth - 1
        if layer % 2 == 0:
            return _ffn_layer(h2, norm_ffn_g[layer], ffn_w_gate[j], ffn_w_up[j], ffn_w_down[j],
                              final_norm_g, final)
        return _moe_layer(h2, norm_ffn_g[layer], moe_w_router[j], moe_w1[j], moe_w3[j], moe_w2[j],
                          final_norm_g, final)

    h = x
    for layer in range(depth):
        h = _mix_layer(layer, h, mix_params(layer))
        h = ffn_layer(h.reshape(B * S, D), layer).reshape(B, S, D)
    return h
```

```python
import functools

import jax
import jax.numpy as jnp
from jax import lax
from jax.experimental import pallas as pl
from jax.experimental.pallas import tpu as pltpu

F32 = jnp.float32
BF16 = jnp.bfloat16
I32 = jnp.int32

GROUP_WIDTH = 256
N_GROUP_HEADS = 4
HEAD_DIM = 64
N_IN_SLICES = 10
LRU_CONV = 4
LRU_C = 8.0
HGRN_CHUNK = 64
POOL_WINDOWS = (2, 4, 8, 16)
SCONV_WIDTH = 3
TOP_K = 2
NORM_EPS = 1e-6
HEAD_NORM_EPS = 1e-5

V7X_LANES = 128
V7X_SUBLANES = 8
V7X_VMEM_BYTES = 64 * 1024 * 1024

MIX_TILE = 512
HALO = 16
DIAG = 8
FFN_TILE = 512
FFN_CHUNK = 256
MOE_TILE = 256
ROW_PAD = V7X_SUBLANES
EXPERT_ROWS = 2560
EXPERT_VMEM_BYTES = 60 * 1024 * 1024
EXPERT_SUB = 256
EXPERT_FULL = 512
EXPERT_FF = 512


def _dot(a, b):
    return jnp.dot(a, b, preferred_element_type=F32)


def _dot_nt(a, b):
    return lax.dot_general(a, b, (((1,), (1,)), ((), ())), preferred_element_type=F32)


def _dot_tn(a, b):
    return lax.dot_general(a, b, (((0,), (0,)), ((), ())), preferred_element_type=F32)


def _rms_norm(x, g):
    ms = jnp.mean(x * x, axis=-1, keepdims=True)
    return x * lax.rsqrt(ms + NORM_EPS) * g


def _split3(x):
    hi = x.astype(BF16)
    r1 = x - hi.astype(F32)
    mid = r1.astype(BF16)
    lo = (r1 - mid.astype(F32)).astype(BF16)
    return hi, mid, lo


N_MIX_PARAMS = 16
N_MIX_SCRATCH = 16


def _mix_reset(ti, scratch):
    (proj_s, ax_s, p_s, sc_s, sa_s, su_s, ta_s, tu_s, tp_s, lruh_s, st_s, q_s, k_s, lf_s, o_s,
     y_s) = scratch
    GW = GROUP_WIDTH

    @pl.when(ti == 0)
    def _():
        ax_s[0:HALO, :] = jnp.zeros((HALO, GW), F32)
        p_s[0:HALO, :] = jnp.zeros((HALO, GW), F32)
        sc_s[0:HALO, :] = jnp.zeros((HALO, GW), F32)
        sa_s[0:V7X_SUBLANES, :] = jnp.ones((V7X_SUBLANES, GW), F32)
        su_s[0:V7X_SUBLANES, :] = jnp.zeros((V7X_SUBLANES, GW), F32)
        ta_s[0:V7X_SUBLANES, :] = jnp.zeros((V7X_SUBLANES, GW), F32)
        tu_s[0:V7X_SUBLANES, :] = jnp.zeros((V7X_SUBLANES, GW), F32)
        tp_s[0:HALO, :] = jnp.zeros((HALO, GW), F32)
        lruh_s[...] = jnp.zeros(lruh_s.shape, F32)
        st_s[...] = jnp.zeros(st_s.shape, F32)


def _mix_tile(layer, x, ti, params, scratch):
    (gmix_ref, w_in_ref, w_out_ref, cw_ref, cb_ref, wg_ref, bg_ref, lam_ref, lbraw_ref, hg_ref,
     wp_ref, ps_ref, sw_ref, ones_ref, tri_ref, lvl_ref) = params
    (proj_s, ax_s, p_s, sc_s, sa_s, su_s, ta_s, tu_s, tp_s, lruh_s, st_s, q_s, k_s, lf_s, o_s,
     y_s) = scratch
    T = x.shape[0]
    GW = GROUP_WIDTH
    hn = _rms_norm(x, gmix_ref[...]).astype(BF16)
    proj_s[...] = _dot(hn, w_in_ref[...])

    def sl(i):
        return proj_s[:, i * GW:(i + 1) * GW]

    ax_s[HALO:HALO + T, :] = sl(0)
    a_in = cb_ref[...]
    for kk in range(LRU_CONV):
        off = HALO - (LRU_CONV - 1) + kk
        a_in = a_in + cw_ref[kk:kk + 1, :] * ax_s[off:off + T, :]
    gates = _dot(a_in.astype(BF16), wg_ref[...]) + bg_ref[...]
    r_gate = jax.nn.sigmoid(gates[:, :GW])
    i_gate = jax.nn.sigmoid(gates[:, GW:])
    log_a = (-LRU_C) * r_gate * jax.nn.softplus(-lam_ref[...])
    a = jnp.exp(log_a)
    mult = jnp.sqrt(1.0 - a * a)
    u = mult * i_gate * a_in
    SUB = V7X_SUBLANES
    sa_s[SUB:SUB + T, :] = a
    su_s[SUB:SUB + T, :] = u
    a = sa_s[...]
    u = su_s[...]
    d = 1
    while d < SUB:
        ta_s[SUB:, :] = a
        tu_s[SUB:, :] = u
        a_sh = ta_s[SUB - d:SUB - d + SUB + T, :]
        u_sh = tu_s[SUB - d:SUB - d + SUB + T, :]
        u = a * u_sh + u
        a = a * a_sh
        d *= 2
    sa_s[0:SUB, :] = sa_s[T:T + SUB, :]
    su_s[0:SUB, :] = su_s[T:T + SUB, :]
    h_grp = lruh_s[...]
    groups = []
    for gi in range(1, T // SUB + 1):
        h_grp = a[gi * SUB:(gi + 1) * SUB, :] * h_grp + u[gi * SUB:(gi + 1) * SUB, :]
        groups.append(h_grp)
    lruh_s[...] = h_grp
    h_lru = jnp.concatenate(groups, axis=0)
    y_a = h_lru * jax.nn.gelu(sl(1))
    y_s[:, 0:GW] = y_a.astype(BF16)
    ax_s[0:HALO, :] = ax_s[T:T + HALO, :]

    p = sl(6)
    p_s[HALO:HALO + T, :] = p
    lane = lax.broadcasted_iota(I32, (1, GW), 1)
    win = jnp.where(lane < HEAD_DIM, float(POOL_WINDOWS[0]),
                    jnp.where(lane < 2 * HEAD_DIM, float(POOL_WINDOWS[1]),
                              jnp.where(lane < 3 * HEAD_DIM, float(POOL_WINDOWS[2]),
                                        float(POOL_WINDOWS[3]))))
    sums = []
    acc = p_s[...]
    d = 1
    while d < max(POOL_WINDOWS):
        tp_s[HALO:, :] = acc
        acc = acc + tp_s[HALO - d:HALO - d + HALO + T, :]
        sums.append(acc[HALO:, :])
        d *= 2
    wsum = jnp.where(lane < HEAD_DIM, sums[0],
                     jnp.where(lane < 2 * HEAD_DIM, sums[1],
                               jnp.where(lane < 3 * HEAD_DIM, sums[2], sums[3])))
    pos =(ti * T + 1 + lax.broadcasted_iota(I32, (T, 1), 0)).astype(F32)
    dpool = wsum / jnp.minimum(pos, win) - p
    y_c = _dot(dpool.astype(BF16), wp_ref[...]) * ps_ref[...]
    y_s[:, 2 * GW:3 * GW] = y_c.astype(BF16)
    p_s[0:HALO, :] = p_s[T:T + HALO, :]

    sc_s[HALO:HALO + T, :] = sl(8) * sl(9)
    conv = jnp.zeros((T, GW), F32)
    for kk in range(SCONV_WIDTH):
        off = HALO - (SCONV_WIDTH - 1) + kk
        conv = conv + sw_ref[kk:kk + 1, :] * sc_s[off:off + T, :]
    y_s[:, 3 * GW:4 * GW] = (sl(7) * conv).astype(BF16)
    sc_s[0:HALO, :] = sc_s[T:T + HALO, :]

    lbraw = lbraw_ref[...]
    e_lb = jnp.exp(lbraw - jnp.max(lbraw, axis=0, keepdims=True))
    p_lb = e_lb / jnp.sum(e_lb, axis=0, keepdims=True)
    lb = jnp.zeros((1, GW), F32)
    for li in range(1, layer + 1):
        lb = lb + p_lb[li:li + 1, :]
    z = sl(3)
    lf_s[...] = jnp.log(lb + (1.0 - lb) * jax.nn.sigmoid(z))
    k_s[...] = (1.0 - lb) * jax.nn.sigmoid(-z)
    q_s[...] = jax.nn.silu(sl(2))

    L = HGRN_CHUNK
    ones_bd = ones_ref[...]
    ones_f = ones_bd.astype(F32)
    tri = tri_ref[...]
    widths = _level_widths()
    row_l = lax.broadcasted_iota(I32, (L, 1), 0)
    row_d = lax.broadcasted_iota(I32, (DIAG, 1), 0)

    def chunk_body(c, carry):
        r0 = c * L
        lf = lf_s[pl.ds(r0, L), :]
        hi, mid, lo = _split3(lf)
        G = _dot(tri, hi) + _dot(tri, mid) + _dot(tri, lo)
        q = q_s[pl.ds(r0, L), :]
        k = k_s[pl.ds(r0, L), :]
        v = proj_s[pl.ds(r0, L), 4 * GW:5 * GW]
        vb = v.astype(BF16)
        g_last = G[L - 1:L, :]
        st = st_s[...]

        o = _dot_nt((q * jnp.exp(G)).astype(BF16), st.astype(BF16))

        scores = jnp.zeros((N_GROUP_HEADS * L, L), F32)
        for li, w in enumerate(widths):
            gref = G[w - 1:w, :]
            for pair in range(1, L // (2 * w)):
                lastrow = pair * 2 * w + w - 1
                gref = jnp.where(row_l < pair * 2 * w, gref, G[lastrow:lastrow + 1, :])
            qt = q * jnp.exp(jnp.minimum(G - gref, 0.0))
            kt = k * jnp.exp(jnp.minimum(gref - G, 0.0))
            q4 = jnp.concatenate([qt.astype(BF16)] * N_GROUP_HEADS, axis=0) * ones_bd
            s4 = _dot_nt(q4, kt.astype(BF16))
            scores = scores + s4 * lvl_ref[li]
        o4 = _dot(scores.astype(BF16), vb) * ones_f
        for hh in range(N_GROUP_HEADS):
            o = o + o4[hh * L:(hh + 1) * L, :]

        o_diag = []
        for b in range(L // DIAG):
            gs = G[b * DIAG:(b + 1) * DIAG, :]
            qs = q[b * DIAG:(b + 1) * DIAG, :]
            ks = k[b * DIAG:(b + 1) * DIAG, :]
            vs = v[b * DIAG:(b + 1) * DIAG, :]
            terms = []
            for j in range(DIAG):
                dg = jnp.where(row_d >= j, gs - gs[j:j + 1, :], -jnp.inf)
                terms.append((qs * (ks[j:j + 1, :] * jnp.exp(dg))).astype(BF16))
            e = jnp.concatenate(terms, axis=0)
            pd = _dot(e, ones_bd)
            od = jnp.zeros((DIAG, GW), F32)
            for j in range(DIAG):
                od = od + pd[j * DIAG:(j + 1) * DIAG, :] * vs[j:j + 1, :]
            o_diag.append(od)
        o = o + jnp.concatenate(o_diag, axis=0)
        o_s[pl.ds(r0, L), :] = o

        kd = (k * jnp.exp(g_last - G)).astype(BF16)
        st_s[...] = st * jnp.exp(g_last) + _dot_tn(vb, kd) * ones_f
        return carry

    for c in range(T // L):
        chunk_body(c, 0)

    o = o_s[...]
    ms = _dot((o * o).astype(BF16), ones_bd) * (1.0 / HEAD_DIM)
    o = o * lax.rsqrt(ms + HEAD_NORM_EPS) * hg_ref[...]
    y_s[:, GW:2 * GW] = (o * jax.nn.silu(sl(5))).astype(BF16)

    return x + _dot(y_s[...], w_out_ref[...])


def _mix_kernel(layer, h_ref, *refs):
    params = refs[:N_MIX_PARAMS]
    out_ref = refs[N_MIX_PARAMS]
    scratch = refs[N_MIX_PARAMS + 1:]
    _mix_reset(pl.program_id(1), scratch)
    out_ref[...] = _mix_tile(layer, h_ref[...], pl.program_id(1), params, scratch)


def _level_widths():
    widths = []
    w = DIAG
    while w < HGRN_CHUNK:
        widths.append(w)
        w *= 2
    return widths


def _level_masks():
    L = HGRN_CHUNK
    t = jnp.arange(N_GROUP_HEADS * L)[:, None] % L
    s = jnp.arange(L)[None, :]
    return jnp.stack([(((t // w) % 2 == 1) & (s // w == t // w - 1)).astype(F32)
                      for w in _level_widths()])


def _block_diag(w):
    return jax.scipy.linalg.block_diag(*[w[i] for i in range(w.shape[0])])


def _const_spec(shape):
    nd = len(shape)
    return pl.BlockSpec(shape, lambda *_: (0,) * nd)


def _mix_params(gmix, w_in, w_out, conv_w, conv_b, w_a, b_a, w_x, b_x, lam, lb_raw, norm_g,
                pool_w, pool_scale, sconv_w):
    GW = GROUP_WIDTH
    assert w_in.shape[1] == N_IN_SLICES * GW and POOL_WINDOWS == (2, 4, 8, 16)
    wg = jnp.concatenate([_block_diag(w_a), _block_diag(w_x)], axis=1).astype(BF16)
    bg = jnp.concatenate([b_a, b_x])[None, :]
    head_of = jnp.arange(GW) // HEAD_DIM
    ones_bd = (head_of[:, None] == head_of[None, :]).astype(BF16)
    tri = jnp.tril(jnp.ones((HGRN_CHUNK, HGRN_CHUNK), BF16))
    params = [gmix[None, :], w_in.astype(BF16), w_out.astype(BF16), conv_w, conv_b[None, :], wg, bg,
              lam[None, :], lb_raw, jnp.tile(norm_g, N_GROUP_HEADS)[None, :],
              _block_diag(pool_w).astype(BF16), pool_scale[None, :], sconv_w, ones_bd, tri,
              _level_masks()]
    assert len(params) == N_MIX_PARAMS
    return params


def _mix_scratch(T):
    GW = GROUP_WIDTH
    scratch = [
        pltpu.VMEM((T, N_IN_SLICES * GW), F32),
        pltpu.VMEM((HALO + T, GW), F32),
        pltpu.VMEM((HALO + T, GW), F32),
        pltpu.VMEM((HALO + T, GW), F32),
        pltpu.VMEM((V7X_SUBLANES + T, GW), F32),
        pltpu.VMEM((V7X_SUBLANES + T, GW), F32),
        pltpu.VMEM((2 * V7X_SUBLANES + T, GW), F32),
        pltpu.VMEM((2 * V7X_SUBLANES + T, GW), F32),
        pltpu.VMEM((2 * HALO + T, GW), F32),
        pltpu.VMEM((V7X_SUBLANES, GW), F32),
        pltpu.VMEM((GW, GW), F32),
        pltpu.VMEM((T, GW), F32),
        pltpu.VMEM((T, GW), F32),
        pltpu.VMEM((T, GW), F32),
        pltpu.VMEM((T, GW), F32),
        pltpu.VMEM((T, N_GROUP_HEADS * GW), BF16),
    ]
    assert len(scratch) == N_MIX_SCRATCH
    return scratch


def _mix_layer(layer, h, params):
    B, S, D = h.shape
    T = min(MIX_TILE, S)
    assert S % T == 0 and T % HGRN_CHUNK == 0
    in_specs = [pl.BlockSpec((None, T, D), lambda b, t: (b, t, 0))]
    in_specs += [_const_spec(a.shape) for a in params]
    return pl.pallas_call(
        functools.partial(_mix_kernel, layer),
        out_shape=jax.ShapeDtypeStruct((B, S, D), F32),
        grid=(B, S // T),
        in_specs=in_specs,
        out_specs=pl.BlockSpec((None, T, D), lambda b, t: (b, t, 0)),
        scratch_shapes=_mix_scratch(T),
        compiler_params=pltpu.CompilerParams(
            dimension_semantics=("arbitrary", "arbitrary"),
            vmem_limit_bytes=48 * 1024 * 1024),
        name=f"mix{layer}",
    )(h, *params)


def _ffn_tile(x, g_ref, wg_ref, wu_ref, wd_ref, act_s):
    hn = _rms_norm(x, g_ref[...]).astype(BF16)
    ff = wg_ref.shape[1]
    for c in range(0, ff, FFN_CHUNK):
        gate = _dot(hn, wg_ref[:, c:c + FFN_CHUNK])
        up = _dot(hn, wu_ref[:, c:c + FFN_CHUNK])
        act_s[:, c:c + FFN_CHUNK] = (jax.nn.silu(gate) * up).astype(BF16)
    return x + _dot(act_s[...], wd_ref[...])


def _ffn_kernel(final, h_ref, g_ref, gf_ref, wg_ref, wu_ref, wd_ref, out_ref, act_s):
    y = _ffn_tile(h_ref[...], g_ref, wg_ref, wu_ref, wd_ref, act_s)
    out_ref[...] = _rms_norm(y, gf_ref[...]) if final else y


def _ffn_layer(h2, g, w_gate, w_up, w_down, g_final, final):
    n, d = h2.shape
    ff = w_gate.shape[1]
    tm = min(FFN_TILE, n)
    assert n % tm == 0 and ff % FFN_CHUNK == 0
    once = pl.Buffered(1)
    return pl.pallas_call(
        functools.partial(_ffn_kernel, final),
        out_shape=jax.ShapeDtypeStruct((n, d), F32),
        grid=(n // tm,),
        in_specs=[pl.BlockSpec((tm, d), lambda i: (i, 0)),
                  _const_spec((1, d)),
                  _const_spec((1, d)),
                  pl.BlockSpec((d, ff), lambda i: (0, 0), pipeline_mode=once),
                  pl.BlockSpec((d, ff), lambda i: (0, 0), pipeline_mode=once),
                  pl.BlockSpec((ff, d), lambda i: (0, 0), pipeline_mode=once)],
        out_specs=pl.BlockSpec((tm, d), lambda i: (i, 0)),
        scratch_shapes=[pltpu.VMEM((tm, ff), BF16)],
        compiler_params=pltpu.CompilerParams(
            dimension_semantics=("arbitrary",),
            vmem_limit_bytes=48 * 1024 * 1024),
        name="ffn",
    )(h2, g[None, :], g_final[None, :], w_gate.astype(BF16), w_up.astype(BF16), w_down.astype(BF16))


def _router_kernel(h_ref, g_ref, wr_ref, triu_ref, hn_ref, mi_ref, mf_ref, cnt_ref):
    n_e = wr_ref.shape[0]
    tr = h_ref.shape[0]
    hn = _rms_norm(h_ref[...], g_ref[...])
    xh = hn.astype(BF16)
    hn_ref[...] = xh
    xm = (hn - xh.astype(F32)).astype(BF16)
    w = wr_ref[...]
    wh = w.astype(BF16)
    wm = (w - wh.astype(F32)).astype(BF16)
    logits = _dot_nt(wh, xh) + _dot_nt(wh, xm) + _dot_nt(wm, xh)
    eid = lax.broadcasted_iota(I32, (n_e, tr), 0)
    m1 = jnp.max(logits, axis=0, keepdims=True)
    e1 = jnp.min(jnp.where(logits == m1, eid, n_e), axis=0, keepdims=True)
    rest = jnp.where(eid == e1, -jnp.inf, logits)
    m2 = jnp.max(rest, axis=0, keepdims=True)
    e2 = jnp.min(jnp.where(rest == m2, eid, n_e), axis=0, keepdims=True)
    ex = jnp.exp(m2 - m1)
    g1 = 1.0 / (1.0 + ex)
    g2 = ex / (1.0 + ex)
    member = jnp.where((eid == e1) | (eid == e2), 1.0, 0.0)
    incl = _dot(member.astype(BF16), triu_ref[...])
    rank = incl - member
    r1 = jnp.sum(jnp.where(eid == e1, rank, 0.0), axis=0, keepdims=True)
    r2 = jnp.sum(jnp.where(eid == e2, rank, 0.0), axis=0, keepdims=True)
    cnt_ref[...] = jnp.broadcast_to(incl[:, tr - 1:tr], cnt_ref.shape).astype(I32)
    zi = jnp.zeros((1, tr), I32)
    mi_ref[...] = jnp.concatenate(
        [e1, e2, r1.astype(I32), r2.astype(I32), zi, zi, zi, zi], axis=0)
    zf = jnp.zeros((1, tr), F32)
    mf_ref[...] = jnp.concatenate([g1, g2, zf, zf, zf, zf, zf, zf], axis=0)


def _router(h2, g, w_router):
    n, d = h2.shape
    n_e = w_router.shape[1]
    tr = min(MOE_TILE, n)
    assert n % tr == 0 and n_e == V7X_SUBLANES
    tok = jnp.arange(tr)
    triu = (tok[:, None] <= tok[None, :]).astype(BF16)
    return pl.pallas_call(
        _router_kernel,
        out_shape=(jax.ShapeDtypeStruct((n, d), BF16),
                   jax.ShapeDtypeStruct((V7X_SUBLANES, n), I32),
                   jax.ShapeDtypeStruct((V7X_SUBLANES, n), F32),
                   jax.ShapeDtypeStruct((n // tr, n_e, V7X_LANES), I32)),
        grid=(n // tr,),
        in_specs=[pl.BlockSpec((tr, d), lambda i: (i, 0)),
                  _const_spec((1, d)),
                  _const_spec((n_e, d)),
                  _const_spec((tr, tr))],
        out_specs=(pl.BlockSpec((tr, d), lambda i: (i, 0)),
                   pl.BlockSpec((V7X_SUBLANES, tr), lambda i: (0, i)),
                   pl.BlockSpec((V7X_SUBLANES, tr), lambda i: (0, i)),
                   pl.BlockSpec((None, n_e, V7X_LANES), lambda i: (i, 0, 0))),
        compiler_params=pltpu.CompilerParams(dimension_semantics=("arbitrary",)),
        name="router",
    )(h2, g[None, :], w_router.T, triu)


def _segment_copies(step, slot, tables, n_e, tile, local_ref, rows_ref, sem, to_rows, start):
    off_ref, row_ref, pad_ref = tables

    def copy(off, row, size):
        local = local_ref.at[slot, pl.ds(pl.multiple_of(off, ROW_PAD), size)]
        remote = rows_ref.at[pl.ds(pl.multiple_of(row, ROW_PAD), size)]
        return (pltpu.make_async_copy(local, remote, sem.at[slot]) if to_rows
                else pltpu.make_async_copy(remote, local, sem.at[slot]))

    if not start:
        last = step * n_e + n_e - 1
        total = off_ref[last] + pad_ref[last]
        size = TOP_K * tile
        while size >= ROW_PAD:
            pl.when((total & size) != 0)(copy(0, 0, size).wait)
            size //= 2
        return
    for e in range(n_e):
        off = off_ref[step * n_e + e]
        row = row_ref[step * n_e + e]
        pad = pad_ref[step * n_e + e]
        size = tile
        while size >= ROW_PAD:
            done = pad & (-2 * size)
            pl.when((pad & size) != 0)(copy(off + done, row + done, size).start)
            size //= 2


def _stack_rows(mi, off_ref, step, n_e):
    e1, e2, row1, row2 = mi[0:1, :], mi[1:2, :], mi[2:3, :], mi[3:4, :]
    for e in range(n_e):
        off = off_ref[step * n_e + e]
        row1 = row1 + jnp.where(e1 == e, off, 0)
        row2 = row2 + jnp.where(e2 == e, off, 0)
    return row1, row2


def _fill_copies(fill_ref, n_e, zero_s, xs_ref, sem, start):
    zrows = zero_s.shape[0]
    rb = EXPERT_ROWS

    def go(cp):
        if start:
            cp.start()
        else:
            cp.wait()

    def zero_copy(row, size):
        return pltpu.make_async_copy(
            zero_s.at[pl.ds(0, size)], xs_ref.at[pl.ds(pl.multiple_of(row, ROW_PAD), size)], sem)

    for e in range(n_e):
        row = fill_ref[e]
        gap = fill_ref[n_e + e]
        size = zrows
        while size >= ROW_PAD:
            done = gap & (-2 * size)
            pl.when((gap & size) != 0)(functools.partial(go, zero_copy(row + done, size)))
            size //= 2

    def block(b, c):
        done = 0
        size = zrows
        while done < rb:
            if size <= rb - done:
                go(zero_copy(b * rb + done, size))
                done += size
            else:
                size //= 2
        return c

    lax.fori_loop(fill_ref[2 * n_e], xs_ref.shape[0] // rb, block, 0)


def _dispatch_kernel(nt, off_ref, row_ref, pad_ref, fill_ref, hn_ref, mi_ref, xs_ref, slab_s, zero_s,
                     sem, fill_sem):
    i = pl.program_id(0)
    tile = hn_ref.shape[0]
    n_e = V7X_SUBLANES
    r_stack = slab_s.shape[1]
    slot = lax.rem(i, 2)
    copies = functools.partial(_segment_copies, tables=(off_ref, row_ref, pad_ref), n_e=n_e,
                               tile=tile, local_ref=slab_s, rows_ref=xs_ref, sem=sem, to_rows=True)

    @pl.when(i == 0)
    def _():
        zero_s[...] = jnp.zeros(zero_s.shape, F32)
        _fill_copies(fill_ref, n_e, zero_s, xs_ref, fill_sem, start=True)

    @pl.when(i >= 2)
    def _():
        copies(i - 2, slot, start=False)

    row1, row2 = _stack_rows(mi_ref[...], off_ref, i, n_e)
    rid = lax.broadcasted_iota(I32, (r_stack, tile), 0)
    sel = jnp.where(rid == row1, 1.0, jnp.where(rid == row2, 1.0, 0.0)).astype(BF16)
    slab_s[slot] = _dot(sel, hn_ref[...])
    copies(i, slot, start=True)

    @pl.when(i == nt - 1)
    def _():
        if nt > 1:
            copies(i - 1, 1 - slot, start=False)
        copies(i, slot, start=False)
        _fill_copies(fill_ref, n_e, zero_s, xs_ref, fill_sem, start=False)


def _dispatch(hn, mi, tables, fill, n_rows):
    n, d = hn.shape
    tile = min(MOE_TILE, n)
    nt = n // tile
    r_stack = TOP_K * tile + V7X_SUBLANES * ROW_PAD
    return pl.pallas_call(
        functools.partial(_dispatch_kernel, nt),
        out_shape=jax.ShapeDtypeStruct((n_rows, d), F32),
        grid_spec=pltpu.PrefetchScalarGridSpec(
            num_scalar_prefetch=4,
            grid=(nt,),
            in_specs=[pl.BlockSpec((tile, d), lambda i, *_: (i, 0)),
                      pl.BlockSpec((V7X_SUBLANES, tile), lambda i, *_: (0, i))],
            out_specs=pl.BlockSpec(memory_space=pl.ANY),
            scratch_shapes=[pltpu.VMEM((2, r_stack, d), F32),
                            pltpu.VMEM((pl.next_power_of_2(EXPERT_ROWS) // 2, d), F32),
                            pltpu.SemaphoreType.DMA((2,)), pltpu.SemaphoreType.DMA(())]),
        compiler_params=pltpu.CompilerParams(dimension_semantics=("arbitrary",),
                                             has_side_effects=True),
        name="dispatch",
    )(*tables, fill, hn, mi)


def _expert_kernel(be_ref, rows_ref, src_ref, x_ref, w1_ref, w3_ref, w2_ref, out_ref,
                   w1_s, w3_s, w2_s):
    del be_ref, src_ref
    b = pl.program_id(0)
    f = pl.program_id(1)
    rows = rows_ref[b]

    @pl.when(f == 0)
    def _():
        out_ref[...] = jnp.zeros(out_ref.shape, F32)

    def cast_weights():
        w1_s[...] = w1_ref[...].astype(BF16)
        w3_s[...] = w3_ref[...].astype(BF16)
        w2_s[...] = w2_ref[...].astype(BF16)

    def swiglu_rows(r0, n_rows):
        xb = x_ref[pl.ds(r0, n_rows), :].astype(BF16)
        gate = _dot(xb, w1_s[...])
        up = _dot(xb, w3_s[...])
        act = (jax.nn.silu(gate) * up).astype(BF16)
        out_ref[pl.ds(r0, n_rows), :] += _dot(act, w2_s[...])

    @pl.when(rows == EXPERT_ROWS)
    def _():
        cast_weights()
        for s in range(EXPERT_ROWS // EXPERT_FULL):
            swiglu_rows(s * EXPERT_FULL, EXPERT_FULL)

    @pl.when((rows > 0) & (rows < EXPERT_ROWS))
    def _():
        n_sub = (rows + EXPERT_SUB - 1) // EXPERT_SUB
        n_pair = n_sub // 2

        @pl.when(n_pair >= 2)
        def _():
            cast_weights()
            swiglu_rows(0, EXPERT_FULL)
            swiglu_rows(EXPERT_FULL, EXPERT_FULL)

        @pl.when(n_pair < 2)
        def _():
            cast_weights()

            @pl.when(n_pair == 1)
            def _():
                swiglu_rows(0, EXPERT_FULL)

        def pair(s, c):
            swiglu_rows(pl.multiple_of(s * EXPERT_FULL, EXPERT_FULL), EXPERT_FULL)
            return c

        lax.fori_loop(2, n_pair, pair, 0)

        @pl.when(n_sub % 2 == 1)
        def _():
            swiglu_rows(pl.multiple_of((n_sub - 1) * EXPERT_SUB, EXPERT_SUB), EXPERT_SUB)


def _experts(xs, blk_e, blk_rows, blk_src, w1, w3, w2):
    n_rows, d = xs.shape
    n_e, _, ff = w1.shape
    rb = EXPERT_ROWS
    fft = min(EXPERT_FF, ff)
    assert n_rows % rb == 0 and ff % fft == 0
    assert EXPERT_FULL == 2 * EXPERT_SUB and rb % EXPERT_FULL == 0
    nf = ff // fft
    nb = n_rows // rb

    def f_eff(b, f, rows):
        return jnp.where(rows[b] > 0, f, nf - 1)

    return pl.pallas_call(
        _expert_kernel,
        out_shape=jax.ShapeDtypeStruct((n_rows, d), F32),
        grid_spec=pltpu.PrefetchScalarGridSpec(
            num_scalar_prefetch=3,
            grid=(nb, nf),
            in_specs=[
                pl.BlockSpec((rb, d), lambda b, f, be, rows, src: (src[b], 0)),
                pl.BlockSpec((None, d, fft), lambda b, f, be, rows, src: (be[b], 0, f_eff(b, f, rows))),
                pl.BlockSpec((None, d, fft), lambda b, f, be, rows, src: (be[b], 0, f_eff(b, f, rows))),
                pl.BlockSpec((None, fft, d), lambda b, f, be, rows, src: (be[b], f_eff(b, f, rows), 0)),
            ],
            out_specs=pl.BlockSpec((rb, d), lambda b, f, be, rows, src: (b, 0)),
            scratch_shapes=[pltpu.VMEM((d, fft), BF16), pltpu.VMEM((d, fft), BF16),
                            pltpu.VMEM((fft, d), BF16)]),
        compiler_params=pltpu.CompilerParams(
            dimension_semantics=("arbitrary", "arbitrary"),
            vmem_limit_bytes=EXPERT_VMEM_BYTES),
        name="experts",
    )(blk_e, blk_rows, blk_src, xs, w1, w3, w2)


def _combine_kernel(final, nt, off_ref, row_ref, pad_ref, h_ref, mi_ref, gt_ref, g_ref, ys_ref,
                    out_ref, stack_s, sem):
    i = pl.program_id(0)
    tile = h_ref.shape[0]
    n_e = V7X_SUBLANES
    r_stack = stack_s.shape[1]
    slot = lax.rem(i, 2)
    copies = functools.partial(_segment_copies, tables=(off_ref, row_ref, pad_ref), n_e=n_e,
                               tile=tile, local_ref=stack_s, rows_ref=ys_ref, sem=sem, to_rows=False)

    @pl.when(i == 0)
    def _():
        stack_s[...] = jnp.zeros(stack_s.shape, F32)
        copies(i, slot, start=True)

    @pl.when(i + 1 < nt)
    def _():
        copies(i + 1, 1 - slot, start=True)

    copies(i, slot, start=False)
    y_rows = stack_s[slot].astype(BF16)
    row1, row2 = _stack_rows(mi_ref[...], off_ref, i, n_e)
    rid = lax.broadcasted_iota(I32, (r_stack, TOP_K * tile), 0)
    rows12 = jnp.concatenate([row1, row2], axis=1)
    y12 = _dot_tn(jnp.where(rid == rows12, 1.0, 0.0).astype(BF16), y_rows)
    gates = gt_ref[...]
    y = h_ref[...] + y12[:tile] * gates[:, 0:1] + y12[tile:] * gates[:, 1:2]
    out_ref[...] = _rms_norm(y, g_ref[...]) if final else y


def _combine(h2, mi, gates, tables, ys, g_final, final):
    n, d = h2.shape
    tile = min(MOE_TILE, n)
    nt = n // tile
    r_stack = TOP_K * tile + V7X_SUBLANES * ROW_PAD
    return pl.pallas_call(
        functools.partial(_combine_kernel, final, nt),
        out_shape=jax.ShapeDtypeStruct((n, d), F32),
        grid_spec=pltpu.PrefetchScalarGridSpec(
            num_scalar_prefetch=3,
            grid=(nt,),
            in_specs=[pl.BlockSpec((tile, d), lambda i, *_: (i, 0)),
                      pl.BlockSpec((V7X_SUBLANES, tile), lambda i, *_: (0, i)),
                      pl.BlockSpec((tile, TOP_K), lambda i, *_: (i, 0)),
                      pl.BlockSpec((1, d), lambda i, *_: (0, 0)),
                      pl.BlockSpec(memory_space=pl.ANY)],
            out_specs=pl.BlockSpec((tile, d), lambda i, *_: (i, 0)),
            scratch_shapes=[pltpu.VMEM((2, r_stack, d), F32), pltpu.SemaphoreType.DMA((2,))]),
        compiler_params=pltpu.CompilerParams(dimension_semantics=("arbitrary",)),
        name="combine",
    )(*tables, h2, mi, gates, g_final[None, :], ys)


def _moe_layer(h2, g, w_router, w1, w3, w2, g_final, final=True):
    n, d = h2.shape
    n_e = w_router.shape[1]
    rb = EXPERT_ROWS
    tile = min(MOE_TILE, n)
    nt = n // tile
    hn, mi, mf, cnt = _router(h2, g, w_router)
    pad = (cnt[:, :, 0] + ROW_PAD - 1) // ROW_PAD * ROW_PAD
    off = jnp.cumsum(pad, axis=1) - pad
    total = jnp.sum(pad, axis=0)
    padded = (total + rb - 1) // rb * rb
    pend = jnp.cumsum(padded)
    pstart = pend - padded
    row = pstart[None, :] + jnp.cumsum(pad, axis=0) - pad
    tables = tuple(t.reshape(-1).astype(I32) for t in (off, row, pad))
    nb = (n * TOP_K + nt * n_e * (ROW_PAD - 1)) // rb + n_e
    blk = jnp.arange(nb, dtype=I32)
    n_used = pend[-1] // rb
    last = jnp.maximum(n_used - 1, 0)
    blk_src = jnp.minimum(blk, last)
    blk_e = jnp.minimum(jnp.sum(blk_src[:, None] * rb >= pend[None, :], axis=1), n_e - 1).astype(I32)
    blk_rows = jnp.where(blk < n_used,
                         jnp.clip(total[blk_e] - (blk * rb - pstart[blk_e]), 0, rb), 0).astype(I32)
    fill = jnp.concatenate([pstart + total, padded - total, n_used[None]]).astype(I32)
    xs = _dispatch(hn, mi, tables, fill, nb * rb)
    ys = _experts(xs, blk_e, blk_rows, blk_src, w1, w3, w2)
    return _combine(h2, mi, mf[:TOP_K].T, tables, ys, g_final, final)


def kernel(x, w_in, w_out, lru_conv_w, lru_conv_b, lru_w_a, lru_b_a, lru_w_x, lru_b_x, lru_lambda,
           hgrn_lower_bounds, hgrn_norm_g, pool_w, pool_scale, sconv_w, norm_mix_g, norm_ffn_g,
           ffn_w_gate, ffn_w_up, ffn_w_down, moe_w_router, moe_w1, moe_w3, moe_w2, final_norm_g):
    B, S, D = x.shape
    depth = w_in.shape[0]

    def mix_params(layer):
        return _mix_params(norm_mix_g[layer], w_in[layer], w_out[layer], lru_conv_w[layer],
                           lru_conv_b[layer], lru_w_a[layer], lru_b_a[layer], lru_w_x[layer],
                           lru_b_x[layer], lru_lambda[layer], hgrn_lower_bounds, hgrn_norm_g[layer],
                           pool_w[layer], pool_scale[layer], sconv_w[layer])

    def ffn_layer(h2, layer):
        j = layer // 2
        final = layer == depth - 1
        if layer % 2 == 0:
            return _ffn_layer(h2, norm_ffn_g[layer], ffn_w_gate[j], ffn_w_up[j], ffn_w_down[j],
                              final_norm_g, final)
        return _moe_layer(h2, norm_ffn_g[layer], moe_w_router[j], moe_w1[j], moe_w3[j], moe_w2[j],
                          final_norm_g, final)

    h = x
    for layer in range(depth):
        h = _mix_layer(layer, h, mix_params(layer))
        h = ffn_layer(h.reshape(B * S, D), layer).reshape(B, S, D)
    return h
```

```python
import functools

import jax
import jax.numpy as jnp
from jax import lax
from jax.experimental import pallas as pl
from jax.experimental.pallas import tpu as pltpu

F32 = jnp.float32
BF16 = jnp.bfloat16
I32 = jnp.int32

GROUP_WIDTH = 256
N_GROUP_HEADS = 4
HEAD_DIM = 64
N_IN_SLICES = 10
LRU_CONV = 4
LRU_C = 8.0
HGRN_CHUNK = 64
POOL_WINDOWS = (2, 4, 8, 16)
SCONV_WIDTH = 3
TOP_K = 2
NORM_EPS = 1e-6
HEAD_NORM_EPS = 1e-5
LOG2_E = 1.4426950408889634

V7X_LANES = 128
V7X_SUBLANES = 8
V7X_VMEM_BYTES = 64 * 1024 * 1024

MIX_TILE = 512
HALO = 16
DIAG = 8
FFN_TILE = 1024
FFN_CHUNK = 256
MOE_TILE = 256
ROW_PAD = V7X_SUBLANES
EXPERT_ROWS = 2560
EXPERT_VMEM_BYTES = 60 * 1024 * 1024
EXPERT_SUB = 256
EXPERT_FULL = 512
EXPERT_FF = 512


def _dot(a, b):
    return jnp.dot(a, b, preferred_element_type=F32)


def _dot_nt(a, b):
    return lax.dot_general(a, b, (((1,), (1,)), ((), ())), preferred_element_type=F32)


def _dot_tn(a, b):
    return lax.dot_general(a, b, (((0,), (0,)), ((), ())), preferred_element_type=F32)


def _rms_norm(x, g):
    ms = jnp.mean(x * x, axis=-1, keepdims=True)
    return x * lax.rsqrt(ms + NORM_EPS) * g


def _split3(x):
    hi = x.astype(BF16)
    r1 = x - hi.astype(F32)
    mid = r1.astype(BF16)
    lo = (r1 - mid.astype(F32)).astype(BF16)
    return hi, mid, lo


N_MIX_PARAMS = 16
N_MIX_SCRATCH = 16


def _mix_reset(ti, scratch):
    (proj_s, ax_s, p_s, sc_s, sa_s, su_s, ta_s, tu_s, tp_s, lruh_s, st_s, q_s, k_s, lf_s, o_s,
     y_s) = scratch
    GW = GROUP_WIDTH

    @pl.when(ti == 0)
    def _():
        ax_s[0:HALO, :] = jnp.zeros((HALO, GW), F32)
        p_s[0:HALO, :] = jnp.zeros((HALO, GW), F32)
        sc_s[0:HALO, :] = jnp.zeros((HALO, GW), F32)
        sa_s[0:V7X_SUBLANES, :] = jnp.ones((V7X_SUBLANES, GW), F32)
        su_s[0:V7X_SUBLANES, :] = jnp.zeros((V7X_SUBLANES, GW), F32)
        ta_s[0:V7X_SUBLANES, :] = jnp.zeros((V7X_SUBLANES, GW), F32)
        tu_s[0:V7X_SUBLANES, :] = jnp.zeros((V7X_SUBLANES, GW), F32)
        tp_s[0:HALO, :] = jnp.zeros((HALO, GW), F32)
        lruh_s[...] = jnp.zeros(lruh_s.shape, F32)
        st_s[...] = jnp.zeros(st_s.shape, F32)


def _mix_tile(layer, x, ti, params, scratch):
    (gmix_ref, w_in_ref, w_out_ref, cw_ref, cb_ref, wg_ref, bg_ref, lam_ref, lbraw_ref, hg_ref,
     wp_ref, ps_ref, sw_ref, ones_ref, tri_ref, lvl_ref) = params
    (proj_s, ax_s, p_s, sc_s, sa_s, su_s, ta_s, tu_s, tp_s, lruh_s, st_s, q_s, k_s, lf_s, o_s,
     y_s) = scratch
    T = x.shape[0]
    GW = GROUP_WIDTH
    hn = _rms_norm(x, gmix_ref[...]).astype(BF16)
    proj_s[...] = _dot(hn, w_in_ref[...])

    def sl(i):
        return proj_s[:, i * GW:(i + 1) * GW]

    ax_s[HALO:HALO + T, :] = sl(0)
    a_in = cb_ref[...]
    for kk in range(LRU_CONV):
        off = HALO - (LRU_CONV - 1) + kk
        a_in = a_in + cw_ref[kk:kk + 1, :] * ax_s[off:off + T, :]
    gates = _dot(a_in.astype(BF16), wg_ref[...]) + bg_ref[...]
    r_gate = jax.nn.sigmoid(gates[:, :GW])
    i_gate = jax.nn.sigmoid(gates[:, GW:])
    a = jnp.exp2(r_gate * ((-LRU_C * LOG2_E) * jax.nn.softplus(-lam_ref[...])))
    mult = jnp.sqrt(1.0 - a * a)
    u = mult * i_gate * a_in
    SUB = V7X_SUBLANES
    sa_s[SUB:SUB + T, :] = a
    su_s[SUB:SUB + T, :] = u
    a = sa_s[...]
    u = su_s[...]
    d = 1
    while d < SUB:
        ta_s[SUB:, :] = a
        tu_s[SUB:, :] = u
        a_sh = ta_s[SUB - d:SUB - d + SUB + T, :]
        u_sh = tu_s[SUB - d:SUB - d + SUB + T, :]
        u = a * u_sh + u
        a = a * a_sh
        d *= 2
    sa_s[0:SUB, :] = sa_s[T:T + SUB, :]
    su_s[0:SUB, :] = su_s[T:T + SUB, :]
    h_grp = lruh_s[...]
    groups = []
    for gi in range(1, T // SUB + 1):
        h_grp = a[gi * SUB:(gi + 1) * SUB, :] * h_grp + u[gi * SUB:(gi + 1) * SUB, :]
        groups.append(h_grp)
    lruh_s[...] = h_grp
    h_lru = jnp.concatenate(groups, axis=0)
    y_a = h_lru * jax.nn.gelu(sl(1))
    y_s[:, 0:GW] = y_a.astype(BF16)
    ax_s[0:HALO, :] = ax_s[T:T + HALO, :]

    p = sl(6)
    p_s[HALO:HALO + T, :] = p
    lane = lax.broadcasted_iota(I32, (1, GW), 1)
    win = jnp.where(lane < HEAD_DIM, float(POOL_WINDOWS[0]),
                    jnp.where(lane < 2 * HEAD_DIM, float(POOL_WINDOWS[1]),
                              jnp.where(lane < 3 * HEAD_DIM, float(POOL_WINDOWS[2]),
                                        float(POOL_WINDOWS[3]))))
    sums = []
    acc = p_s[...]
    d = 1
    while d < max(POOL_WINDOWS):
        tp_s[HALO:, :] = acc
        acc = acc + tp_s[HALO - d:HALO - d + HALO + T, :]
        sums.append(acc[HALO:, :])
        d *= 2
    wsum = jnp.where(lane < HEAD_DIM, sums[0],
                     jnp.where(lane < 2 * HEAD_DIM, sums[1],
                               jnp.where(lane < 3 * HEAD_DIM, sums[2], sums[3])))
    pos =(ti * T + 1 + lax.broadcasted_iota(I32, (T, 1), 0)).astype(F32)
    dpool = wsum / jnp.minimum(pos, win) - p
    y_c = _dot(dpool.astype(BF16), wp_ref[...]) * ps_ref[...]
    y_s[:, 2 * GW:3 * GW] = y_c.astype(BF16)
    p_s[0:HALO, :] = p_s[T:T + HALO, :]

    sc_s[HALO:HALO + T, :] = sl(8) * sl(9)
    conv = jnp.zeros((T, GW), F32)
    for kk in range(SCONV_WIDTH):
        off = HALO - (SCONV_WIDTH - 1) + kk
        conv = conv + sw_ref[kk:kk + 1, :] * sc_s[off:off + T, :]
    y_s[:, 3 * GW:4 * GW] = (sl(7) * conv).astype(BF16)
    sc_s[0:HALO, :] = sc_s[T:T + HALO, :]

    lbraw = lbraw_ref[...]
    e_lb = jnp.exp(lbraw - jnp.max(lbraw, axis=0, keepdims=True))
    p_lb = e_lb / jnp.sum(e_lb, axis=0, keepdims=True)
    lb = jnp.zeros((1, GW), F32)
    for li in range(1, layer + 1):
        lb = lb + p_lb[li:li + 1, :]
    z = sl(3)
    lf_s[...] = jnp.log2(lb + (1.0 - lb) * jax.nn.sigmoid(z))
    k_s[...] = (1.0 - lb) * jax.nn.sigmoid(-z)
    q_s[...] = jax.nn.silu(sl(2))

    L = HGRN_CHUNK
    ones_bd = ones_ref[...]
    ones_f = ones_bd.astype(F32)
    tri = tri_ref[...]
    widths = _level_widths()
    row_l = lax.broadcasted_iota(I32, (L, 1), 0)
    row_d = lax.broadcasted_iota(I32, (DIAG, 1), 0)

    def chunk_body(c, carry):
        r0 = c * L
        lf = lf_s[pl.ds(r0, L), :]
        hi, mid, lo = _split3(lf)
        G = _dot(tri, hi) + _dot(tri, mid) + _dot(tri, lo)

        def g_row(r):
            return G[r:r + 1, :]

        q =q_s[pl.ds(r0, L), :]
        k = k_s[pl.ds(r0, L), :]
        v = proj_s[pl.ds(r0, L), 4 * GW:5 * GW]
        vb = v.astype(BF16)
        g_last = g_row(L - 1)
        st = st_s[...]

        o = _dot_nt((q * jnp.exp2(G)).astype(BF16), st.astype(BF16))

        scores = jnp.zeros((N_GROUP_HEADS * L, L), F32)
        for li, w in enumerate(widths):
            gref = g_row(w - 1)
            for pair in range(1, L // (2 * w)):
                gref = jnp.where(row_l < pair * 2 * w, gref, g_row(pair * 2 * w + w - 1))
            qt = q * jnp.exp2(jnp.minimum(G - gref, 0.0))
            kt = k * jnp.exp2(jnp.minimum(gref - G, 0.0))
            q4 = jnp.concatenate([qt.astype(BF16)] * N_GROUP_HEADS, axis=0) * ones_bd
            s4 = _dot_nt(q4, kt.astype(BF16))
            scores = scores + s4 * lvl_ref[li]
        o4 = _dot(scores.astype(BF16), vb) * ones_f
        for hh in range(N_GROUP_HEADS):
            o = o + o4[hh * L:(hh + 1) * L, :]

        o_diag = []
        for b in range(L // DIAG):
            gs = G[b * DIAG:(b + 1) * DIAG, :]
            qs = q[b * DIAG:(b + 1) * DIAG, :]
            ks = k[b * DIAG:(b + 1) * DIAG, :]
            vs = v[b * DIAG:(b + 1) * DIAG, :]
            terms = []
            for j in range(DIAG):
                dg = jnp.where(row_d >= j, gs - gs[j:j + 1, :], -jnp.inf)
                terms.append((qs * (ks[j:j + 1, :] * jnp.exp2(dg))).astype(BF16))
            e = jnp.concatenate(terms, axis=0)
            pd = _dot(e, ones_bd)
            od = jnp.zeros((DIAG, GW), F32)
            for j in range(DIAG):
                od = od + pd[j * DIAG:(j + 1) * DIAG, :] * vs[j:j + 1, :]
            o_diag.append(od)
        o = o + jnp.concatenate(o_diag, axis=0)
        o_s[pl.ds(r0, L), :] = o

        kd = (k * jnp.exp2(g_last - G)).astype(BF16)
        st_s[...] = st * jnp.exp2(g_last) + _dot_tn(vb, kd) * ones_f
        return carry

    for c in range(T // L):
        chunk_body(c, 0)

    o = o_s[...]
    ms = _dot((o * o).astype(BF16), ones_bd) * (1.0 / HEAD_DIM)
    o = o * lax.rsqrt(ms + HEAD_NORM_EPS) * hg_ref[...]
    y_s[:, GW:2 * GW] = (o * jax.nn.silu(sl(5))).astype(BF16)

    return x + _dot(y_s[...], w_out_ref[...])


def _mix_kernel(layer, h_ref, *refs):
    params = refs[:N_MIX_PARAMS]
    out_ref = refs[N_MIX_PARAMS]
    scratch = refs[N_MIX_PARAMS + 1:]
    _mix_reset(pl.program_id(1), scratch)
    out_ref[...] = _mix_tile(layer, h_ref[...], pl.program_id(1), params, scratch)


def _level_widths():
    widths = []
    w = DIAG
    while w < HGRN_CHUNK:
        widths.append(w)
        w *= 2
    return widths


def _level_masks():
    L = HGRN_CHUNK
    t = jnp.arange(N_GROUP_HEADS * L)[:, None] % L
    s = jnp.arange(L)[None, :]
    return jnp.stack([(((t // w) % 2 == 1) & (s // w == t // w - 1)).astype(F32)
                      for w in _level_widths()])


def _block_diag(w):
    return jax.scipy.linalg.block_diag(*[w[i] for i in range(w.shape[0])])


def _const_spec(shape):
    nd = len(shape)
    return pl.BlockSpec(shape, lambda *_: (0,) * nd)


def _mix_params(gmix, w_in, w_out, conv_w, conv_b, w_a, b_a, w_x, b_x, lam, lb_raw, norm_g,
                pool_w, pool_scale, sconv_w):
    GW = GROUP_WIDTH
    assert w_in.shape[1] == N_IN_SLICES * GW and POOL_WINDOWS == (2, 4, 8, 16)
    wg = jnp.concatenate([_block_diag(w_a), _block_diag(w_x)], axis=1).astype(BF16)
    bg = jnp.concatenate([b_a, b_x])[None, :]
    head_of = jnp.arange(GW) // HEAD_DIM
    ones_bd = (head_of[:, None] == head_of[None, :]).astype(BF16)
    tri = jnp.tril(jnp.ones((HGRN_CHUNK, HGRN_CHUNK), BF16))
    params = [gmix[None, :], w_in.astype(BF16), w_out.astype(BF16), conv_w, conv_b[None, :], wg, bg,
              lam[None, :], lb_raw, jnp.tile(norm_g, N_GROUP_HEADS)[None, :],
              _block_diag(pool_w).astype(BF16), pool_scale[None, :], sconv_w, ones_bd, tri,
              _level_masks()]
    assert len(params) == N_MIX_PARAMS
    return params


def _mix_scratch(T):
    GW = GROUP_WIDTH
    scratch = [
        pltpu.VMEM((T, N_IN_SLICES * GW), F32),
        pltpu.VMEM((HALO + T, GW), F32),
        pltpu.VMEM((HALO + T, GW), F32),
        pltpu.VMEM((HALO + T, GW), F32),
        pltpu.VMEM((V7X_SUBLANES + T, GW), F32),
        pltpu.VMEM((V7X_SUBLANES + T, GW), F32),
        pltpu.VMEM((2 * V7X_SUBLANES + T, GW), F32),
        pltpu.VMEM((2 * V7X_SUBLANES + T, GW), F32),
        pltpu.VMEM((2 * HALO + T, GW), F32),
        pltpu.VMEM((V7X_SUBLANES, GW), F32),
        pltpu.VMEM((GW, GW), F32),
        pltpu.VMEM((T, GW), F32),
        pltpu.VMEM((T, GW), F32),
        pltpu.VMEM((T, GW), F32),
        pltpu.VMEM((T, GW), F32),
        pltpu.VMEM((T, N_GROUP_HEADS * GW), BF16),
    ]
    assert len(scratch) == N_MIX_SCRATCH
    return scratch


def _mix_layer(layer, h, params):
    B, S, D = h.shape
    T = min(MIX_TILE, S)
    assert S % T == 0 and T % HGRN_CHUNK == 0
    in_specs = [pl.BlockSpec((None, T, D), lambda b, t: (b, t, 0))]
    in_specs += [_const_spec(a.shape) for a in params]
    return pl.pallas_call(
        functools.partial(_mix_kernel, layer),
        out_shape=jax.ShapeDtypeStruct((B, S, D), F32),
        grid=(B, S // T),
        in_specs=in_specs,
        out_specs=pl.BlockSpec((None, T, D), lambda b, t: (b, t, 0)),
        scratch_shapes=_mix_scratch(T),
        compiler_params=pltpu.CompilerParams(
            dimension_semantics=("arbitrary", "arbitrary"),
            vmem_limit_bytes=48 * 1024 * 1024),
        name=f"mix{layer}",
    )(h, *params)


def _ffn_tile(x, g_ref, wg_ref, wu_ref, wd_ref, act_s):
    hn = _rms_norm(x, g_ref[...]).astype(BF16)
    ff = wg_ref.shape[1]
    for c in range(0, ff, FFN_CHUNK):
        gate = _dot(hn, wg_ref[:, c:c + FFN_CHUNK])
        up = _dot(hn, wu_ref[:, c:c + FFN_CHUNK])
        act_s[:, c:c + FFN_CHUNK] = (jax.nn.silu(gate) * up).astype(BF16)
    return x + _dot(act_s[...], wd_ref[...])


def _ffn_kernel(final, h_ref, g_ref, gf_ref, wg_ref, wu_ref, wd_ref, out_ref, act_s):
    y = _ffn_tile(h_ref[...], g_ref, wg_ref, wu_ref, wd_ref, act_s)
    out_ref[...] = _rms_norm(y, gf_ref[...]) if final else y


def _ffn_layer(h2, g, w_gate, w_up, w_down, g_final, final):
    n, d = h2.shape
    ff = w_gate.shape[1]
    tm = min(FFN_TILE, n)
    assert n % tm == 0 and ff % FFN_CHUNK == 0
    once = pl.Buffered(1)
    return pl.pallas_call(
        functools.partial(_ffn_kernel, final),
        out_shape=jax.ShapeDtypeStruct((n, d), F32),
        grid=(n // tm,),
        in_specs=[pl.BlockSpec((tm, d), lambda i: (i, 0)),
                  _const_spec((1, d)),
                  _const_spec((1, d)),
                  pl.BlockSpec((d, ff), lambda i: (0, 0), pipeline_mode=once),
                  pl.BlockSpec((d, ff), lambda i: (0, 0), pipeline_mode=once),
                  pl.BlockSpec((ff, d), lambda i: (0, 0), pipeline_mode=once)],
        out_specs=pl.BlockSpec((tm, d), lambda i: (i, 0)),
        scratch_shapes=[pltpu.VMEM((tm, ff), BF16)],
        compiler_params=pltpu.CompilerParams(
            dimension_semantics=("arbitrary",),
            vmem_limit_bytes=48 * 1024 * 1024),
        name="ffn",
    )(h2, g[None, :], g_final[None, :], w_gate.astype(BF16), w_up.astype(BF16), w_down.astype(BF16))


def _router_kernel(h_ref, g_ref, wr_ref, triu_ref, hn_ref, mi_ref, mf_ref, cnt_ref):
    n_e = wr_ref.shape[0]
    tr = h_ref.shape[0]
    hn = _rms_norm(h_ref[...], g_ref[...])
    xh = hn.astype(BF16)
    hn_ref[...] = xh
    xm = (hn - xh.astype(F32)).astype(BF16)
    w = wr_ref[...]
    wh = w.astype(BF16)
    wm = (w - wh.astype(F32)).astype(BF16)
    logits = _dot_nt(wh, xh) + _dot_nt(wh, xm) + _dot_nt(wm, xh)
    eid = lax.broadcasted_iota(I32, (n_e, tr), 0)
    m1 = jnp.max(logits, axis=0, keepdims=True)
    e1 = jnp.min(jnp.where(logits == m1, eid, n_e), axis=0, keepdims=True)
    rest = jnp.where(eid == e1, -jnp.inf, logits)
    m2 = jnp.max(rest, axis=0, keepdims=True)
    e2 = jnp.min(jnp.where(rest == m2, eid, n_e), axis=0, keepdims=True)
    ex = jnp.exp(m2 - m1)
    g1 = 1.0 / (1.0 + ex)
    g2 = ex / (1.0 + ex)
    member = jnp.where((eid == e1) | (eid == e2), 1.0, 0.0)
    incl = _dot(member.astype(BF16), triu_ref[...])
    rank = incl - member
    r1 = jnp.sum(jnp.where(eid == e1, rank, 0.0), axis=0, keepdims=True)
    r2 = jnp.sum(jnp.where(eid == e2, rank, 0.0), axis=0, keepdims=True)
    cnt_ref[...] = jnp.broadcast_to(incl[:, tr - 1:tr], cnt_ref.shape).astype(I32)
    zi = jnp.zeros((1, tr), I32)
    mi_ref[...] = jnp.concatenate(
        [e1, e2, r1.astype(I32), r2.astype(I32), zi, zi, zi, zi], axis=0)
    zf = jnp.zeros((1, tr), F32)
    mf_ref[...] = jnp.concatenate([g1, g2, zf, zf, zf, zf, zf, zf], axis=0)


def _router(h2, g, w_router):
    n, d = h2.shape
    n_e = w_router.shape[1]
    tr = min(MOE_TILE, n)
    assert n % tr == 0 and n_e == V7X_SUBLANES
    tok = jnp.arange(tr)
    triu = (tok[:, None] <= tok[None, :]).astype(BF16)
    return pl.pallas_call(
        _router_kernel,
        out_shape=(jax.ShapeDtypeStruct((n, d), BF16),
                   jax.ShapeDtypeStruct((V7X_SUBLANES, n), I32),
                   jax.ShapeDtypeStruct((V7X_SUBLANES, n), F32),
                   jax.ShapeDtypeStruct((n // tr, n_e, V7X_LANES), I32)),
        grid=(n // tr,),
        in_specs=[pl.BlockSpec((tr, d), lambda i: (i, 0)),
                  _const_spec((1, d)),
                  _const_spec((n_e, d)),
                  _const_spec((tr, tr))],
        out_specs=(pl.BlockSpec((tr, d), lambda i: (i, 0)),
                   pl.BlockSpec((V7X_SUBLANES, tr), lambda i: (0, i)),
                   pl.BlockSpec((V7X_SUBLANES, tr), lambda i: (0, i)),
                   pl.BlockSpec((None, n_e, V7X_LANES), lambda i: (i, 0, 0))),
        compiler_params=pltpu.CompilerParams(dimension_semantics=("arbitrary",)),
        name="router",
    )(h2, g[None, :], w_router.T, triu)


def _segment_copies(step, slot, tables, n_e, tile, local_ref, rows_ref, sem, to_rows, start):
    off_ref, row_ref, pad_ref = tables

    def copy(off, row, size):
        local = local_ref.at[slot, pl.ds(pl.multiple_of(off, ROW_PAD), size)]
        remote = rows_ref.at[pl.ds(pl.multiple_of(row, ROW_PAD), size)]
        return (pltpu.make_async_copy(local, remote, sem.at[slot]) if to_rows
                else pltpu.make_async_copy(remote, local, sem.at[slot]))

    if not start:
        last = step * n_e + n_e - 1
        total = off_ref[last] + pad_ref[last]
        size = TOP_K * tile
        while size >= ROW_PAD:
            pl.when((total & size) != 0)(copy(0, 0, size).wait)
            size //= 2
        return
    for e in range(n_e):
        off = off_ref[step * n_e + e]
        row = row_ref[step * n_e + e]
        pad = pad_ref[step * n_e + e]
        size = tile
        while size >= ROW_PAD:
            done = pad & (-2 * size)
            pl.when((pad & size) != 0)(copy(off + done, row + done, size).start)
            size //= 2


def _stack_rows(mi, off_ref, step, n_e):
    e1, e2, row1, row2 = mi[0:1, :], mi[1:2, :], mi[2:3, :], mi[3:4, :]
    for e in range(n_e):
        off = off_ref[step * n_e + e]
        row1 = row1 + jnp.where(e1 == e, off, 0)
        row2 = row2 + jnp.where(e2 == e, off, 0)
    return row1, row2


def _fill_copies(fill_ref, n_e, zero_s, xs_ref, sem, start):
    zrows = zero_s.shape[0]
    rb = EXPERT_ROWS

    def go(cp):
        if start:
            cp.start()
        else:
            cp.wait()

    def zero_copy(row, size):
        return pltpu.make_async_copy(
            zero_s.at[pl.ds(0, size)], xs_ref.at[pl.ds(pl.multiple_of(row, ROW_PAD), size)], sem)

    for e in range(n_e):
        row = fill_ref[e]
        gap = fill_ref[n_e + e]
        size = zrows
        while size >= ROW_PAD:
            done = gap & (-2 * size)
            pl.when((gap & size) != 0)(functools.partial(go, zero_copy(row + done, size)))
            size //= 2

    def block(b, c):
        done = 0
        size = zrows
        while done < rb:
            if size <= rb - done:
                go(zero_copy(b * rb + done, size))
                done += size
            else:
                size //= 2
        return c

    lax.fori_loop(fill_ref[2 * n_e], xs_ref.shape[0] // rb, block, 0)


def _dispatch_kernel(nt, off_ref, row_ref, pad_ref, fill_ref, hn_ref, mi_ref, xs_ref, slab_s, zero_s,
                     sem, fill_sem):
    i = pl.program_id(0)
    tile = hn_ref.shape[0]
    n_e = V7X_SUBLANES
    r_stack = slab_s.shape[1]
    slot = lax.rem(i, 2)
    copies = functools.partial(_segment_copies, tables=(off_ref, row_ref, pad_ref), n_e=n_e,
                               tile=tile, local_ref=slab_s, rows_ref=xs_ref, sem=sem, to_rows=True)

    @pl.when(i == 0)
    def _():
        zero_s[...] = jnp.zeros(zero_s.shape, F32)
        _fill_copies(fill_ref, n_e, zero_s, xs_ref, fill_sem, start=True)

    @pl.when(i >= 2)
    def _():
        copies(i - 2, slot, start=False)

    row1, row2 = _stack_rows(mi_ref[...], off_ref, i, n_e)
    rid = lax.broadcasted_iota(I32, (r_stack, tile), 0)
    sel = jnp.where(rid == row1, 1.0, jnp.where(rid == row2, 1.0, 0.0)).astype(BF16)
    slab_s[slot] = _dot(sel, hn_ref[...])
    copies(i, slot, start=True)

    @pl.when(i == nt - 1)
    def _():
        if nt > 1:
            copies(i - 1, 1 - slot, start=False)
        copies(i, slot, start=False)
        _fill_copies(fill_ref, n_e, zero_s, xs_ref, fill_sem, start=False)


def _dispatch(hn, mi, tables, fill, n_rows):
    n, d = hn.shape
    tile = min(MOE_TILE, n)
    nt = n // tile
    r_stack = TOP_K * tile + V7X_SUBLANES * ROW_PAD
    return pl.pallas_call(
        functools.partial(_dispatch_kernel, nt),
        out_shape=jax.ShapeDtypeStruct((n_rows, d), F32),
        grid_spec=pltpu.PrefetchScalarGridSpec(
            num_scalar_prefetch=4,
            grid=(nt,),
            in_specs=[pl.BlockSpec((tile, d), lambda i, *_: (i, 0)),
                      pl.BlockSpec((V7X_SUBLANES, tile), lambda i, *_: (0, i))],
            out_specs=pl.BlockSpec(memory_space=pl.ANY),
            scratch_shapes=[pltpu.VMEM((2, r_stack, d), F32),
                            pltpu.VMEM((pl.next_power_of_2(EXPERT_ROWS) // 2, d), F32),
                            pltpu.SemaphoreType.DMA((2,)), pltpu.SemaphoreType.DMA(())]),
        compiler_params=pltpu.CompilerParams(dimension_semantics=("arbitrary",),
                                             has_side_effects=True),
        name="dispatch",
    )(*tables, fill, hn, mi)


def _expert_kernel(be_ref, rows_ref, src_ref, x_ref, w1_ref, w3_ref, w2_ref, out_ref,
                   w1_s, w3_s, w2_s):
    del be_ref, src_ref
    b = pl.program_id(0)
    f = pl.program_id(1)
    rows = rows_ref[b]

    @pl.when(f == 0)
    def _():
        out_ref[...] = jnp.zeros(out_ref.shape, F32)

    def cast_weights():
        w1_s[...] = w1_ref[...].astype(BF16)
        w3_s[...] = w3_ref[...].astype(BF16)
        w2_s[...] = w2_ref[...].astype(BF16)

    def swiglu_rows(r0, n_rows):
        xb = x_ref[pl.ds(r0, n_rows), :].astype(BF16)
        gate = _dot(xb, w1_s[...])
        up = _dot(xb, w3_s[...])
        act = (jax.nn.silu(gate) * up).astype(BF16)
        out_ref[pl.ds(r0, n_rows), :] += _dot(act, w2_s[...])

    @pl.when(rows == EXPERT_ROWS)
    def _():
        cast_weights()
        for s in range(EXPERT_ROWS // EXPERT_FULL):
            swiglu_rows(s * EXPERT_FULL, EXPERT_FULL)

    @pl.when((rows > 0) & (rows < EXPERT_ROWS))
    def _():
        n_sub = (rows + EXPERT_SUB - 1) // EXPERT_SUB
        n_pair = n_sub // 2

        @pl.when(n_pair >= 2)
        def _():
            cast_weights()
            swiglu_rows(0, EXPERT_FULL)
            swiglu_rows(EXPERT_FULL, EXPERT_FULL)

        @pl.when(n_pair < 2)
        def _():
            cast_weights()

            @pl.when(n_pair == 1)
            def _():
                swiglu_rows(0, EXPERT_FULL)

        def pair(s, c):
            swiglu_rows(pl.multiple_of(s * EXPERT_FULL, EXPERT_FULL), EXPERT_FULL)
            return c

        lax.fori_loop(2, n_pair, pair, 0)

        @pl.when(n_sub % 2 == 1)
        def _():
            swiglu_rows(pl.multiple_of((n_sub - 1) * EXPERT_SUB, EXPERT_SUB), EXPERT_SUB)


def _experts(xs, blk_e, blk_rows, blk_src, w1, w3, w2):
    n_rows, d = xs.shape
    n_e, _, ff = w1.shape
    rb = EXPERT_ROWS
    fft = min(EXPERT_FF, ff)
    assert n_rows % rb == 0 and ff % fft == 0
    assert EXPERT_FULL == 2 * EXPERT_SUB and rb % EXPERT_FULL == 0
    nf = ff // fft
    nb = n_rows // rb

    def f_eff(b, f, rows):
        return jnp.where(rows[b] > 0, f, nf - 1)

    return pl.pallas_call(
        _expert_kernel,
        out_shape=jax.ShapeDtypeStruct((n_rows, d), F32),
        grid_spec=pltpu.PrefetchScalarGridSpec(
            num_scalar_prefetch=3,
            grid=(nb, nf),
            in_specs=[
                pl.BlockSpec((rb, d), lambda b, f, be, rows, src: (src[b], 0)),
                pl.BlockSpec((None, d, fft), lambda b, f, be, rows, src: (be[b], 0, f_eff(b, f, rows))),
                pl.BlockSpec((None, d, fft), lambda b, f, be, rows, src: (be[b], 0, f_eff(b, f, rows))),
                pl.BlockSpec((None, fft, d), lambda b, f, be, rows, src: (be[b], f_eff(b, f, rows), 0)),
            ],
            out_specs=pl.BlockSpec((rb, d), lambda b, f, be, rows, src: (b, 0)),
            scratch_shapes=[pltpu.VMEM((d, fft), BF16), pltpu.VMEM((d, fft), BF16),
                            pltpu.VMEM((fft, d), BF16)]),
        compiler_params=pltpu.CompilerParams(
            dimension_semantics=("arbitrary", "arbitrary"),
            vmem_limit_bytes=EXPERT_VMEM_BYTES),
        name="experts",
    )(blk_e, blk_rows, blk_src, xs, w1, w3, w2)


def _combine_kernel(final, nt, off_ref, row_ref, pad_ref, h_ref, mi_ref, gt_ref, g_ref, ys_ref,
                    out_ref, stack_s, sem):
    i = pl.program_id(0)
    tile = h_ref.shape[0]
    n_e = V7X_SUBLANES
    r_stack = stack_s.shape[1]
    slot = lax.rem(i, 2)
    copies = functools.partial(_segment_copies, tables=(off_ref, row_ref, pad_ref), n_e=n_e,
                               tile=tile, local_ref=stack_s, rows_ref=ys_ref, sem=sem, to_rows=False)

    @pl.when(i == 0)
    def _():
        stack_s[...] = jnp.zeros(stack_s.shape, F32)
        copies(i, slot, start=True)

    @pl.when(i + 1 < nt)
    def _():
        copies(i + 1, 1 - slot, start=True)

    copies(i, slot, start=False)
    y_rows = stack_s[slot].astype(BF16)
    row1, row2 = _stack_rows(mi_ref[...], off_ref, i, n_e)
    rid = lax.broadcasted_iota(I32, (r_stack, TOP_K * tile), 0)
    rows12 = jnp.concatenate([row1, row2], axis=1)
    y12 = _dot_tn(jnp.where(rid == rows12, 1.0, 0.0).astype(BF16), y_rows)
    gates = gt_ref[...]
    y = h_ref[...] + y12[:tile] * gates[:, 0:1] + y12[tile:] * gates[:, 1:2]
    out_ref[...] = _rms_norm(y, g_ref[...]) if final else y


def _combine(h2, mi, gates, tables, ys, g_final, final):
    n, d = h2.shape
    tile = min(MOE_TILE, n)
    nt = n // tile
    r_stack = TOP_K * tile + V7X_SUBLANES * ROW_PAD
    return pl.pallas_call(
        functools.partial(_combine_kernel, final, nt),
        out_shape=jax.ShapeDtypeStruct((n, d), F32),
        grid_spec=pltpu.PrefetchScalarGridSpec(
            num_scalar_prefetch=3,
            grid=(nt,),
            in_specs=[pl.BlockSpec((tile, d), lambda i, *_: (i, 0)),
                      pl.BlockSpec((V7X_SUBLANES, tile), lambda i, *_: (0, i)),
                      pl.BlockSpec((tile, TOP_K), lambda i, *_: (i, 0)),
                      pl.BlockSpec((1, d), lambda i, *_: (0, 0)),
                      pl.BlockSpec(memory_space=pl.ANY)],
            out_specs=pl.BlockSpec((tile, d), lambda i, *_: (i, 0)),
            scratch_shapes=[pltpu.VMEM((2, r_stack, d), F32), pltpu.SemaphoreType.DMA((2,))]),
        compiler_params=pltpu.CompilerParams(dimension_semantics=("arbitrary",)),
        name="combine",
    )(*tables, h2, mi, gates, g_final[None, :], ys)


def _moe_layer(h2, g, w_router, w1, w3, w2, g_final, final=True):
    n, d = h2.shape
    n_e = w_router.shape[1]
    rb = EXPERT_ROWS
    tile = min(MOE_TILE, n)
    nt = n // tile
    hn, mi, mf, cnt = _router(h2, g, w_router)
    pad = (cnt[:, :, 0] + ROW_PAD - 1) // ROW_PAD * ROW_PAD
    off = jnp.cumsum(pad, axis=1) - pad
    total = jnp.sum(pad, axis=0)
    padded = (total + rb - 1) // rb * rb
    pend = jnp.cumsum(padded)
    pstart = pend - padded
    row = pstart[None, :] + jnp.cumsum(pad, axis=0) - pad
    tables = tuple(t.reshape(-1).astype(I32) for t in (off, row, pad))
    nb = (n * TOP_K + nt * n_e * (ROW_PAD - 1)) // rb + n_e
    blk = jnp.arange(nb, dtype=I32)
    n_used = pend[-1] // rb
    last = jnp.maximum(n_used - 1, 0)
    blk_src = jnp.minimum(blk, last)
    blk_e = jnp.minimum(jnp.sum(blk_src[:, None] * rb >= pend[None, :], axis=1), n_e - 1).astype(I32)
    blk_rows = jnp.where(blk < n_used,
                         jnp.clip(total[blk_e] - (blk * rb - pstart[blk_e]), 0, rb), 0).astype(I32)
    fill = jnp.concatenate([pstart + total, padded - total, n_used[None]]).astype(I32)
    xs = _dispatch(hn, mi, tables, fill, nb * rb)
    ys = _experts(xs, blk_e, blk_rows, blk_src, w1, w3, w2)
    return _combine(h2, mi, mf[:TOP_K].T, tables, ys, g_final, final)


def kernel(x, w_in, w_out, lru_conv_w, lru_conv_b, lru_w_a, lru_b_a, lru_w_x, lru_b_x, lru_lambda,
           hgrn_lower_bounds, hgrn_norm_g, pool_w, pool_scale, sconv_w, norm_mix_g, norm_ffn_g,
           ffn_w_gate, ffn_w_up, ffn_w_down, moe_w_router, moe_w1, moe_w3, moe_w2, final_norm_g):
    B, S, D = x.shape
    depth = w_in.shape[0]

    def mix_params(layer):
        return _mix_params(norm_mix_g[layer], w_in[layer], w_out[layer], lru_conv_w[layer],
                           lru_conv_b[layer], lru_w_a[layer], lru_b_a[layer], lru_w_x[layer],
                           lru_b_x[layer], lru_lambda[layer], hgrn_lower_bounds, hgrn_norm_g[layer],
                           pool_w[layer], pool_scale[layer], sconv_w[layer])

    def ffn_layer(h2, layer):
        j = layer // 2
        final = layer == depth - 1
        if layer % 2 == 0:
            return _ffn_layer(h2, norm_ffn_g[layer], ffn_w_gate[j], ffn_w_up[j], ffn_w_down[j],
                              final_norm_g, final)
        return _moe_layer(h2, norm_ffn_g[layer], moe_w_router[j], moe_w1[j], moe_w3[j], moe_w2[j],
                          final_norm_g, final)

    h = x
    for layer in range(depth):
        h = _mix_layer(layer, h, mix_params(layer))
        h = ffn_layer(h.reshape(B * S, D), layer).reshape(B, S, D)
    return h
```

```python
import functools

import jax
import jax.numpy as jnp
from jax import lax
from jax.experimental import pallas as pl
from jax.experimental.pallas import tpu as pltpu

F32 = jnp.float32
BF16 = jnp.bfloat16
I32 = jnp.int32

GROUP_WIDTH = 256
N_GROUP_HEADS = 4
HEAD_DIM = 64
N_IN_SLICES = 10
LRU_CONV = 4
LRU_C = 8.0
HGRN_CHUNK = 64
POOL_WINDOWS = (2, 4, 8, 16)
SCONV_WIDTH = 3
TOP_K = 2
NORM_EPS = 1e-6
HEAD_NORM_EPS = 1e-5
LOG2_E = 1.4426950408889634

V7X_LANES = 128
V7X_SUBLANES = 8
V7X_VMEM_BYTES = 64 * 1024 * 1024

MIX_TILE = 512
HALO = 16
DIAG = 8
FFN_TILE = 1024
FFN_CHUNK = 256
MOE_TILE = 256
ROW_PAD = V7X_SUBLANES
EXPERT_ROWS = 2560
KERNEL_VMEM_BYTES = V7X_VMEM_BYTES * 7 // 8
EXPERT_SUB = 256
EXPERT_FULL = 512
EXPERT_FF = 512


def _dot(a, b):
    return jnp.dot(a, b, preferred_element_type=F32)


def _dot_nt(a, b):
    return lax.dot_general(a, b, (((1,), (1,)), ((), ())), preferred_element_type=F32)


def _dot_tn(a, b):
    return lax.dot_general(a, b, (((0,), (0,)), ((), ())), preferred_element_type=F32)


def _rms_norm(x, g):
    ms = jnp.mean(x * x, axis=-1, keepdims=True)
    return x * lax.rsqrt(ms + NORM_EPS) * g


def _split3(x):
    hi = x.astype(BF16)
    r1 = x - hi.astype(F32)
    mid = r1.astype(BF16)
    lo = (r1 - mid.astype(F32)).astype(BF16)
    return hi, mid, lo


N_MIX_PARAMS = 16
N_MIX_SCRATCH = 16


def _mix_reset(ti, scratch):
    (proj_s, ax_s, p_s, sc_s, sa_s, su_s, ta_s, tu_s, tp_s, lruh_s, st_s, q_s, k_s, lf_s, o_s,
     y_s) = scratch
    GW = GROUP_WIDTH

    @pl.when(ti == 0)
    def _():
        ax_s[0:HALO, :] = jnp.zeros((HALO, GW), F32)
        p_s[0:HALO, :] = jnp.zeros((HALO, GW), F32)
        sc_s[0:HALO, :] = jnp.zeros((HALO, GW), F32)
        sa_s[0:V7X_SUBLANES, :] = jnp.ones((V7X_SUBLANES, GW), F32)
        su_s[0:V7X_SUBLANES, :] = jnp.zeros((V7X_SUBLANES, GW), F32)
        ta_s[0:V7X_SUBLANES, :] = jnp.zeros((V7X_SUBLANES, GW), F32)
        tu_s[0:V7X_SUBLANES, :] = jnp.zeros((V7X_SUBLANES, GW), F32)
        tp_s[0:HALO, :] = jnp.zeros((HALO, GW), F32)
        lruh_s[...] = jnp.zeros(lruh_s.shape, F32)
        st_s[...] = jnp.zeros(st_s.shape, F32)


def _mix_tile(layer, x, ti, params, scratch):
    (gmix_ref, w_in_ref, w_out_ref, cw_ref, cb_ref, wg_ref, bg_ref, lam_ref, lbraw_ref, hg_ref,
     wp_ref, ps_ref, sw_ref, ones_ref, tri_ref, lvl_ref) = params
    (proj_s, ax_s, p_s, sc_s, sa_s, su_s, ta_s, tu_s, tp_s, lruh_s, st_s, q_s, k_s, lf_s, o_s,
     y_s) = scratch
    T = x.shape[0]
    GW = GROUP_WIDTH
    hn = _rms_norm(x, gmix_ref[...]).astype(BF16)
    proj_s[...] = _dot(hn, w_in_ref[...])

    def sl(i):
        return proj_s[:, i * GW:(i + 1) * GW]

    ax_s[HALO:HALO + T, :] = sl(0)
    a_in = cb_ref[...]
    for kk in range(LRU_CONV):
        off = HALO - (LRU_CONV - 1) + kk
        a_in = a_in + cw_ref[kk:kk + 1, :] * ax_s[off:off + T, :]
    gates = _dot(a_in.astype(BF16), wg_ref[...]) + bg_ref[...]
    r_gate = jax.nn.sigmoid(gates[:, :GW])
    i_gate = jax.nn.sigmoid(gates[:, GW:])
    a = jnp.exp2(r_gate * ((-LRU_C * LOG2_E) * jax.nn.softplus(-lam_ref[...])))
    mult = jnp.sqrt(1.0 - a * a)
    u = mult * i_gate * a_in
    SUB = V7X_SUBLANES
    sa_s[SUB:SUB + T, :] = a
    su_s[SUB:SUB + T, :] = u
    a = sa_s[...]
    u = su_s[...]
    d = 1
    while d < SUB:
        ta_s[SUB:, :] = a
        tu_s[SUB:, :] = u
        a_sh = ta_s[SUB - d:SUB - d + SUB + T, :]
        u_sh = tu_s[SUB - d:SUB - d + SUB + T, :]
        u = a * u_sh + u
        a = a * a_sh
        d *= 2
    sa_s[0:SUB, :] = sa_s[T:T + SUB, :]
    su_s[0:SUB, :] = su_s[T:T + SUB, :]
    h_grp = lruh_s[...]
    groups = []
    for gi in range(1, T // SUB + 1):
        h_grp = a[gi * SUB:(gi + 1) * SUB, :] * h_grp + u[gi * SUB:(gi + 1) * SUB, :]
        groups.append(h_grp)
    lruh_s[...] = h_grp
    h_lru = jnp.concatenate(groups, axis=0)
    y_a = h_lru * jax.nn.gelu(sl(1))
    y_s[:, 0:GW] = y_a.astype(BF16)
    ax_s[0:HALO, :] = ax_s[T:T + HALO, :]

    p = sl(6)
    p_s[HALO:HALO + T, :] = p
    lane = lax.broadcasted_iota(I32, (1, GW), 1)
    win = jnp.where(lane < HEAD_DIM, float(POOL_WINDOWS[0]),
                    jnp.where(lane < 2 * HEAD_DIM, float(POOL_WINDOWS[1]),
                              jnp.where(lane < 3 * HEAD_DIM, float(POOL_WINDOWS[2]),
                                        float(POOL_WINDOWS[3]))))
    sums = []
    acc = p_s[...]
    d = 1
    while d < max(POOL_WINDOWS):
        tp_s[HALO:, :] = acc
        acc = acc + tp_s[HALO - d:HALO - d + HALO + T, :]
        sums.append(acc[HALO:, :])
        d *= 2
    wsum = jnp.where(lane < HEAD_DIM, sums[0],
                     jnp.where(lane < 2 * HEAD_DIM, sums[1],
                               jnp.where(lane < 3 * HEAD_DIM, sums[2], sums[3])))
    pos =(ti * T + 1 + lax.broadcasted_iota(I32, (T, 1), 0)).astype(F32)
    dpool = wsum / jnp.minimum(pos, win) - p
    y_c = _dot(dpool.astype(BF16), wp_ref[...]) * ps_ref[...]
    y_s[:, 2 * GW:3 * GW] = y_c.astype(BF16)
    p_s[0:HALO, :] = p_s[T:T + HALO, :]

    sc_s[HALO:HALO + T, :] = sl(8) * sl(9)
    conv = jnp.zeros((T, GW), F32)
    for kk in range(SCONV_WIDTH):
        off = HALO - (SCONV_WIDTH - 1) + kk
        conv = conv + sw_ref[kk:kk + 1, :] * sc_s[off:off + T, :]
    y_s[:, 3 * GW:4 * GW] = (sl(7) * conv).astype(BF16)
    sc_s[0:HALO, :] = sc_s[T:T + HALO, :]

    lbraw = lbraw_ref[...]
    e_lb = jnp.exp(lbraw - jnp.max(lbraw, axis=0, keepdims=True))
    p_lb = e_lb / jnp.sum(e_lb, axis=0, keepdims=True)
    lb = jnp.zeros((1, GW), F32)
    for li in range(1, layer + 1):
        lb = lb + p_lb[li:li + 1, :]
    z = sl(3)
    lf_s[...] = jnp.log2(lb + (1.0 - lb) * jax.nn.sigmoid(z))
    k_s[...] = (1.0 - lb) * jax.nn.sigmoid(-z)
    q_s[...] = jax.nn.silu(sl(2))

    L = HGRN_CHUNK
    ones_bd = ones_ref[...]
    ones_f = ones_bd.astype(F32)
    tri = tri_ref[...]
    widths = _level_widths()
    row_l = lax.broadcasted_iota(I32, (L, 1), 0)
    row_d = lax.broadcasted_iota(I32, (DIAG, 1), 0)

    def chunk_body(c, carry):
        r0 = c * L
        lf = lf_s[pl.ds(r0, L), :]
        hi, mid, lo = _split3(lf)
        G = _dot(tri, hi) + _dot(tri, mid) + _dot(tri, lo)

        def g_row(r):
            return G[r:r + 1, :]

        q =q_s[pl.ds(r0, L), :]
        k = k_s[pl.ds(r0, L), :]
        v = proj_s[pl.ds(r0, L), 4 * GW:5 * GW]
        vb = v.astype(BF16)
        g_last = g_row(L - 1)
        st = st_s[...]

        o = _dot_nt((q * jnp.exp2(G)).astype(BF16), st.astype(BF16))

        scores = jnp.zeros((N_GROUP_HEADS * L, L), F32)
        for li, w in enumerate(widths):
            gref = g_row(w - 1)
            for pair in range(1, L // (2 * w)):
                gref = jnp.where(row_l < pair * 2 * w, gref, g_row(pair * 2 * w + w - 1))
            qt = q * jnp.exp2(jnp.minimum(G - gref, 0.0))
            kt = k * jnp.exp2(jnp.minimum(gref - G, 0.0))
            q4 = jnp.concatenate([qt.astype(BF16)] * N_GROUP_HEADS, axis=0) * ones_bd
            s4 = _dot_nt(q4, kt.astype(BF16))
            scores = scores + s4 * lvl_ref[li]
        o4 = _dot(scores.astype(BF16), vb) * ones_f
        for hh in range(N_GROUP_HEADS):
            o = o + o4[hh * L:(hh + 1) * L, :]

        o_diag = []
        for b in range(L // DIAG):
            gs = G[b * DIAG:(b + 1) * DIAG, :]
            qs = q[b * DIAG:(b + 1) * DIAG, :]
            ks = k[b * DIAG:(b + 1) * DIAG, :]
            vs = v[b * DIAG:(b + 1) * DIAG, :]
            terms = []
            for j in range(DIAG):
                dg = jnp.where(row_d >= j, gs - gs[j:j + 1, :], -jnp.inf)
                terms.append((qs * (ks[j:j + 1, :] * jnp.exp2(dg))).astype(BF16))
            e = jnp.concatenate(terms, axis=0)
            pd = _dot(e, ones_bd)
            od = jnp.zeros((DIAG, GW), F32)
            for j in range(DIAG):
                od = od + pd[j * DIAG:(j + 1) * DIAG, :] * vs[j:j + 1, :]
            o_diag.append(od)
        o = o + jnp.concatenate(o_diag, axis=0)
        o_s[pl.ds(r0, L), :] = o

        kd = (k * jnp.exp2(g_last - G)).astype(BF16)
        st_s[...] = st * jnp.exp2(g_last) + _dot_tn(vb, kd) * ones_f
        return carry

    for c in range(T // L):
        chunk_body(c, 0)

    o = o_s[...]
    ms = _dot((o * o).astype(BF16), ones_bd) * (1.0 / HEAD_DIM)
    o = o * lax.rsqrt(ms + HEAD_NORM_EPS) * hg_ref[...]
    y_s[:, GW:2 * GW] = (o * jax.nn.silu(sl(5))).astype(BF16)

    return x + _dot(y_s[...], w_out_ref[...])


def _mix_kernel(layer, h_ref, *refs):
    params = refs[:N_MIX_PARAMS]
    out_ref = refs[N_MIX_PARAMS]
    scratch = refs[N_MIX_PARAMS + 1:]
    _mix_reset(pl.program_id(1), scratch)
    out_ref[...] = _mix_tile(layer, h_ref[...], pl.program_id(1), params, scratch)


def _level_widths():
    widths = []
    w = DIAG
    while w < HGRN_CHUNK:
        widths.append(w)
        w *= 2
    return widths


def _level_masks():
    L = HGRN_CHUNK
    t = jnp.arange(N_GROUP_HEADS * L)[:, None] % L
    s = jnp.arange(L)[None, :]
    return jnp.stack([(((t // w) % 2 == 1) & (s // w == t // w - 1)).astype(F32)
                      for w in _level_widths()])


def _block_diag(w):
    return jax.scipy.linalg.block_diag(*[w[i] for i in range(w.shape[0])])


def _const_spec(shape):
    nd = len(shape)
    return pl.BlockSpec(shape, lambda *_: (0,) * nd)


def _mix_params(gmix, w_in, w_out, conv_w, conv_b, w_a, b_a, w_x, b_x, lam, lb_raw, norm_g,
                pool_w, pool_scale, sconv_w):
    GW = GROUP_WIDTH
    assert w_in.shape[1] == N_IN_SLICES * GW and POOL_WINDOWS == (2, 4, 8, 16)
    wg = jnp.concatenate([_block_diag(w_a), _block_diag(w_x)], axis=1).astype(BF16)
    bg = jnp.concatenate([b_a, b_x])[None, :]
    head_of = jnp.arange(GW) // HEAD_DIM
    ones_bd = (head_of[:, None] == head_of[None, :]).astype(BF16)
    tri = jnp.tril(jnp.ones((HGRN_CHUNK, HGRN_CHUNK), BF16))
    params = [gmix[None, :], w_in.astype(BF16), w_out.astype(BF16), conv_w, conv_b[None, :], wg, bg,
              lam[None, :], lb_raw, jnp.tile(norm_g, N_GROUP_HEADS)[None, :],
              _block_diag(pool_w).astype(BF16), pool_scale[None, :], sconv_w, ones_bd, tri,
              _level_masks()]
    assert len(params) == N_MIX_PARAMS
    return params


def _mix_scratch(T):
    GW = GROUP_WIDTH
    scratch = [
        pltpu.VMEM((T, N_IN_SLICES * GW), F32),
        pltpu.VMEM((HALO + T, GW), F32),
        pltpu.VMEM((HALO + T, GW), F32),
        pltpu.VMEM((HALO + T, GW), F32),
        pltpu.VMEM((V7X_SUBLANES + T, GW), F32),
        pltpu.VMEM((V7X_SUBLANES + T, GW), F32),
        pltpu.VMEM((2 * V7X_SUBLANES + T, GW), F32),
        pltpu.VMEM((2 * V7X_SUBLANES + T, GW), F32),
        pltpu.VMEM((2 * HALO + T, GW), F32),
        pltpu.VMEM((V7X_SUBLANES, GW), F32),
        pltpu.VMEM((GW, GW), F32),
        pltpu.VMEM((T, GW), F32),
        pltpu.VMEM((T, GW), F32),
        pltpu.VMEM((T, GW), F32),
        pltpu.VMEM((T, GW), F32),
        pltpu.VMEM((T, N_GROUP_HEADS * GW), BF16),
    ]
    assert len(scratch) == N_MIX_SCRATCH
    return scratch


def _mix_layer(layer, h, params):
    B, S, D = h.shape
    T = min(MIX_TILE, S)
    assert S % T == 0 and T % HGRN_CHUNK == 0
    in_specs = [pl.BlockSpec((None, T, D), lambda b, t: (b, t, 0))]
    in_specs += [_const_spec(a.shape) for a in params]
    return pl.pallas_call(
        functools.partial(_mix_kernel, layer),
        out_shape=jax.ShapeDtypeStruct((B, S, D), F32),
        grid=(B, S // T),
        in_specs=in_specs,
        out_specs=pl.BlockSpec((None, T, D), lambda b, t: (b, t, 0)),
        scratch_shapes=_mix_scratch(T),
        compiler_params=pltpu.CompilerParams(
            dimension_semantics=("arbitrary", "arbitrary"),
            vmem_limit_bytes=KERNEL_VMEM_BYTES),
        name=f"mix{layer}",
    )(h, *params)


def _ffn_tile(x, g_ref, wg_ref, wu_ref, wd_ref, act_s):
    hn = _rms_norm(x, g_ref[...]).astype(BF16)
    ff = wg_ref.shape[1]
    for c in range(0, ff, FFN_CHUNK):
        gate = _dot(hn, wg_ref[:, c:c + FFN_CHUNK])
        up = _dot(hn, wu_ref[:, c:c + FFN_CHUNK])
        act_s[:, c:c + FFN_CHUNK] = (jax.nn.silu(gate) * up).astype(BF16)
    return x + _dot(act_s[...], wd_ref[...])


def _ffn_kernel(final, h_ref, g_ref, gf_ref, wg_ref, wu_ref, wd_ref, out_ref, act_s):
    y = _ffn_tile(h_ref[...], g_ref, wg_ref, wu_ref, wd_ref, act_s)
    out_ref[...] = _rms_norm(y, gf_ref[...]) if final else y


def _ffn_layer(h2, g, w_gate, w_up, w_down, g_final, final):
    n, d = h2.shape
    ff = w_gate.shape[1]
    tm = min(FFN_TILE, n)
    assert n % tm == 0 and ff % FFN_CHUNK == 0
    once = pl.Buffered(1)
    return pl.pallas_call(
        functools.partial(_ffn_kernel, final),
        out_shape=jax.ShapeDtypeStruct((n, d), F32),
        grid=(n // tm,),
        in_specs=[pl.BlockSpec((tm, d), lambda i: (i, 0)),
                  _const_spec((1, d)),
                  _const_spec((1, d)),
                  pl.BlockSpec((d, ff), lambda i: (0, 0), pipeline_mode=once),
                  pl.BlockSpec((d, ff), lambda i: (0, 0), pipeline_mode=once),
                  pl.BlockSpec((ff, d), lambda i: (0, 0), pipeline_mode=once)],
        out_specs=pl.BlockSpec((tm, d), lambda i: (i, 0)),
        scratch_shapes=[pltpu.VMEM((tm, ff), BF16)],
        compiler_params=pltpu.CompilerParams(
            dimension_semantics=("arbitrary",),
            vmem_limit_bytes=KERNEL_VMEM_BYTES),
        name="ffn",
    )(h2, g[None, :], g_final[None, :], w_gate.astype(BF16), w_up.astype(BF16), w_down.astype(BF16))


def _router_kernel(h_ref, g_ref, wr_ref, triu_ref, hn_ref, mi_ref, mf_ref, cnt_ref):
    n_e = wr_ref.shape[0]
    tr = h_ref.shape[0]
    hn = _rms_norm(h_ref[...], g_ref[...])
    xh = hn.astype(BF16)
    hn_ref[...] = xh
    xm = (hn - xh.astype(F32)).astype(BF16)
    w = wr_ref[...]
    wh = w.astype(BF16)
    wm = (w - wh.astype(F32)).astype(BF16)
    logits = _dot_nt(wh, xh) + _dot_nt(wh, xm) + _dot_nt(wm, xh)
    eid = lax.broadcasted_iota(I32, (n_e, tr), 0)
    m1 = jnp.max(logits, axis=0, keepdims=True)
    e1 = jnp.min(jnp.where(logits == m1, eid, n_e), axis=0, keepdims=True)
    rest = jnp.where(eid == e1, -jnp.inf, logits)
    m2 = jnp.max(rest, axis=0, keepdims=True)
    e2 = jnp.min(jnp.where(rest == m2, eid, n_e), axis=0, keepdims=True)
    ex = jnp.exp(m2 - m1)
    g1 = 1.0 / (1.0 + ex)
    g2 = ex / (1.0 + ex)
    member = jnp.where((eid == e1) | (eid == e2), 1.0, 0.0)
    incl = _dot(member.astype(BF16), triu_ref[...])
    rank = incl - member
    r1 = jnp.sum(jnp.where(eid == e1, rank, 0.0), axis=0, keepdims=True)
    r2 = jnp.sum(jnp.where(eid == e2, rank, 0.0), axis=0, keepdims=True)
    cnt_ref[...] = jnp.broadcast_to(incl[:, tr - 1:tr], cnt_ref.shape).astype(I32)
    zi = jnp.zeros((1, tr), I32)
    mi_ref[...] = jnp.concatenate(
        [e1, e2, r1.astype(I32), r2.astype(I32), zi, zi, zi, zi], axis=0)
    zf = jnp.zeros((1, tr), F32)
    mf_ref[...] = jnp.concatenate([g1, g2, zf, zf, zf, zf, zf, zf], axis=0)


def _router(h2, g, w_router):
    n, d = h2.shape
    n_e = w_router.shape[1]
    tr = min(MOE_TILE, n)
    assert n % tr == 0 and n_e == V7X_SUBLANES
    tok = jnp.arange(tr)
    triu = (tok[:, None] <= tok[None, :]).astype(BF16)
    return pl.pallas_call(
        _router_kernel,
        out_shape=(jax.ShapeDtypeStruct((n, d), BF16),
                   jax.ShapeDtypeStruct((V7X_SUBLANES, n), I32),
                   jax.ShapeDtypeStruct((V7X_SUBLANES, n), F32),
                   jax.ShapeDtypeStruct((n // tr, n_e, V7X_LANES), I32)),
        grid=(n // tr,),
        in_specs=[pl.BlockSpec((tr, d), lambda i: (i, 0)),
                  _const_spec((1, d)),
                  _const_spec((n_e, d)),
                  _const_spec((tr, tr))],
        out_specs=(pl.BlockSpec((tr, d), lambda i: (i, 0)),
                   pl.BlockSpec((V7X_SUBLANES, tr), lambda i: (0, i)),
                   pl.BlockSpec((V7X_SUBLANES, tr), lambda i: (0, i)),
                   pl.BlockSpec((None, n_e, V7X_LANES), lambda i: (i, 0, 0))),
        compiler_params=pltpu.CompilerParams(dimension_semantics=("arbitrary",)),
        name="router",
    )(h2, g[None, :], w_router.T, triu)


U32 = jnp.uint32
HIGH_HALF = 0xFFFF0000


def _pack_bf16_pairs(x):
    half = x.shape[1] // 2
    lo = lax.bitcast_convert_type(x[:, :half], U32)
    hi = lax.bitcast_convert_type(x[:, half:], U32)
    return (hi & U32(HIGH_HALF)) | (lo >> U32(16))


def _unpack_bf16_pairs(w):
    lo = lax.bitcast_convert_type(w << U32(16), F32)
    hi = lax.bitcast_convert_type(w & U32(HIGH_HALF), F32)
    return jnp.concatenate([lo, hi], axis=1).astype(BF16)


def _segment_copies(step, slot, tables, n_e, tile, local_ref, rows_ref, sem, to_rows, start):
    off_ref, row_ref, pad_ref = tables

    def copy(off, row, size):
        local = local_ref.at[slot, pl.ds(pl.multiple_of(off, ROW_PAD), size)]
        remote = rows_ref.at[pl.ds(pl.multiple_of(row, ROW_PAD), size)]
        return (pltpu.make_async_copy(local, remote, sem.at[slot]) if to_rows
                else pltpu.make_async_copy(remote, local, sem.at[slot]))

    if not start:
        last = step * n_e + n_e - 1
        total = off_ref[last] + pad_ref[last]
        size = TOP_K * tile
        while size >= ROW_PAD:
            pl.when((total & size) != 0)(copy(0, 0, size).wait)
            size //= 2
        return
    for e in range(n_e):
        off = off_ref[step * n_e + e]
        row = row_ref[step * n_e + e]
        pad = pad_ref[step * n_e + e]
        size = tile
        while size >= ROW_PAD:
            done = pad & (-2 * size)
            pl.when((pad & size) != 0)(copy(off + done, row + done, size).start)
            size //= 2


def _stack_rows(mi, off_ref, step, n_e):
    e1, e2, row1, row2 = mi[0:1, :], mi[1:2, :], mi[2:3, :], mi[3:4, :]
    for e in range(n_e):
        off = off_ref[step * n_e + e]
        row1 = row1 + jnp.where(e1 == e, off, 0)
        row2 = row2 + jnp.where(e2 == e, off, 0)
    return row1, row2


def _fill_copies(fill_ref, n_e, zero_s, xs_ref, sem, start):
    zrows = zero_s.shape[0]
    rb = EXPERT_ROWS

    def go(cp):
        if start:
            cp.start()
        else:
            cp.wait()

    def zero_copy(row, size):
        return pltpu.make_async_copy(
            zero_s.at[pl.ds(0, size)], xs_ref.at[pl.ds(pl.multiple_of(row, ROW_PAD), size)], sem)

    for e in range(n_e):
        row = fill_ref[e]
        gap = fill_ref[n_e + e]
        size = zrows
        while size >= ROW_PAD:
            done = gap & (-2 * size)
            pl.when((gap & size) != 0)(functools.partial(go, zero_copy(row + done, size)))
            size //= 2

    def block(b, c):
        done = 0
        size = zrows
        while done < rb:
            if size <= rb - done:
                go(zero_copy(b * rb + done, size))
                done += size
            else:
                size //= 2
        return c

    lax.fori_loop(fill_ref[2 * n_e], xs_ref.shape[0] // rb, block, 0)


def _dispatch_kernel(nt, off_ref, row_ref, pad_ref, fill_ref, hn_ref, mi_ref, xs_ref, slab_s, zero_s,
                     sem, fill_sem):
    i = pl.program_id(0)
    tile = hn_ref.shape[0]
    n_e = V7X_SUBLANES
    r_stack = slab_s.shape[1]
    slot = lax.rem(i, 2)
    copies = functools.partial(_segment_copies, tables=(off_ref, row_ref, pad_ref), n_e=n_e,
                               tile=tile, local_ref=slab_s, rows_ref=xs_ref, sem=sem, to_rows=True)

    @pl.when(i == 0)
    def _():
        zero_s[...] = jnp.zeros(zero_s.shape, U32)
        _fill_copies(fill_ref, n_e, zero_s, xs_ref, fill_sem, start=True)

    @pl.when(i >= 2)
    def _():
        copies(i - 2, slot, start=False)

    row1, row2 = _stack_rows(mi_ref[...], off_ref, i, n_e)
    rid = lax.broadcasted_iota(I32, (r_stack, tile), 0)
    sel = jnp.where(rid == row1, 1.0, jnp.where(rid == row2, 1.0, 0.0)).astype(BF16)
    slab_s[slot] = _pack_bf16_pairs(_dot(sel, hn_ref[...]))
    copies(i, slot, start=True)

    @pl.when(i == nt - 1)
    def _():
        if nt > 1:
            copies(i - 1, 1 - slot, start=False)
        copies(i, slot, start=False)
        _fill_copies(fill_ref, n_e, zero_s, xs_ref, fill_sem, start=False)


def _dispatch(hn, mi, tables, fill, n_rows):
    n, d = hn.shape
    tile = min(MOE_TILE, n)
    nt = n // tile
    r_stack = TOP_K * tile + V7X_SUBLANES * ROW_PAD
    return pl.pallas_call(
        functools.partial(_dispatch_kernel, nt),
        out_shape=jax.ShapeDtypeStruct((n_rows, d // 2), U32),
        grid_spec=pltpu.PrefetchScalarGridSpec(
            num_scalar_prefetch=4,
            grid=(nt,),
            in_specs=[pl.BlockSpec((tile, d), lambda i, *_: (i, 0)),
                      pl.BlockSpec((V7X_SUBLANES, tile), lambda i, *_: (0, i))],
            out_specs=pl.BlockSpec(memory_space=pl.ANY),
            scratch_shapes=[pltpu.VMEM((2, r_stack, d // 2), U32),
                            pltpu.VMEM((pl.next_power_of_2(EXPERT_ROWS) // 2, d // 2), U32),
                            pltpu.SemaphoreType.DMA((2,)), pltpu.SemaphoreType.DMA(())]),
        compiler_params=pltpu.CompilerParams(dimension_semantics=("arbitrary",),
                                             has_side_effects=True),
        name="dispatch",
    )(*tables, fill, hn, mi)


def _expert_kernel(be_ref, rows_ref, src_ref, x_ref, w1_ref, w3_ref, w2_ref, out_ref,
                   w1_s, w3_s, w2_s):
    del be_ref, src_ref
    b = pl.program_id(0)
    f = pl.program_id(1)
    rows = rows_ref[b]

    @pl.when(f == 0)
    def _():
        out_ref[...] = jnp.zeros(out_ref.shape, F32)

    def cast_weights():
        w1_s[...] = w1_ref[...].astype(BF16)
        w3_s[...] = w3_ref[...].astype(BF16)
        w2_s[...] = w2_ref[...].astype(BF16)

    def swiglu_rows(r0, n_rows):
        xb = _unpack_bf16_pairs(x_ref[pl.ds(r0, n_rows), :])
        gate = _dot(xb, w1_s[...])
        up = _dot(xb, w3_s[...])
        act = (jax.nn.silu(gate) * up).astype(BF16)
        out_ref[pl.ds(r0, n_rows), :] += _dot(act, w2_s[...])

    @pl.when(rows == EXPERT_ROWS)
    def _():
        cast_weights()
        for s in range(EXPERT_ROWS // EXPERT_FULL):
            swiglu_rows(s * EXPERT_FULL, EXPERT_FULL)

    @pl.when((rows > 0) & (rows < EXPERT_ROWS))
    def _():
        n_sub = (rows + EXPERT_SUB - 1) // EXPERT_SUB
        n_pair = n_sub // 2

        @pl.when(n_pair >= 2)
        def _():
            cast_weights()
            swiglu_rows(0, EXPERT_FULL)
            swiglu_rows(EXPERT_FULL, EXPERT_FULL)

        @pl.when(n_pair < 2)
        def _():
            cast_weights()

            @pl.when(n_pair == 1)
            def _():
                swiglu_rows(0, EXPERT_FULL)

        def pair(s, c):
            swiglu_rows(pl.multiple_of(s * EXPERT_FULL, EXPERT_FULL), EXPERT_FULL)
            return c

        lax.fori_loop(2, n_pair, pair, 0)

        @pl.when(n_sub % 2 == 1)
        def _():
            swiglu_rows(pl.multiple_of((n_sub - 1) * EXPERT_SUB, EXPERT_SUB), EXPERT_SUB)


def _experts(xs, blk_e, blk_rows, blk_src, w1, w3, w2):
    n_rows = xs.shape[0]
    n_e, d, ff = w1.shape
    assert xs.shape[1] * 2 == d
    rb = EXPERT_ROWS
    fft = min(EXPERT_FF, ff)
    assert n_rows % rb == 0 and ff % fft == 0
    assert EXPERT_FULL == 2 * EXPERT_SUB and rb % EXPERT_FULL == 0
    nf = ff // fft
    nb = n_rows // rb

    def f_eff(b, f, rows):
        return jnp.where(rows[b] > 0, f, nf - 1)

    return pl.pallas_call(
        _expert_kernel,
        out_shape=jax.ShapeDtypeStruct((n_rows, d), F32),
        grid_spec=pltpu.PrefetchScalarGridSpec(
            num_scalar_prefetch=3,
            grid=(nb, nf),
            in_specs=[
                pl.BlockSpec((rb, d // 2), lambda b, f, be, rows, src: (src[b], 0)),
                pl.BlockSpec((None, d, fft), lambda b, f, be, rows, src: (be[b], 0, f_eff(b, f, rows))),
                pl.BlockSpec((None, d, fft), lambda b, f, be, rows, src: (be[b], 0, f_eff(b, f, rows))),
                pl.BlockSpec((None, fft, d), lambda b, f, be, rows, src: (be[b], f_eff(b, f, rows), 0)),
            ],
            out_specs=pl.BlockSpec((rb, d), lambda b, f, be, rows, src: (b, 0)),
            scratch_shapes=[pltpu.VMEM((d, fft), BF16), pltpu.VMEM((d, fft), BF16),
                            pltpu.VMEM((fft, d), BF16)]),
        compiler_params=pltpu.CompilerParams(
            dimension_semantics=("arbitrary", "arbitrary"),
            vmem_limit_bytes=KERNEL_VMEM_BYTES),
        name="experts",
    )(blk_e, blk_rows, blk_src, xs, w1, w3, w2)


def _combine_kernel(final, nt, off_ref, row_ref, pad_ref, h_ref, mi_ref, gt_ref, g_ref, ys_ref,
                    out_ref, stack_s, sem):
    i = pl.program_id(0)
    tile = h_ref.shape[0]
    n_e = V7X_SUBLANES
    r_stack = stack_s.shape[1]
    slot = lax.rem(i, 2)
    copies = functools.partial(_segment_copies, tables=(off_ref, row_ref, pad_ref), n_e=n_e,
                               tile=tile, local_ref=stack_s, rows_ref=ys_ref, sem=sem, to_rows=False)

    @pl.when(i == 0)
    def _():
        stack_s[...] = jnp.zeros(stack_s.shape, F32)
        copies(i, slot, start=True)

    @pl.when(i + 1 < nt)
    def _():
        copies(i + 1, 1 - slot, start=True)

    copies(i, slot, start=False)
    y_rows = stack_s[slot].astype(BF16)
    row1, row2 = _stack_rows(mi_ref[...], off_ref, i, n_e)
    rid = lax.broadcasted_iota(I32, (r_stack, TOP_K * tile), 0)
    rows12 = jnp.concatenate([row1, row2], axis=1)
    y12 = _dot_tn(jnp.where(rid == rows12, 1.0, 0.0).astype(BF16), y_rows)
    gates = gt_ref[...]
    y = h_ref[...] + y12[:tile] * gates[:, 0:1] + y12[tile:] * gates[:, 1:2]
    out_ref[...] = _rms_norm(y, g_ref[...]) if final else y


def _combine(h2, mi, gates, tables, ys, g_final, final):
    n, d = h2.shape
    tile = min(MOE_TILE, n)
    nt = n // tile
    r_stack = TOP_K * tile + V7X_SUBLANES * ROW_PAD
    return pl.pallas_call(
        functools.partial(_combine_kernel, final, nt),
        out_shape=jax.ShapeDtypeStruct((n, d), F32),
        grid_spec=pltpu.PrefetchScalarGridSpec(
            num_scalar_prefetch=3,
            grid=(nt,),
            in_specs=[pl.BlockSpec((tile, d), lambda i, *_: (i, 0)),
                      pl.BlockSpec((V7X_SUBLANES, tile), lambda i, *_: (0, i)),
                      pl.BlockSpec((tile, TOP_K), lambda i, *_: (i, 0)),
                      pl.BlockSpec((1, d), lambda i, *_: (0, 0)),
                      pl.BlockSpec(memory_space=pl.ANY)],
            out_specs=pl.BlockSpec((tile, d), lambda i, *_: (i, 0)),
            scratch_shapes=[pltpu.VMEM((2, r_stack, d), F32), pltpu.SemaphoreType.DMA((2,))]),
        compiler_params=pltpu.CompilerParams(dimension_semantics=("arbitrary",)),
        name="combine",
    )(*tables, h2, mi, gates, g_final[None, :], ys)


def _moe_layer(h2, g, w_router, w1, w3, w2, g_final, final=True):
    n, d = h2.shape
    n_e = w_router.shape[1]
    rb = EXPERT_ROWS
    tile = min(MOE_TILE, n)
    nt = n // tile
    hn, mi, mf, cnt = _router(h2, g, w_router)
    pad = (cnt[:, :, 0] + ROW_PAD - 1) // ROW_PAD * ROW_PAD
    off = jnp.cumsum(pad, axis=1) - pad
    total = jnp.sum(pad, axis=0)
    padded = (total + rb - 1) // rb * rb
    pend = jnp.cumsum(padded)
    pstart = pend - padded
    row = pstart[None, :] + jnp.cumsum(pad, axis=0) - pad
    tables = tuple(t.reshape(-1).astype(I32) for t in (off, row, pad))
    nb = (n * TOP_K + nt * n_e * (ROW_PAD - 1)) // rb + n_e
    blk = jnp.arange(nb, dtype=I32)
    n_used = pend[-1] // rb
    last = jnp.maximum(n_used - 1, 0)
    blk_src = jnp.minimum(blk, last)
    blk_e = jnp.minimum(jnp.sum(blk_src[:, None] * rb >= pend[None, :], axis=1), n_e - 1).astype(I32)
    blk_rows = jnp.where(blk < n_used,
                         jnp.clip(total[blk_e] - (blk * rb - pstart[blk_e]), 0, rb), 0).astype(I32)
    fill = jnp.concatenate([pstart + total, padded - total, n_used[None]]).astype(I32)
    xs = _dispatch(hn, mi, tables, fill, nb * rb)
    ys = _experts(xs, blk_e, blk_rows, blk_src, w1, w3, w2)
    return _combine(h2, mi, mf[:TOP_K].T, tables, ys, g_final, final)


def kernel(x, w_in, w_out, lru_conv_w, lru_conv_b, lru_w_a, lru_b_a, lru_w_x, lru_b_x, lru_lambda,
           hgrn_lower_bounds, hgrn_norm_g, pool_w, pool_scale, sconv_w, norm_mix_g, norm_ffn_g,
           ffn_w_gate, ffn_w_up, ffn_w_down, moe_w_router, moe_w1, moe_w3, moe_w2, final_norm_g):
    B, S, D = x.shape
    depth = w_in.shape[0]

    def mix_params(layer):
        return _mix_params(norm_mix_g[layer], w_in[layer], w_out[layer], lru_conv_w[layer],
                           lru_conv_b[layer], lru_w_a[layer], lru_b_a[layer], lru_w_x[layer],
                           lru_b_x[layer], lru_lambda[layer], hgrn_lower_bounds, hgrn_norm_g[layer],
                           pool_w[layer], pool_scale[layer], sconv_w[layer])

    def ffn_layer(h2, layer):
        j = layer // 2
        final = layer == depth - 1
        if layer % 2 == 0:
            return _ffn_layer(h2, norm_ffn_g[layer], ffn_w_gate[j], ffn_w_up[j], ffn_w_down[j],
                              final_norm_g, final)
        return _moe_layer(h2, norm_ffn_g[layer], moe_w_router[j], moe_w1[j], moe_w3[j], moe_w2[j],
                          final_norm_g, final)

    h = x
    for layer in range(depth):
        h = _mix_layer(layer, h, mix_params(layer))
        h = ffn_layer(h.reshape(B * S, D), layer).reshape(B, S, D)
    return h
```

```python
import functools

import jax
import jax.numpy as jnp
from jax import lax
from jax.experimental import pallas as pl
from jax.experimental.pallas import tpu as pltpu

F32 = jnp.float32
BF16 = jnp.bfloat16
I32 = jnp.int32

GROUP_WIDTH = 256
N_GROUP_HEADS = 4
HEAD_DIM = 64
N_IN_SLICES = 10
LRU_CONV = 4
LRU_C = 8.0
HGRN_CHUNK = 64
POOL_WINDOWS = (2, 4, 8, 16)
SCONV_WIDTH = 3
TOP_K = 2
NORM_EPS = 1e-6
HEAD_NORM_EPS = 1e-5
LOG2_E = 1.4426950408889634

V7X_LANES = 128
V7X_SUBLANES = 8
V7X_VMEM_BYTES = 64 * 1024 * 1024

MIX_TILE = 512
HALO = 16
DIAG = 8
FFN_TILE = 512
FFN_CHUNK = 256
MOE_TILE = 256
ROW_PAD = V7X_SUBLANES
EXPERT_ROWS = 2560
KERNEL_VMEM_BYTES = V7X_VMEM_BYTES * 7 // 8
EXPERT_SUB = 256
EXPERT_FULL = 512
EXPERT_FF = 512


def _dot(a, b):
    return jnp.dot(a, b, preferred_element_type=F32)


def _dot_nt(a, b):
    return lax.dot_general(a, b, (((1,), (1,)), ((), ())), preferred_element_type=F32)


def _dot_tn(a, b):
    return lax.dot_general(a, b, (((0,), (0,)), ((), ())), preferred_element_type=F32)


def _rms_norm(x, g):
    ms = jnp.mean(x * x, axis=-1, keepdims=True)
    return x * lax.rsqrt(ms + NORM_EPS) * g


def _split3(x):
    hi = x.astype(BF16)
    r1 = x - hi.astype(F32)
    mid = r1.astype(BF16)
    lo = (r1 - mid.astype(F32)).astype(BF16)
    return hi, mid, lo


N_MIX_PARAMS = 16
N_MIX_SCRATCH = 16


def _mix_reset(ti, scratch):
    (proj_s, ax_s, p_s, sc_s, sa_s, su_s, ta_s, tu_s, tp_s, lruh_s, st_s, q_s, k_s, lf_s, o_s,
     y_s) = scratch
    GW = GROUP_WIDTH

    @pl.when(ti == 0)
    def _():
        ax_s[0:HALO, :] = jnp.zeros((HALO, GW), F32)
        p_s[0:HALO, :] = jnp.zeros((HALO, GW), F32)
        sc_s[0:HALO, :] = jnp.zeros((HALO, GW), F32)
        sa_s[0:V7X_SUBLANES, :] = jnp.ones((V7X_SUBLANES, GW), F32)
        su_s[0:V7X_SUBLANES, :] = jnp.zeros((V7X_SUBLANES, GW), F32)
        ta_s[0:V7X_SUBLANES, :] = jnp.zeros((V7X_SUBLANES, GW), F32)
        tu_s[0:V7X_SUBLANES, :] = jnp.zeros((V7X_SUBLANES, GW), F32)
        tp_s[0:HALO, :] = jnp.zeros((HALO, GW), F32)
        lruh_s[...] = jnp.zeros(lruh_s.shape, F32)
        st_s[...] = jnp.zeros(st_s.shape, F32)


def _mix_tile(layer, x, ti, params, scratch):
    (gmix_ref, w_in_ref, w_out_ref, cw_ref, cb_ref, wg_ref, bg_ref, lam_ref, lbraw_ref, hg_ref,
     wp_ref, ps_ref, sw_ref, ones_ref, tri_ref, lvl_ref) = params
    (proj_s, ax_s, p_s, sc_s, sa_s, su_s, ta_s, tu_s, tp_s, lruh_s, st_s, q_s, k_s, lf_s, o_s,
     y_s) = scratch
    T = x.shape[0]
    GW = GROUP_WIDTH
    hn = _rms_norm(x, gmix_ref[...]).astype(BF16)
    proj_s[...] = _dot(hn, w_in_ref[...])

    def sl(i):
        return proj_s[:, i * GW:(i + 1) * GW]

    ax_s[HALO:HALO + T, :] = sl(0)
    a_in = cb_ref[...]
    for kk in range(LRU_CONV):
        off = HALO - (LRU_CONV - 1) + kk
        a_in = a_in + cw_ref[kk:kk + 1, :] * ax_s[off:off + T, :]
    gates = _dot(a_in.astype(BF16), wg_ref[...]) + bg_ref[...]
    r_gate = jax.nn.sigmoid(gates[:, :GW])
    i_gate = jax.nn.sigmoid(gates[:, GW:])
    a = jnp.exp2(r_gate * ((-LRU_C * LOG2_E) * jax.nn.softplus(-lam_ref[...])))
    mult = jnp.sqrt(1.0 - a * a)
    u = mult * i_gate * a_in
    SUB = V7X_SUBLANES
    sa_s[SUB:SUB + T, :] = a
    su_s[SUB:SUB + T, :] = u
    a = sa_s[...]
    u = su_s[...]
    d = 1
    while d < SUB:
        ta_s[SUB:, :] = a
        tu_s[SUB:, :] = u
        a_sh = ta_s[SUB - d:SUB - d + SUB + T, :]
        u_sh = tu_s[SUB - d:SUB - d + SUB + T, :]
        u = a * u_sh + u
        a = a * a_sh
        d *= 2
    sa_s[0:SUB, :] = sa_s[T:T + SUB, :]
    su_s[0:SUB, :] = su_s[T:T + SUB, :]
    h_grp = lruh_s[...]
    groups = []
    for gi in range(1, T // SUB + 1):
        h_grp = a[gi * SUB:(gi + 1) * SUB, :] * h_grp + u[gi * SUB:(gi + 1) * SUB, :]
        groups.append(h_grp)
    lruh_s[...] = h_grp
    h_lru = jnp.concatenate(groups, axis=0)
    y_a = h_lru * jax.nn.gelu(sl(1))
    y_s[:, 0:GW] = y_a.astype(BF16)
    ax_s[0:HALO, :] = ax_s[T:T + HALO, :]

    p = sl(6)
    p_s[HALO:HALO + T, :] = p
    lane = lax.broadcasted_iota(I32, (1, GW), 1)
    win = jnp.where(lane < HEAD_DIM, float(POOL_WINDOWS[0]),
                    jnp.where(lane < 2 * HEAD_DIM, float(POOL_WINDOWS[1]),
                              jnp.where(lane < 3 * HEAD_DIM, float(POOL_WINDOWS[2]),
                                        float(POOL_WINDOWS[3]))))
    sums = []
    acc = p_s[...]
    d = 1
    while d < max(POOL_WINDOWS):
        tp_s[HALO:, :] = acc
        acc = acc + tp_s[HALO - d:HALO - d + HALO + T, :]
        sums.append(acc[HALO:, :])
        d *= 2
    wsum = jnp.where(lane < HEAD_DIM, sums[0],
                     jnp.where(lane < 2 * HEAD_DIM, sums[1],
                               jnp.where(lane < 3 * HEAD_DIM, sums[2], sums[3])))
    pos =(ti * T + 1 + lax.broadcasted_iota(I32, (T, 1), 0)).astype(F32)
    dpool = wsum / jnp.minimum(pos, win) - p
    y_c = _dot(dpool.astype(BF16), wp_ref[...]) * ps_ref[...]
    y_s[:, 2 * GW:3 * GW] = y_c.astype(BF16)
    p_s[0:HALO, :] = p_s[T:T + HALO, :]

    sc_s[HALO:HALO + T, :] = sl(8) * sl(9)
    conv = jnp.zeros((T, GW), F32)
    for kk in range(SCONV_WIDTH):
        off = HALO - (SCONV_WIDTH - 1) + kk
        conv = conv + sw_ref[kk:kk + 1, :] * sc_s[off:off + T, :]
    y_s[:, 3 * GW:4 * GW] = (sl(7) * conv).astype(BF16)
    sc_s[0:HALO, :] = sc_s[T:T + HALO, :]

    lbraw = lbraw_ref[...]
    e_lb = jnp.exp(lbraw - jnp.max(lbraw, axis=0, keepdims=True))
    p_lb = e_lb / jnp.sum(e_lb, axis=0, keepdims=True)
    lb = jnp.zeros((1, GW), F32)
    for li in range(1, layer + 1):
        lb = lb + p_lb[li:li + 1, :]
    z = sl(3)
    lf_s[...] = jnp.log2(lb + (1.0 - lb) * jax.nn.sigmoid(z))
    k_s[...] = (1.0 - lb) * jax.nn.sigmoid(-z)
    q_s[...] = jax.nn.silu(sl(2))

    L = HGRN_CHUNK
    ones_bd = ones_ref[...]
    ones_f = ones_bd.astype(F32)
    tri = tri_ref[...]
    widths = _level_widths()
    row_l = lax.broadcasted_iota(I32, (L, 1), 0)
    row_d = lax.broadcasted_iota(I32, (DIAG, 1), 0)

    def chunk_body(c, carry):
        r0 = c * L
        lf = lf_s[pl.ds(r0, L), :]
        hi, mid, lo = _split3(lf)
        G = _dot(tri, hi) + _dot(tri, mid) + _dot(tri, lo)

        def g_row(r):
            return G[r:r + 1, :]

        q =q_s[pl.ds(r0, L), :]
        k = k_s[pl.ds(r0, L), :]
        v = proj_s[pl.ds(r0, L), 4 * GW:5 * GW]
        vb = v.astype(BF16)
        g_last = g_row(L - 1)
        st = st_s[...]

        o = _dot_nt((q * jnp.exp2(G)).astype(BF16), st.astype(BF16))

        scores = jnp.zeros((N_GROUP_HEADS * L, L), F32)
        for li, w in enumerate(widths):
            gref = g_row(w - 1)
            for pair in range(1, L // (2 * w)):
                gref = jnp.where(row_l < pair * 2 * w, gref, g_row(pair * 2 * w + w - 1))
            qt = q * jnp.exp2(jnp.minimum(G - gref, 0.0))
            kt = k * jnp.exp2(jnp.minimum(gref - G, 0.0))
            q4 = jnp.concatenate([qt.astype(BF16)] * N_GROUP_HEADS, axis=0) * ones_bd
            s4 = _dot_nt(q4, kt.astype(BF16))
            scores = scores + s4 * lvl_ref[li]
        o4 = _dot(scores.astype(BF16), vb) * ones_f
        for hh in range(N_GROUP_HEADS):
            o = o + o4[hh * L:(hh + 1) * L, :]

        o_diag = []
        for b in range(L // DIAG):
            gs = G[b * DIAG:(b + 1) * DIAG, :]
            qs = q[b * DIAG:(b + 1) * DIAG, :]
            ks = k[b * DIAG:(b + 1) * DIAG, :]
            vs = v[b * DIAG:(b + 1) * DIAG, :]
            terms = []
            for j in range(DIAG):
                dg = jnp.where(row_d >= j, gs - gs[j:j + 1, :], -jnp.inf)
                terms.append((qs * (ks[j:j + 1, :] * jnp.exp2(dg))).astype(BF16))
            e = jnp.concatenate(terms, axis=0)
            pd = _dot(e, ones_bd)
            od = jnp.zeros((DIAG, GW), F32)
            for j in range(DIAG):
                od = od + pd[j * DIAG:(j + 1) * DIAG, :] * vs[j:j + 1, :]
            o_diag.append(od)
        o = o + jnp.concatenate(o_diag, axis=0)
        o_s[pl.ds(r0, L), :] = o

        kd = (k * jnp.exp2(g_last - G)).astype(BF16)
        st_s[...] = st * jnp.exp2(g_last) + _dot_tn(vb, kd) * ones_f
        return carry

    for c in range(T // L):
        chunk_body(c, 0)

    o = o_s[...]
    ms = _dot((o * o).astype(BF16), ones_bd) * (1.0 / HEAD_DIM)
    o = o * lax.rsqrt(ms + HEAD_NORM_EPS) * hg_ref[...]
    y_s[:, GW:2 * GW] = (o * jax.nn.silu(sl(5))).astype(BF16)

    return x + _dot(y_s[...], w_out_ref[...])


def _mix_kernel(layer, h_ref, *refs):
    params = refs[:N_MIX_PARAMS]
    out_ref = refs[N_MIX_PARAMS]
    scratch = refs[N_MIX_PARAMS + 1:]
    _mix_reset(pl.program_id(1), scratch)
    out_ref[...] = _mix_tile(layer, h_ref[...], pl.program_id(1), params, scratch)


def _level_widths():
    widths = []
    w = DIAG
    while w < HGRN_CHUNK:
        widths.append(w)
        w *= 2
    return widths


def _level_masks():
    L = HGRN_CHUNK
    t = jnp.arange(N_GROUP_HEADS * L)[:, None] % L
    s = jnp.arange(L)[None, :]
    return jnp.stack([(((t // w) % 2 == 1) & (s // w == t // w - 1)).astype(F32)
                      for w in _level_widths()])


def _block_diag(w):
    return jax.scipy.linalg.block_diag(*[w[i] for i in range(w.shape[0])])


def _const_spec(shape):
    nd = len(shape)
    return pl.BlockSpec(shape, lambda *_: (0,) * nd)


def _mix_params(gmix, w_in, w_out, conv_w, conv_b, w_a, b_a, w_x, b_x, lam, lb_raw, norm_g,
                pool_w, pool_scale, sconv_w):
    GW = GROUP_WIDTH
    assert w_in.shape[1] == N_IN_SLICES * GW and POOL_WINDOWS == (2, 4, 8, 16)
    wg = jnp.concatenate([_block_diag(w_a), _block_diag(w_x)], axis=1).astype(BF16)
    bg = jnp.concatenate([b_a, b_x])[None, :]
    head_of = jnp.arange(GW) // HEAD_DIM
    ones_bd = (head_of[:, None] == head_of[None, :]).astype(BF16)
    tri = jnp.tril(jnp.ones((HGRN_CHUNK, HGRN_CHUNK), BF16))
    params = [gmix[None, :], w_in.astype(BF16), w_out.astype(BF16), conv_w, conv_b[None, :], wg, bg,
              lam[None, :], lb_raw, jnp.tile(norm_g, N_GROUP_HEADS)[None, :],
              _block_diag(pool_w).astype(BF16), pool_scale[None, :], sconv_w, ones_bd, tri,
              _level_masks()]
    assert len(params) == N_MIX_PARAMS
    return params


def _mix_scratch(T):
    GW = GROUP_WIDTH
    scratch = [
        pltpu.VMEM((T, N_IN_SLICES * GW), F32),
        pltpu.VMEM((HALO + T, GW), F32),
        pltpu.VMEM((HALO + T, GW), F32),
        pltpu.VMEM((HALO + T, GW), F32),
        pltpu.VMEM((V7X_SUBLANES + T, GW), F32),
        pltpu.VMEM((V7X_SUBLANES + T, GW), F32),
        pltpu.VMEM((2 * V7X_SUBLANES + T, GW), F32),
        pltpu.VMEM((2 * V7X_SUBLANES + T, GW), F32),
        pltpu.VMEM((2 * HALO + T, GW), F32),
        pltpu.VMEM((V7X_SUBLANES, GW), F32),
        pltpu.VMEM((GW, GW), F32),
        pltpu.VMEM((T, GW), F32),
        pltpu.VMEM((T, GW), F32),
        pltpu.VMEM((T, GW), F32),
        pltpu.VMEM((T, GW), F32),
        pltpu.VMEM((T, N_GROUP_HEADS * GW), BF16),
    ]
    assert len(scratch) == N_MIX_SCRATCH
    return scratch


def _mix_layer(layer, h, params):
    B, S, D = h.shape
    T = min(MIX_TILE, S)
    assert S % T == 0 and T % HGRN_CHUNK == 0
    in_specs = [pl.BlockSpec((None, T, D), lambda b, t: (b, t, 0))]
    in_specs += [_const_spec(a.shape) for a in params]
    return pl.pallas_call(
        functools.partial(_mix_kernel, layer),
        out_shape=jax.ShapeDtypeStruct((B, S, D), F32),
        grid=(B, S // T),
        in_specs=in_specs,
        out_specs=pl.BlockSpec((None, T, D), lambda b, t: (b, t, 0)),
        scratch_shapes=_mix_scratch(T),
        compiler_params=pltpu.CompilerParams(
            dimension_semantics=("arbitrary", "arbitrary"),
            vmem_limit_bytes=KERNEL_VMEM_BYTES),
        name=f"mix{layer}",
    )(h, *params)


def _ffn_tile(x, g_ref, wg_ref, wu_ref, wd_ref, act_s, chunk_ready=None):
    hn = _rms_norm(x, g_ref[...]).astype(BF16)
    ff = wg_ref.shape[1]
    for ci, c in enumerate(range(0, ff, FFN_CHUNK)):
        if chunk_ready is not None:
            chunk_ready(ci)
        gate = _dot(hn, wg_ref[:, c:c + FFN_CHUNK])
        up = _dot(hn, wu_ref[:, c:c + FFN_CHUNK])
        act_s[:, c:c + FFN_CHUNK] = (jax.nn.silu(gate) * up).astype(BF16)
    return x + _dot(act_s[...], wd_ref[...])


def _ffn_kernel(final, h_ref, g_ref, gf_ref, wg_hbm, wu_hbm, wd_hbm, out_ref,
                wg_s, wu_s, wd_s, stage_in_s, stage_out_s, sem, act_s):
    n_chunks = wg_s.shape[1] // FFN_CHUNK

    def chunk_copies(ci, slot):
        cols = pl.ds(ci * FFN_CHUNK, FFN_CHUNK)
        return (pltpu.make_async_copy(wg_hbm.at[:, cols], stage_in_s.at[slot, 0], sem.at[slot]),
                pltpu.make_async_copy(wu_hbm.at[:, cols], stage_in_s.at[slot, 1], sem.at[slot]),
                pltpu.make_async_copy(wd_hbm.at[cols, :], stage_out_s.at[slot], sem.at[slot]))

    def load_chunk(ci):
        slot = ci % 2
        if ci == 0:
            for cp in chunk_copies(0, 0):
                cp.start()
        if ci + 1 < n_chunks:
            for cp in chunk_copies(ci + 1, 1 - slot):
                cp.start()
        for cp in chunk_copies(ci, slot):
            cp.wait()
        c = ci * FFN_CHUNK
        wg_s[:, c:c + FFN_CHUNK] = stage_in_s[slot, 0].astype(BF16)
        wu_s[:, c:c + FFN_CHUNK] = stage_in_s[slot, 1].astype(BF16)
        wd_s[c:c + FFN_CHUNK, :] = stage_out_s[slot].astype(BF16)

    def run(chunk_ready):
        y = _ffn_tile(h_ref[...], g_ref, wg_s, wu_s, wd_s, act_s, chunk_ready)
        out_ref[...] = _rms_norm(y, gf_ref[...]) if final else y

    pl.when(pl.program_id(0) == 0)(functools.partial(run, load_chunk))
    pl.when(pl.program_id(0) > 0)(functools.partial(run, None))


def _ffn_layer(h2, g, w_gate, w_up, w_down, g_final, final):
    n, d = h2.shape
    ff = w_gate.shape[1]
    tm = min(FFN_TILE, n)
    assert n % tm == 0 and ff % FFN_CHUNK == 0
    hbm = pl.BlockSpec(memory_space=pl.ANY)
    return pl.pallas_call(
        functools.partial(_ffn_kernel, final),
        out_shape=jax.ShapeDtypeStruct((n, d), F32),
        grid=(n // tm,),
        in_specs=[pl.BlockSpec((tm, d), lambda i: (i, 0)),
                  _const_spec((1, d)),
                  _const_spec((1, d)),
                  hbm, hbm, hbm],
        out_specs=pl.BlockSpec((tm, d), lambda i: (i, 0)),
        scratch_shapes=[pltpu.VMEM((d, ff), BF16), pltpu.VMEM((d, ff), BF16),
                        pltpu.VMEM((ff, d), BF16),
                        pltpu.VMEM((2, 2, d, FFN_CHUNK), F32),
                        pltpu.VMEM((2, FFN_CHUNK, d), F32),
                        pltpu.SemaphoreType.DMA((2,)),
                        pltpu.VMEM((tm, ff), BF16)],
        compiler_params=pltpu.CompilerParams(
            dimension_semantics=("arbitrary",),
            vmem_limit_bytes=KERNEL_VMEM_BYTES),
        name="ffn",
    )(h2, g[None, :], g_final[None, :], w_gate, w_up, w_down)


def _router_kernel(h_ref, g_ref, wr_ref, triu_ref, hn_ref, mi_ref, mf_ref, cnt_ref):
    n_e = wr_ref.shape[0]
    tr = h_ref.shape[0]
    hn = _rms_norm(h_ref[...], g_ref[...])
    xh = hn.astype(BF16)
    hn_ref[...] = xh
    xm = (hn - xh.astype(F32)).astype(BF16)
    w = wr_ref[...]
    wh = w.astype(BF16)
    wm = (w - wh.astype(F32)).astype(BF16)
    logits = _dot_nt(wh, xh) + _dot_nt(wh, xm) + _dot_nt(wm, xh)
    eid = lax.broadcasted_iota(I32, (n_e, tr), 0)
    m1 = jnp.max(logits, axis=0, keepdims=True)
    e1 = jnp.min(jnp.where(logits == m1, eid, n_e), axis=0, keepdims=True)
    rest = jnp.where(eid == e1, -jnp.inf, logits)
    m2 = jnp.max(rest, axis=0, keepdims=True)
    e2 = jnp.min(jnp.where(rest == m2, eid, n_e), axis=0, keepdims=True)
    ex = jnp.exp(m2 - m1)
    g1 = 1.0 / (1.0 + ex)
    g2 = ex / (1.0 + ex)
    member = jnp.where((eid == e1) | (eid == e2), 1.0, 0.0)
    incl = _dot(member.astype(BF16), triu_ref[...])
    rank = incl - member
    r1 = jnp.sum(jnp.where(eid == e1, rank, 0.0), axis=0, keepdims=True)
    r2 = jnp.sum(jnp.where(eid == e2, rank, 0.0), axis=0, keepdims=True)
    cnt_ref[...] = jnp.broadcast_to(incl[:, tr - 1:tr], cnt_ref.shape).astype(I32)
    zi = jnp.zeros((1, tr), I32)
    mi_ref[...] = jnp.concatenate(
        [e1, e2, r1.astype(I32), r2.astype(I32), zi, zi, zi, zi], axis=0)
    zf = jnp.zeros((1, tr), F32)
    mf_ref[...] = jnp.concatenate([g1, g2, zf, zf, zf, zf, zf, zf], axis=0)


def _router(h2, g, w_router):
    n, d = h2.shape
    n_e = w_router.shape[1]
    tr = min(MOE_TILE, n)
    assert n % tr == 0 and n_e == V7X_SUBLANES
    tok = jnp.arange(tr)
    triu = (tok[:, None] <= tok[None, :]).astype(BF16)
    return pl.pallas_call(
        _router_kernel,
        out_shape=(jax.ShapeDtypeStruct((n, d), BF16),
                   jax.ShapeDtypeStruct((V7X_SUBLANES, n), I32),
                   jax.ShapeDtypeStruct((V7X_SUBLANES, n), F32),
                   jax.ShapeDtypeStruct((n // tr, n_e, V7X_LANES), I32)),
        grid=(n // tr,),
        in_specs=[pl.BlockSpec((tr, d), lambda i: (i, 0)),
                  _const_spec((1, d)),
                  _const_spec((n_e, d)),
                  _const_spec((tr, tr))],
        out_specs=(pl.BlockSpec((tr, d), lambda i: (i, 0)),
                   pl.BlockSpec((V7X_SUBLANES, tr), lambda i: (0, i)),
                   pl.BlockSpec((V7X_SUBLANES, tr), lambda i: (0, i)),
                   pl.BlockSpec((None, n_e, V7X_LANES), lambda i: (i, 0, 0))),
        compiler_params=pltpu.CompilerParams(dimension_semantics=("arbitrary",)),
        name="router",
    )(h2, g[None, :], w_router.T, triu)


U32 = jnp.uint32
HIGH_HALF = 0xFFFF0000


def _pack_bf16_pairs(x):
    half = x.shape[1] // 2
    lo = lax.bitcast_convert_type(x[:, :half], U32)
    hi = lax.bitcast_convert_type(x[:, half:], U32)
    return (hi & U32(HIGH_HALF)) | (lo >> U32(16))


def _unpack_bf16_pairs(w):
    lo = lax.bitcast_convert_type(w << U32(16), F32)
    hi = lax.bitcast_convert_type(w & U32(HIGH_HALF), F32)
    return jnp.concatenate([lo, hi], axis=1).astype(BF16)


def _segment_copies(step, slot, tables, n_e, tile, local_ref, rows_ref, sem, to_rows, start):
    off_ref, row_ref, pad_ref = tables

    def copy(off, row, size):
        local = local_ref.at[slot, pl.ds(pl.multiple_of(off, ROW_PAD), size)]
        remote = rows_ref.at[pl.ds(pl.multiple_of(row, ROW_PAD), size)]
        return (pltpu.make_async_copy(local, remote, sem.at[slot]) if to_rows
                else pltpu.make_async_copy(remote, local, sem.at[slot]))

    if not start:
        last = step * n_e + n_e - 1
        total = off_ref[last] + pad_ref[last]
        size = TOP_K * tile
        while size >= ROW_PAD:
            pl.when((total & size) != 0)(copy(0, 0, size).wait)
            size //= 2
        return
    for e in range(n_e):
        off = off_ref[step * n_e + e]
        row = row_ref[step * n_e + e]
        pad = pad_ref[step * n_e + e]
        size = tile
        while size >= ROW_PAD:
            done = pad & (-2 * size)
            pl.when((pad & size) != 0)(copy(off + done, row + done, size).start)
            size //= 2


def _stack_rows(mi, off_ref, step, n_e):
    e1, e2, row1, row2 = mi[0:1, :], mi[1:2, :], mi[2:3, :], mi[3:4, :]
    for e in range(n_e):
        off = off_ref[step * n_e + e]
        row1 = row1 + jnp.where(e1 == e, off, 0)
        row2 = row2 + jnp.where(e2 == e, off, 0)
    return row1, row2


def _fill_copies(fill_ref, n_e, zero_s, xs_ref, sem, start):
    zrows = zero_s.shape[0]
    rb = EXPERT_ROWS

    def go(cp):
        if start:
            cp.start()
        else:
            cp.wait()

    def zero_copy(row, size):
        return pltpu.make_async_copy(
            zero_s.at[pl.ds(0, size)], xs_ref.at[pl.ds(pl.multiple_of(row, ROW_PAD), size)], sem)

    for e in range(n_e):
        row = fill_ref[e]
        gap = fill_ref[n_e + e]
        size = zrows
        while size >= ROW_PAD:
            done = gap & (-2 * size)
            pl.when((gap & size) != 0)(functools.partial(go, zero_copy(row + done, size)))
            size //= 2

    def block(b, c):
        done = 0
        size = zrows
        while done < rb:
            if size <= rb - done:
                go(zero_copy(b * rb + done, size))
                done += size
            else:
                size //= 2
        return c

    lax.fori_loop(fill_ref[2 * n_e], xs_ref.shape[0] // rb, block, 0)


def _dispatch_kernel(nt, off_ref, row_ref, pad_ref, fill_ref, hn_ref, mi_ref, xs_ref, slab_s, zero_s,
                     sem, fill_sem):
    i = pl.program_id(0)
    tile = hn_ref.shape[0]
    n_e = V7X_SUBLANES
    r_stack = slab_s.shape[1]
    slot = lax.rem(i, 2)
    copies = functools.partial(_segment_copies, tables=(off_ref, row_ref, pad_ref), n_e=n_e,
                               tile=tile, local_ref=slab_s, rows_ref=xs_ref, sem=sem, to_rows=True)

    @pl.when(i == 0)
    def _():
        zero_s[...] = jnp.zeros(zero_s.shape, U32)
        _fill_copies(fill_ref, n_e, zero_s, xs_ref, fill_sem, start=True)

    @pl.when(i >= 2)
    def _():
        copies(i - 2, slot, start=False)

    row1, row2 = _stack_rows(mi_ref[...], off_ref, i, n_e)
    rid = lax.broadcasted_iota(I32, (r_stack, tile), 0)
    sel = jnp.where(rid == row1, 1.0, jnp.where(rid == row2, 1.0, 0.0)).astype(BF16)
    slab_s[slot] = _pack_bf16_pairs(_dot(sel, hn_ref[...]))
    copies(i, slot, start=True)

    @pl.when(i == nt - 1)
    def _():
        if nt > 1:
            copies(i - 1, 1 - slot, start=False)
        copies(i, slot, start=False)
        _fill_copies(fill_ref, n_e, zero_s, xs_ref, fill_sem, start=False)


def _dispatch(hn, mi, tables, fill, n_rows):
    n, d = hn.shape
    tile = min(MOE_TILE, n)
    nt = n // tile
    r_stack = TOP_K * tile + V7X_SUBLANES * ROW_PAD
    return pl.pallas_call(
        functools.partial(_dispatch_kernel, nt),
        out_shape=jax.ShapeDtypeStruct((n_rows, d // 2), U32),
        grid_spec=pltpu.PrefetchScalarGridSpec(
            num_scalar_prefetch=4,
            grid=(nt,),
            in_specs=[pl.BlockSpec((tile, d), lambda i, *_: (i, 0)),
                      pl.BlockSpec((V7X_SUBLANES, tile), lambda i, *_: (0, i))],
            out_specs=pl.BlockSpec(memory_space=pl.ANY),
            scratch_shapes=[pltpu.VMEM((2, r_stack, d // 2), U32),
                            pltpu.VMEM((pl.next_power_of_2(EXPERT_ROWS) // 2, d // 2), U32),
                            pltpu.SemaphoreType.DMA((2,)), pltpu.SemaphoreType.DMA(())]),
        compiler_params=pltpu.CompilerParams(dimension_semantics=("arbitrary",),
                                             has_side_effects=True),
        name="dispatch",
    )(*tables, fill, hn, mi)


def _expert_kernel(be_ref, rows_ref, src_ref, x_ref, w1_ref, w3_ref, w2_ref, out_ref,
                   w1_s, w3_s, w2_s):
    del be_ref, src_ref
    b = pl.program_id(0)
    f = pl.program_id(1)
    rows = rows_ref[b]

    @pl.when(f == 0)
    def _():
        out_ref[...] = jnp.zeros(out_ref.shape, F32)

    def cast_weights():
        w1_s[...] = w1_ref[...].astype(BF16)
        w3_s[...] = w3_ref[...].astype(BF16)
        w2_s[...] = w2_ref[...].astype(BF16)

    def swiglu_rows(r0, n_rows):
        xb = _unpack_bf16_pairs(x_ref[pl.ds(r0, n_rows), :])
        gate = _dot(xb, w1_s[...])
        up = _dot(xb, w3_s[...])
        act = (jax.nn.silu(gate) * up).astype(BF16)
        out_ref[pl.ds(r0, n_rows), :] += _dot(act, w2_s[...])

    @pl.when(rows == EXPERT_ROWS)
    def _():
        cast_weights()
        for s in range(EXPERT_ROWS // EXPERT_FULL):
            swiglu_rows(s * EXPERT_FULL, EXPERT_FULL)

    @pl.when((rows > 0) & (rows < EXPERT_ROWS))
    def _():
        n_sub = (rows + EXPERT_SUB - 1) // EXPERT_SUB
        n_pair = n_sub // 2

        @pl.when(n_pair >= 2)
        def _():
            cast_weights()
            swiglu_rows(0, EXPERT_FULL)
            swiglu_rows(EXPERT_FULL, EXPERT_FULL)

        @pl.when(n_pair < 2)
        def _():
            cast_weights()

            @pl.when(n_pair == 1)
            def _():
                swiglu_rows(0, EXPERT_FULL)

        def pair(s, c):
            swiglu_rows(pl.multiple_of(s * EXPERT_FULL, EXPERT_FULL), EXPERT_FULL)
            return c

        lax.fori_loop(2, n_pair, pair, 0)

        @pl.when(n_sub % 2 == 1)
        def _():
            swiglu_rows(pl.multiple_of((n_sub - 1) * EXPERT_SUB, EXPERT_SUB), EXPERT_SUB)


def _experts(xs, blk_e, blk_rows, blk_src, w1, w3, w2):
    n_rows = xs.shape[0]
    n_e, d, ff = w1.shape
    assert xs.shape[1] * 2 == d
    rb = EXPERT_ROWS
    fft = min(EXPERT_FF, ff)
    assert n_rows % rb == 0 and ff % fft == 0
    assert EXPERT_FULL == 2 * EXPERT_SUB and rb % EXPERT_FULL == 0
    nf = ff // fft
    nb = n_rows // rb

    def f_eff(b, f, rows):
        return jnp.where(rows[b] > 0, f, nf - 1)

    return pl.pallas_call(
        _expert_kernel,
        out_shape=jax.ShapeDtypeStruct((n_rows, d), F32),
        grid_spec=pltpu.PrefetchScalarGridSpec(
            num_scalar_prefetch=3,
            grid=(nb, nf),
            in_specs=[
                pl.BlockSpec((rb, d // 2), lambda b, f, be, rows, src: (src[b], 0)),
                pl.BlockSpec((None, d, fft), lambda b, f, be, rows, src: (be[b], 0, f_eff(b, f, rows))),
                pl.BlockSpec((None, d, fft), lambda b, f, be, rows, src: (be[b], 0, f_eff(b, f, rows))),
                pl.BlockSpec((None, fft, d), lambda b, f, be, rows, src: (be[b], f_eff(b, f, rows), 0)),
            ],
            out_specs=pl.BlockSpec((rb, d), lambda b, f, be, rows, src: (b, 0)),
            scratch_shapes=[pltpu.VMEM((d, fft), BF16), pltpu.VMEM((d, fft), BF16),
                            pltpu.VMEM((fft, d), BF16)]),
        compiler_params=pltpu.CompilerParams(
            dimension_semantics=("arbitrary", "arbitrary"),
            vmem_limit_bytes=KERNEL_VMEM_BYTES),
        name="experts",
    )(blk_e, blk_rows, blk_src, xs, w1, w3, w2)


def _combine_kernel(final, nt, off_ref, row_ref, pad_ref, h_ref, mi_ref, gt_ref, g_ref, ys_ref,
                    out_ref, stack_s, sem):
    i = pl.program_id(0)
    tile = h_ref.shape[0]
    n_e = V7X_SUBLANES
    r_stack = stack_s.shape[1]
    slot = lax.rem(i, 2)
    copies = functools.partial(_segment_copies, tables=(off_ref, row_ref, pad_ref), n_e=n_e,
                               tile=tile, local_ref=stack_s, rows_ref=ys_ref, sem=sem, to_rows=False)

    @pl.when(i == 0)
    def _():
        stack_s[...] = jnp.zeros(stack_s.shape, F32)
        copies(i, slot, start=True)

    @pl.when(i + 1 < nt)
    def _():
        copies(i + 1, 1 - slot, start=True)

    copies(i, slot, start=False)
    y_rows = stack_s[slot].astype(BF16)
    row1, row2 = _stack_rows(mi_ref[...], off_ref, i, n_e)
    rid = lax.broadcasted_iota(I32, (r_stack, TOP_K * tile), 0)
    rows12 = jnp.concatenate([row1, row2], axis=1)
    y12 = _dot_tn(jnp.where(rid == rows12, 1.0, 0.0).astype(BF16), y_rows)
    gates = gt_ref[...]
    y = h_ref[...] + y12[:tile] * gates[:, 0:1] + y12[tile:] * gates[:, 1:2]
    out_ref[...] = _rms_norm(y, g_ref[...]) if final else y


def _combine(h2, mi, gates, tables, ys, g_final, final):
    n, d = h2.shape
    tile = min(MOE_TILE, n)
    nt = n // tile
    r_stack = TOP_K * tile + V7X_SUBLANES * ROW_PAD
    return pl.pallas_call(
        functools.partial(_combine_kernel, final, nt),
        out_shape=jax.ShapeDtypeStruct((n, d), F32),
        grid_spec=pltpu.PrefetchScalarGridSpec(
            num_scalar_prefetch=3,
            grid=(nt,),
            in_specs=[pl.BlockSpec((tile, d), lambda i, *_: (i, 0)),
                      pl.BlockSpec((V7X_SUBLANES, tile), lambda i, *_: (0, i)),
                      pl.BlockSpec((tile, TOP_K), lambda i, *_: (i, 0)),
                      pl.BlockSpec((1, d), lambda i, *_: (0, 0)),
                      pl.BlockSpec(memory_space=pl.ANY)],
            out_specs=pl.BlockSpec((tile, d), lambda i, *_: (i, 0)),
            scratch_shapes=[pltpu.VMEM((2, r_stack, d), F32), pltpu.SemaphoreType.DMA((2,))]),
        compiler_params=pltpu.CompilerParams(dimension_semantics=("arbitrary",)),
        name="combine",
    )(*tables, h2, mi, gates, g_final[None, :], ys)


def _moe_layer(h2, g, w_router, w1, w3, w2, g_final, final=True):
    n, d = h2.shape
    n_e = w_router.shape[1]
    rb = EXPERT_ROWS
    tile = min(MOE_TILE, n)
    nt = n // tile
    hn, mi, mf, cnt = _router(h2, g, w_router)
    pad = (cnt[:, :, 0] + ROW_PAD - 1) // ROW_PAD * ROW_PAD
    off = jnp.cumsum(pad, axis=1) - pad
    total = jnp.sum(pad, axis=0)
    padded = (total + rb - 1) // rb * rb
    pend = jnp.cumsum(padded)
    pstart = pend - padded
    row = pstart[None, :] + jnp.cumsum(pad, axis=0) - pad
    tables = tuple(t.reshape(-1).astype(I32) for t in (off, row, pad))
    nb = (n * TOP_K + nt * n_e * (ROW_PAD - 1)) // rb + n_e
    blk = jnp.arange(nb, dtype=I32)
    n_used = pend[-1] // rb
    last = jnp.maximum(n_used - 1, 0)
    blk_src = jnp.minimum(blk, last)
    blk_e = jnp.minimum(jnp.sum(blk_src[:, None] * rb >= pend[None, :], axis=1), n_e - 1).astype(I32)
    blk_rows = jnp.where(blk < n_used,
                         jnp.clip(total[blk_e] - (blk * rb - pstart[blk_e]), 0, rb), 0).astype(I32)
    fill = jnp.concatenate([pstart + total, padded - total, n_used[None]]).astype(I32)
    xs = _dispatch(hn, mi, tables, fill, nb * rb)
    ys = _experts(xs, blk_e, blk_rows, blk_src, w1, w3, w2)
    return _combine(h2, mi, mf[:TOP_K].T, tables, ys, g_final, final)


def kernel(x, w_in, w_out, lru_conv_w, lru_conv_b, lru_w_a, lru_b_a, lru_w_x, lru_b_x, lru_lambda,
           hgrn_lower_bounds, hgrn_norm_g, pool_w, pool_scale, sconv_w, norm_mix_g, norm_ffn_g,
           ffn_w_gate, ffn_w_up, ffn_w_down, moe_w_router, moe_w1, moe_w3, moe_w2, final_norm_g):
    B, S, D = x.shape
    depth = w_in.shape[0]

    def mix_params(layer):
        return _mix_params(norm_mix_g[layer], w_in[layer], w_out[layer], lru_conv_w[layer],
                           lru_conv_b[layer], lru_w_a[layer], lru_b_a[layer], lru_w_x[layer],
                           lru_b_x[layer], lru_lambda[layer], hgrn_lower_bounds, hgrn_norm_g[layer],
                           pool_w[layer], pool_scale[layer], sconv_w[layer])

    def ffn_layer(h2, layer):
        j = layer // 2
        final = layer == depth - 1
        if layer % 2 == 0:
            return _ffn_layer(h2, norm_ffn_g[layer], ffn_w_gate[j], ffn_w_up[j], ffn_w_down[j],
                              final_norm_g, final)
        return _moe_layer(h2, norm_ffn_g[layer], moe_w_router[j], moe_w1[j], moe_w3[j], moe_w2[j],
                          final_norm_g, final)

    h = x
    for layer in range(depth):
        h = _mix_layer(layer, h, mix_params(layer))
        h = ffn_layer(h.reshape(B * S, D), layer).reshape(B, S, D)
    return h
```
